```python
import jax, jax.numpy as jnp
from jax import lax
import numpy as np

D_MODEL = 1024
BATCH = 8
SEQ = 2048
DEPTH = 1
DEC_BATCH = 32
DEC_SEQ = 8
PAST_LEN = 16384
PAGE_SIZE = 128

CHUNK = 128
A_WIDTH = D_MODEL
A_GROUPS = 8
A_GROUP_DIM = A_WIDTH // A_GROUPS
HEAD_DIM = 64
B_SLOTS = 8
DILATED_PAIRS = ((128, 1), (512, 4), (2048, 16))
N_GROUPS_B = len(DILATED_PAIRS)
B_QKV = N_GROUPS_B * B_SLOTS * HEAD_DIM
B_OUT = B_SLOTS * HEAD_DIM
N_BRANCH = 2
IN_SIZES = (A_WIDTH, A_WIDTH, A_WIDTH, B_QKV, B_QKV, B_QKV, B_OUT, N_BRANCH * D_MODEL)
D_IN = sum(IN_SIZES)
SPLIT_IDX = tuple(int(i) for i in np.cumsum(IN_SIZES)[:-1])
EPS = 1e-6
NEG = -1e30

kernel_name = "hybrid_gmlp_dilated_attn_step"


def _rmsnorm(x, g):
    x32 = x.astype(jnp.float32)
    y = x32 * lax.rsqrt(jnp.mean(x32 * x32, axis=-1, keepdims=True) + EPS)
    return (y * g.astype(jnp.float32)).astype(x.dtype)


def _layernorm(x, g, b):
    x32 = x.astype(jnp.float32)
    mu = jnp.mean(x32, axis=-1, keepdims=True)
    var = jnp.mean(jnp.square(x32 - mu), axis=-1, keepdims=True)
    y = (x32 - mu) * lax.rsqrt(var + EPS)
    return (y * g.astype(jnp.float32) + b.astype(jnp.float32)).astype(x.dtype)


def _dilated_prompt(q, k, v, window, dilation):
    b, s, h, dh = q.shape
    n_back = window // dilation
    blk = n_back
    L = s // dilation
    nb = -(-L // blk)
    lp = nb * blk

    def to_phase(a):
        return a.reshape(b, L, dilation, h, dh).transpose(0, 2, 1, 3, 4)

    qp = jnp.pad(to_phase(q), ((0, 0), (0, 0), (0, lp - L), (0, 0), (0, 0)))
    pad_kv = ((0, 0), (0, 0), (blk, lp - L), (0, 0), (0, 0))
    kp = jnp.pad(to_phase(k), pad_kv)
    vp = jnp.pad(to_phase(v), pad_kv)

    def two_blocks(a):
        prev = a[:, :, :lp].reshape(b, dilation, nb, blk, h, dh)
        cur = a[:, :, blk:].reshape(b, dilation, nb, blk, h, dh)
        return jnp.concatenate([prev, cur], axis=3)

    kb, vb = two_blocks(kp), two_blocks(vp)
    qb = qp.reshape(b, dilation, nb, blk, h, dh)
    scores = jnp.einsum("bdnqhc,bdnkhc->bdnhqk", qb, kb).astype(jnp.float32) * (HEAD_DIM ** -0.5)
    qi = np.arange(lp).reshape(nb, blk, 1)
    ki = (np.arange(nb)[:, None, None] - 1) * blk + np.arange(2 * blk)[None, None, :]
    dist = qi - ki
    mask = (dist >= 0) & (dist <= n_back) & (ki >= 0)
    scores = jnp.where(mask[None, None, :, None], scores, NEG)
    m = jnp.max(scores, axis=-1)
    p = jnp.exp(scores - m[..., None])
    l = jnp.sum(p, axis=-1)
    acc = jnp.einsum("bdnhqk,bdnkhc->bdnqhc", p, vb.astype(jnp.float32))
    acc = acc.reshape(b, dilation, lp, h, dh)[:, :, :L].transpose(0, 2, 1, 3, 4).reshape(b, s, h, dh)

    def stat_back(a):
        a = a.transpose(0, 1, 2, 4, 3).reshape(b, dilation, lp, h)[:, :, :L]
        return a.transpose(0, 2, 1, 3).reshape(b, s, h)

    return acc, stat_back(m), stat_back(l)


def _dilated_sample(q, k, v, kv_cache, window, dilation):
    lw = kv_cache.shape[1]
    t = q.shape[1]
    n_keys = window // dilation + 1
    k_all = jnp.concatenate([kv_cache[:, :, 0], k], axis=1)
    v_all = jnp.concatenate([kv_cache[:, :, 1], v], axis=1)
    idx = lw + np.arange(t)[:, None] - dilation * np.arange(n_keys)[None, :]
    valid = idx >= 0
    idx = np.maximum(idx, 0)
    kg = k_all[:, idx]
    vg = v_all[:, idx]
    scores = jnp.einsum("bthc,btkhc->bthk", q, kg).astype(jnp.float32) * (HEAD_DIM ** -0.5)
    scores = jnp.where(valid[None, :, None, :], scores, NEG)
    m = jnp.max(scores, axis=-1)
    p = jnp.exp(scores - m[..., None])
    l = jnp.sum(p, axis=-1)
    acc = jnp.einsum("bthk,btkhc->bthc", p, vg.astype(jnp.float32))
    return acc, m, l


def _layer(x, c, kv_caches, w_cond, b_cond, g_pre, w_in, ln_v_g, ln_v_b,
           w_spatial, b_spatial, w_proj_a, w_proj_b, w_out, g_post):
    bsz, s, _ = x.shape
    mod = jax.nn.silu(c) @ w_cond + b_cond
    shift, scale, gate = jnp.split(mod, 3, axis=-1)
    h = _rmsnorm(x, g_pre) * (1 + scale[:, None, :]) + shift[:, None, :]
    proj = h @ w_in
    u_a, v_a, z_a, q, k, v, z_b, gate_logits = jnp.split(proj, SPLIT_IDX, axis=-1)

    v_n = _layernorm(v_a, ln_v_g, ln_v_b)
    causal = np.tril(np.ones((CHUNK, CHUNK), dtype=bool))
    w_sp = jnp.where(causal[None], w_spatial, 0)
    if kv_caches is None:
        vg = v_n.reshape(bsz, s // CHUNK, CHUNK, A_GROUPS, A_GROUP_DIM)
        zs = jnp.einsum("gts,bnsgc->bntgc", w_sp, vg) + b_spatial.T[None, None, :, :, None]
    else:
        vg = v_n.reshape(bsz, s, A_GROUPS, A_GROUP_DIM)
        zs = jnp.einsum("gts,bsgc->btgc", w_sp[:, :s, :s], vg) + b_spatial[:, :s].T[None, :, :, None]
    y_a = u_a * zs.reshape(bsz, s, A_WIDTH) * jax.nn.silu(z_a)

    q = q.reshape(bsz, s, N_GROUPS_B, B_SLOTS, HEAD_DIM)
    k = k.reshape(bsz, s, N_GROUPS_B, B_SLOTS, HEAD_DIM)
    v = v.reshape(bsz, s, N_GROUPS_B, B_SLOTS, HEAD_DIM)
    accs, ms, ls, kv_rows = [], [], [], []
    for gi, (window, dilation) in enumerate(DILATED_PAIRS):
        qg, kg, vg_ = q[:, :, gi], k[:, :, gi], v[:, :, gi]
        if kv_caches is None:
            acc, m, l = _dilated_prompt(qg, kg, vg_, window, dilation)
            kv_rows.append(jnp.stack([kg, vg_], axis=2)[:, s - min(window, s):])
        else:
            acc, m, l = _dilated_sample(qg, kg, vg_, kv_caches[gi], window, dilation)
            kv_rows.append(jnp.stack([kg, vg_], axis=2))
        accs.append(acc)
        ms.append(m)
        ls.append(l)
    ms = jnp.stack(ms)
    m_all = jnp.max(ms, axis=0)
    wts = jnp.exp(ms - m_all)
    den = jnp.sum(wts * jnp.stack(ls), axis=0)
    attn = jnp.sum(wts[..., None] * jnp.stack(accs), axis=0) / den[..., None]
    y_b = attn.reshape(bsz, s, B_OUT).astype(x.dtype) * jax.nn.silu(z_b)

    g_a, g_b = jnp.split(jax.nn.sigmoid(gate_logits), 2, axis=-1)
    merged = g_a * (y_a @ w_proj_a) + g_b * (y_b @ w_proj_b)
    out = merged @ w_out
    x_new = x + gate[:, None, :] * _rmsnorm(out, g_post)
    return x_new, kv_rows, v_n


def setup_inputs(seed: int = 0) -> dict:
    key = jax.random.key(seed)
    ks = jax.random.split(key, 24)
    f32 = jnp.float32

    def nrm(k_, shape, scale=1.0):
        return jax.random.normal(k_, shape, f32) * scale

    def cache_shape(window):
        return (DEPTH, DEC_BATCH, min(window, PAST_LEN), 2, B_SLOTS, HEAD_DIM)

    return {
        "x_prompt": nrm(ks[0], (BATCH, SEQ, D_MODEL)),
        "x_sample": nrm(ks[1], (DEC_BATCH, DEC_SEQ, D_MODEL)),
        "cache_kv_w128": nrm(ks[2], cache_shape(DILATED_PAIRS[0][0])),
        "cache_kv_w512": nrm(ks[3], cache_shape(DILATED_PAIRS[1][0])),
        "cache_kv_w2048": nrm(ks[4], cache_shape(DILATED_PAIRS[2][0])),
        "c_prompt": nrm(ks[5], (BATCH, D_MODEL)),
        "c_sample": nrm(ks[6], (DEC_BATCH, D_MODEL)),
        "w_cond": nrm(ks[7], (DEPTH, D_MODEL, 3 * D_MODEL), D_MODEL ** -0.5),
        "b_cond": nrm(ks[8], (DEPTH, 3 * D_MODEL), 0.02),
        "g_pre": 1.0 + nrm(ks[9], (DEPTH, D_MODEL), 0.02),
        "w_in": nrm(ks[10], (DEPTH, D_MODEL, D_IN), D_MODEL ** -0.5),
        "ln_v_g": 1.0 + nrm(ks[11], (DEPTH, A_WIDTH), 0.02),
        "ln_v_b": nrm(ks[12], (DEPTH, A_WIDTH), 0.02),
        "w_spatial": nrm(ks[13], (DEPTH, A_GROUPS, CHUNK, CHUNK), CHUNK ** -0.5),
        "b_spatial": 1.0 + nrm(ks[14], (DEPTH, A_GROUPS, CHUNK), 0.02),
        "w_proj_a": nrm(ks[15], (DEPTH, A_WIDTH, D_MODEL), A_WIDTH ** -0.5),
        "w_proj_b": nrm(ks[16], (DEPTH, B_OUT, D_MODEL), B_OUT ** -0.5),
        "w_out": nrm(ks[17], (DEPTH, D_MODEL, D_MODEL), D_MODEL ** -0.5),
        "g_post": 1.0 + nrm(ks[18], (DEPTH, D_MODEL), 0.02),
    }


def reference(x_prompt, x_sample, cache_kv_w128, cache_kv_w512, cache_kv_w2048, c_prompt, c_sample,
              w_cond, b_cond, g_pre, w_in, ln_v_g, ln_v_b, w_spatial, b_spatial,
              w_proj_a, w_proj_b, w_out, g_post):
    caches = (cache_kv_w128, cache_kv_w512, cache_kv_w2048)
    y_p, y_s = x_prompt, x_sample
    kv_p = [[] for _ in DILATED_PAIRS]
    kv_s = [[] for _ in DILATED_PAIRS]
    v_rows = []
    for layer in range(DEPTH):
        params = (w_cond[layer], b_cond[layer], g_pre[layer], w_in[layer], ln_v_g[layer], ln_v_b[layer],
                  w_spatial[layer], b_spatial[layer], w_proj_a[layer], w_proj_b[layer], w_out[layer],
                  g_post[layer])
        y_p, rows_p, _ = _layer(y_p, c_prompt, None, *params)
        layer_caches = (caches[0][layer], caches[1][layer], caches[2][layer])
        y_s, rows_s, v_n_s = _layer(y_s, c_sample, layer_caches, *params)
        for gi in range(N_GROUPS_B):
            kv_p[gi].append(rows_p[gi])
            kv_s[gi].append(rows_s[gi])
        v_rows.append(v_n_s)
    return (y_p, y_s,
            jnp.stack(kv_p[0]), jnp.stack(kv_p[1]), jnp.stack(kv_p[2]),
            jnp.stack(kv_s[0]), jnp.stack(kv_s[1]), jnp.stack(kv_s[2]),
            jnp.stack(v_rows))
```

```python
import functools

import jax
import jax.numpy as jnp
from jax import lax
from jax.experimental import pallas as pl
from jax.experimental.pallas import tpu as pltpu

D_MODEL = 1024
SEQ = 2048
HEAD_DIM = 64
N_HEADS = 8
QW = N_HEADS * HEAD_DIM
WINDOWS = (128, 512, 2048)
DILATIONS = (1, 4, 16)
N_BACK = 128
CHUNK = 128
A_GROUPS = 8
EPS = 1e-6
NEG = -1e30
LANES = 128
VMEM_LIMIT = 56 * 1024 * 1024

f32 = jnp.float32
bf16 = jnp.bfloat16


def _silu(x):
    return x * jax.nn.sigmoid(x)


def _norm_mod(x, g_pre, scale, shift):
    y = x * lax.rsqrt(jnp.mean(x * x, axis=-1, keepdims=True) + EPS) * g_pre
    return y * (1.0 + scale) + shift


def _dot(a, b):
    return jnp.dot(a, b, preferred_element_type=f32)


def _dot_nt(a, b):
    return lax.dot_general(a, b, (((1,), (1,)), ((), ())), preferred_element_type=f32)


def _cond_kernel(c_ref, w_ref, b_ref, o_ref):
    o_ref[...] = _dot(_silu(c_ref[...]).astype(bf16), w_ref[...]) + b_ref[...]


def _cond(c_all, w_cond, b_cond):
    n = c_all.shape[0]
    return pl.pallas_call(
        _cond_kernel,
        out_shape=jax.ShapeDtypeStruct((n, 3 * D_MODEL), f32),
        name="cond",
        compiler_params=pltpu.CompilerParams(vmem_limit_bytes=VMEM_LIMIT),
    )(c_all, w_cond, b_cond)


ROWS_PER_STEP = 512
Q_BLOCK = 128
NORM_ROWS = 256


def _softmax_block(q_m, k_cur, v_cur, k_prev, v_prev, mask_cur, mask_prev):
    s_cur = jnp.where(mask_cur, _dot_nt(q_m, k_cur), NEG)
    if k_prev is None:
        m = jnp.max(s_cur, axis=-1, keepdims=True)
        p_cur = jnp.exp(s_cur - m)
        l = jnp.sum(p_cur, axis=-1, keepdims=True)
        acc = _dot(p_cur.astype(bf16), v_cur)
        return acc, m, l
    s_prev = jnp.where(mask_prev, _dot_nt(q_m, k_prev), NEG)
    m = jnp.maximum(jnp.max(s_cur, axis=-1, keepdims=True), jnp.max(s_prev, axis=-1, keepdims=True))
    p_cur = jnp.exp(s_cur - m)
    p_prev = jnp.exp(s_prev - m)
    l = jnp.sum(p_cur, axis=-1, keepdims=True) + jnp.sum(p_prev, axis=-1, keepdims=True)
    acc = _dot(p_cur.astype(bf16), v_cur) + _dot(p_prev.astype(bf16), v_prev)
    return acc, m, l


def _attn_kernel(x_ref, scale_ref, shift_ref, gpre_ref, w_ref, o_ref, lse_ref, kv_ref, h_sc, k_sc, v_sc, *, dil):
    c = pl.program_id(1)
    seg = SEQ // dil
    n_lc = D_MODEL // LANES

    @pl.when(c == 0)
    def _():
        for rb in range(SEQ // NORM_ROWS):
            rs = slice(rb * NORM_ROWS, (rb + 1) * NORM_ROWS)
            hb = _norm_mod(x_ref[0, rs, :], gpre_ref[...], scale_ref[0], shift_ref[0])
            for k in range(n_lc):
                h_sc[k, rs, :] = hb[:, k * LANES:(k + 1) * LANES]

    def rows_of(start, n):
        idx = pl.ds(pl.multiple_of(start, n), n) if dil == 1 else pl.ds(start, n, stride=dil)
        return jnp.concatenate([h_sc[k, idx, :] for k in range(n_lc)], axis=1)

    if seg >= ROWS_PER_STEP:
        per = seg // ROWS_PER_STEP
        h = rows_of((c // per) + (c % per) * ROWS_PER_STEP * dil, ROWS_PER_STEP)
    else:
        per = ROWS_PER_STEP // seg
        h = jnp.concatenate([rows_of(per * c + i, seg) for i in range(per)], axis=0)
    h = h.astype(bf16)
    qkv = _dot(h, w_ref[0])
    kv_ref[0] = qkv[:, QW:]
    base = pl.multiple_of(c * ROWS_PER_STEP, ROWS_PER_STEP)
    k_sc[pl.ds(base, ROWS_PER_STEP), :] = qkv[:, QW:2 * QW].astype(bf16)
    v_sc[pl.ds(base, ROWS_PER_STEP), :] = qkv[:, 2 * QW:].astype(bf16)
    q = qkv[:, :QW] * (HEAD_DIM ** -0.5)

    lane = lax.broadcasted_iota(jnp.int32, (Q_BLOCK, LANES), 1)
    row = lax.broadcasted_iota(jnp.int32, (Q_BLOCK, LANES), 0)
    low = lane < HEAD_DIM
    mask_cur = lane <= row
    mask_prev = lane >= row

    for j in range(ROWS_PER_STEP // Q_BLOCK):
        r0 = base + j * Q_BLOCK
        starts_segment_static = ((j * Q_BLOCK) % seg == 0) if seg <= ROWS_PER_STEP else None
        if starts_segment_static is True:
            use_prev, prev_mask = False, None
        elif starts_segment_static is False or j > 0:
            use_prev, prev_mask = True, mask_prev
        else:
            use_prev = True
            off = jnp.where((base % seg) != 0, 0, LANES)
            prev_mask = lane >= row + off
        r0 = pl.multiple_of(r0, Q_BLOCK)
        rp = pl.multiple_of(jnp.maximum(r0 - Q_BLOCK, 0), Q_BLOCK)
        o_parts = []
        lse_tile = jnp.zeros((Q_BLOCK, LANES), f32)
        for hp in range(N_HEADS // 2):
            cols = slice(hp * LANES, (hp + 1) * LANES)
            q2 = q[j * Q_BLOCK:(j + 1) * Q_BLOCK, cols]
            k_cur = k_sc[pl.ds(r0, Q_BLOCK), cols]
            v_cur = v_sc[pl.ds(r0, Q_BLOCK), cols]
            k_prev = k_sc[pl.ds(rp, Q_BLOCK), cols] if use_prev else None
            v_prev = v_sc[pl.ds(rp, Q_BLOCK), cols] if use_prev else None
            q_even = jnp.where(low, q2, 0.0).astype(bf16)
            q_odd = jnp.where(low, 0.0, q2).astype(bf16)
            acc_e, m_e, l_e = _softmax_block(q_even, k_cur, v_cur, k_prev, v_prev, mask_cur, prev_mask)
            acc_o, m_o, l_o = _softmax_block(q_odd, k_cur, v_cur, k_prev, v_prev, mask_cur, prev_mask)
            o_parts.append(jnp.where(low, acc_e / l_e, acc_o / l_o))
            lse_tile = jnp.where(lane == 2 * hp, m_e + jnp.log(l_e), lse_tile)
            lse_tile = jnp.where(lane == 2 * hp + 1, m_o + jnp.log(l_o), lse_tile)
        rows = slice(j * Q_BLOCK, (j + 1) * Q_BLOCK)
        o_ref[0, rows, :] = jnp.concatenate(o_parts, axis=1).astype(bf16)
        lse_ref[0, rows, :] = lse_tile


def _attn_prompt(x, scale, shift, g_pre, w_qkv, pattern):
    b = x.shape[0]
    steps = SEQ // ROWS_PER_STEP
    dil = DILATIONS[pattern]
    return pl.pallas_call(
        functools.partial(_attn_kernel, dil=dil),
        grid=(b, steps),
        in_specs=[
            pl.BlockSpec((1, SEQ, D_MODEL), lambda i, c: (i, 0, 0)),
            pl.BlockSpec((1, 1, D_MODEL), lambda i, c: (i, 0, 0)),
            pl.BlockSpec((1, 1, D_MODEL), lambda i, c: (i, 0, 0)),
            pl.BlockSpec((1, D_MODEL), lambda i, c: (0, 0)),
            pl.BlockSpec((1, D_MODEL, 3 * QW), lambda i, c: (pattern, 0, 0)),
        ],
        out_specs=[
            pl.BlockSpec((1, ROWS_PER_STEP, QW), lambda i, c: (i, c, 0)),
            pl.BlockSpec((1, ROWS_PER_STEP, LANES), lambda i, c: (i, c, 0)),
            pl.BlockSpec((1, ROWS_PER_STEP, 2 * QW), lambda i, c: (i, c, 0)),
        ],
        out_shape=[
            jax.ShapeDtypeStruct((b, SEQ, QW), bf16),
            jax.ShapeDtypeStruct((b, SEQ, LANES), f32),
            jax.ShapeDtypeStruct((b, SEQ, 2 * QW), f32),
        ],
        scratch_shapes=[pltpu.VMEM((D_MODEL // LANES, SEQ, LANES), f32),
                        pltpu.VMEM((SEQ, QW), bf16), pltpu.VMEM((SEQ, QW), bf16)],
        name=f"attn_d{dil}",
        compiler_params=pltpu.CompilerParams(
            dimension_semantics=("arbitrary", "arbitrary"), vmem_limit_bytes=VMEM_LIMIT),
    )(x, scale, shift, g_pre, w_qkv)


def _qkv_s_kernel(x_ref, scale_ref, shift_ref, gpre_ref, w_ref, o_ref):
    h = _norm_mod(x_ref[...], gpre_ref[...], scale_ref[...], shift_ref[...]).astype(bf16)
    o_ref[...] = _dot(h, w_ref[...])


def _qkv_sample(x, scale, shift, g_pre, w):
    n = x.shape[0]
    cols = w.shape[1]
    tile = 3 * QW
    return pl.pallas_call(
        _qkv_s_kernel,
        grid=(cols // tile,),
        in_specs=[
            pl.BlockSpec((n, D_MODEL), lambda j: (0, 0)),
            pl.BlockSpec((n, D_MODEL), lambda j: (0, 0)),
            pl.BlockSpec((n, D_MODEL), lambda j: (0, 0)),
            pl.BlockSpec((1, D_MODEL), lambda j: (0, 0)),
            pl.BlockSpec((D_MODEL, tile), lambda j: (0, j)),
        ],
        out_specs=pl.BlockSpec((n, tile), lambda j: (0, j)),
        out_shape=jax.ShapeDtypeStruct((n, cols), f32),
        name="qkv_s",
        compiler_params=pltpu.CompilerParams(
            dimension_semantics=("arbitrary",), vmem_limit_bytes=VMEM_LIMIT),
    )(x, scale, shift, g_pre, w)


T_NEW = 8


def _attn_s_kernel(qkv_ref, c0_ref, c1_ref, c2_ref, o_ref, new_sc):
    @pl.when(pl.program_id(0) == 0)
    def _():
        new_sc[...] = jnp.zeros_like(new_sc)

    qkv = qkv_ref[0]
    new_sc[0:T_NEW, :] = qkv[:, 3 * QW:]
    q_all = qkv[:, :3 * QW] * (HEAD_DIM ** -0.5)

    lane8 = lax.broadcasted_iota(jnp.int32, (2 * T_NEW, LANES), 1)
    low = lane8 < HEAD_DIM
    t_idx = lax.broadcasted_iota(jnp.int32, (2 * T_NEW, LANES), 0) % T_NEW
    caches = (c0_ref, c1_ref, c2_ref)
    o_parts = []
    for hp in range(N_HEADS // 2):
        m_run = jnp.full((2 * T_NEW, 1), NEG, f32)
        l_run = jnp.zeros((2 * T_NEW, 1), f32)
        acc_run = jnp.zeros((2 * T_NEW, LANES), f32)
        for g, dil in enumerate(DILATIONS):
            rows = WINDOWS[g]
            cref = caches[g]
            q2 = q_all[:, g * QW + hp * LANES: g * QW + (hp + 1) * LANES]
            q_st = jnp.concatenate([jnp.where(low[:T_NEW], q2, 0.0), jnp.where(low[:T_NEW], 0.0, q2)],
                                   axis=0).astype(bf16)
            kt = cref[0, hp * LANES:(hp + 1) * LANES, :].astype(bf16)
            vt = cref[0, QW + hp * LANES: QW + (hp + 1) * LANES, :].astype(bf16)
            k_new = new_sc[:, g * QW + hp * LANES: g * QW + (hp + 1) * LANES].astype(bf16)
            v_new = new_sc[:, 3 * QW + g * QW + hp * LANES: 3 * QW + g * QW + (hp + 1) * LANES].astype(bf16)

            s_c = _dot(q_st, kt)
            rho = lax.broadcasted_iota(jnp.int32, (2 * T_NEW, rows), 1)
            tq = lax.broadcasted_iota(jnp.int32, (2 * T_NEW, rows), 0) % T_NEW
            ok_c = jnp.logical_and(rho >= tq, ((rho - tq) & (dil - 1)) == 0)
            s_c = jnp.where(ok_c, s_c, NEG)
            s_n = _dot_nt(q_st, k_new)
            ok_n = jnp.logical_and(lane8 <= t_idx, ((t_idx - lane8) & (dil - 1)) == 0)
            s_n = jnp.where(ok_n, s_n, NEG)

            m_g = jnp.maximum(jnp.max(s_c, axis=-1, keepdims=True), jnp.max(s_n, axis=-1, keepdims=True))
            m_new = jnp.maximum(m_run, m_g)
            p_c = jnp.exp(s_c - m_new)
            p_n = jnp.exp(s_n - m_new)
            alpha = jnp.exp(m_run - m_new)
            l_run = alpha * l_run + jnp.sum(p_c, axis=-1, keepdims=True) + jnp.sum(p_n, axis=-1, keepdims=True)
            acc_run = alpha * acc_run + _dot_nt(p_c.astype(bf16), vt) + _dot(p_n.astype(bf16), v_new)
            m_run = m_new
        out = acc_run / l_run
        o_parts.append(jnp.where(low[:T_NEW], out[:T_NEW], out[T_NEW:]))
    o_ref[0] = jnp.concatenate(o_parts, axis=1)


def _attn_sample(qkv, c0, c1, c2):
    n = qkv.shape[0]
    return pl.pallas_call(
        _attn_s_kernel,
        grid=(n,),
        in_specs=[
            pl.BlockSpec((1, T_NEW, 9 * QW), lambda i: (i, 0, 0)),
            pl.BlockSpec((1, 2 * QW, WINDOWS[0]), lambda i: (i, 0, 0)),
            pl.BlockSpec((1, 2 * QW, WINDOWS[1]), lambda i: (i, 0, 0)),
            pl.BlockSpec((1, 2 * QW, WINDOWS[2]), lambda i: (i, 0, 0)),
        ],
        out_specs=pl.BlockSpec((1, T_NEW, QW), lambda i: (i, 0, 0)),
        out_shape=jax.ShapeDtypeStruct((n, T_NEW, QW), f32),
        scratch_shapes=[pltpu.VMEM((LANES, 6 * QW), f32)],
        name="attn_s",
        compiler_params=pltpu.CompilerParams(
            dimension_semantics=("arbitrary",), vmem_limit_bytes=VMEM_LIMIT),
    )(qkv, c0, c1, c2)


TILE = 256


def _expand_heads(w):
    r = lax.broadcasted_iota(jnp.int32, (LANES, QW), 0)
    l = lax.broadcasted_iota(jnp.int32, (LANES, QW), 1)
    expand = (l // HEAD_DIM == r).astype(bf16)
    hi = w.astype(bf16)
    lo = (w - hi.astype(f32)).astype(bf16)
    return _dot(hi, expand) + _dot(lo, expand)


def _rest_kernel(*refs, sample):
    if sample:
        (x_ref, scale_ref, shift_ref, gate_ref, gpre_ref, gpost_ref, lng_ref, lnb_ref,
         wa_ref, wzb_ref, wg_ref, wsp_ref, bsp_ref, wpa_ref, wpb_ref, wout_ref,
         attn_ref, y_ref, vn_ref) = refs
        x = x_ref[...]
        scale, shift, gate = scale_ref[...], shift_ref[...], gate_ref[...]
    else:
        (x_ref, scale_ref, shift_ref, gate_ref, gpre_ref, gpost_ref, lng_ref, lnb_ref,
         wa_ref, wzb_ref, wg_ref, wsp_ref, bsp_ref, wpa_ref, wpb_ref, wout_ref,
         o1_ref, o4_ref, o16_ref, l1_ref, l4_ref, l16_ref, y_ref,
         o4_sc, o16_sc, l4_sc, l16_sc) = refs
        x = x_ref[0]
        scale, shift, gate = scale_ref[0], shift_ref[0], gate_ref[0]

    rows = x.shape[0]
    h = _norm_mod(x, gpre_ref[...], scale, shift).astype(bf16)

    pa = _dot(h, wa_ref[...])
    u_a, v_a, z_a = pa[:, :D_MODEL], pa[:, D_MODEL:2 * D_MODEL], pa[:, 2 * D_MODEL:]
    mu = jnp.mean(v_a, axis=-1, keepdims=True)
    cen = v_a - mu
    var = jnp.mean(cen * cen, axis=-1, keepdims=True)
    v_n = cen * lax.rsqrt(var + EPS) * lng_ref[...] + lnb_ref[...]
    v_nb = v_n.astype(bf16)
    if sample:
        vn_ref[...] = v_n
        zs = jnp.concatenate(
            [_dot(wsp_ref[g], v_nb[:, g * LANES:(g + 1) * LANES]) + bsp_ref[:, g:g + 1]
             for g in range(A_GROUPS)], axis=1)
    else:
        chunks = []
        for ck in range(rows // CHUNK):
            rs = slice(ck * CHUNK, (ck + 1) * CHUNK)
            chunks.append(jnp.concatenate(
                [_dot(wsp_ref[g], v_nb[rs, g * LANES:(g + 1) * LANES]) + bsp_ref[:, g:g + 1]
                 for g in range(A_GROUPS)], axis=1))
        zs = jnp.concatenate(chunks, axis=0)
    y_a = u_a * zs * _silu(z_a)

    if sample:
        attn = attn_ref[...]
    else:
        for gsc, lsc, oref, lref, dil in ((o4_sc, l4_sc, o4_ref, l4_ref, DILATIONS[1]),
                                          (o16_sc, l16_sc, o16_ref, l16_ref, DILATIONS[2])):
            n = rows // dil
            for r in range(dil):
                o_r = oref[0, r].astype(f32)
                for k in range(QW // LANES):
                    gsc[k, pl.ds(r, n, stride=dil), :] = o_r[:, k * LANES:(k + 1) * LANES]
                lsc[pl.ds(r, n, stride=dil), :] = lref[0, r]
        lses = (l1_ref[0], l4_sc[...], l16_sc[...])
        unchunk = lambda sc: jnp.concatenate([sc[k] for k in range(QW // LANES)], axis=1)
        outs = (o1_ref[0].astype(f32), unchunk(o4_sc), unchunk(o16_sc))
        m_all = jnp.maximum(jnp.maximum(lses[0], lses[1]), lses[2])
        ws = [jnp.exp(l - m_all) for l in lses]
        den = ws[0] + ws[1] + ws[2]
        attn = sum(_expand_heads(w / den) * o for w, o in zip(ws, outs))

    z_b = _dot(h, wzb_ref[...])
    y_b = (attn * _silu(z_b)).astype(bf16)

    gl = _dot(h, wg_ref[...])
    p_a = _dot(y_a.astype(bf16), wpa_ref[...])
    p_b = _dot(y_b, wpb_ref[...])
    merged = jax.nn.sigmoid(gl[:, :D_MODEL]) * p_a + jax.nn.sigmoid(gl[:, D_MODEL:]) * p_b
    out = _dot(merged.astype(bf16), wout_ref[...])
    normed = out * lax.rsqrt(jnp.mean(out * out, axis=-1, keepdims=True) + EPS) * gpost_ref[...]
    y = x + gate * normed
    if sample:
        y_ref[...] = y
    else:
        y_ref[0] = y


def _full(shape):
    nd = len(shape)
    return pl.BlockSpec(shape, lambda *_: (0,) * nd)


def _weight_specs(weights):
    return [_full(w.shape) for w in weights]


def _rest_prompt(x, scale, shift, gate, vecs, weights, o1, o4, o16, l1, l4, l16):
    b = x.shape[0]
    tiles = SEQ // TILE
    d4, d16 = DILATIONS[1], DILATIONS[2]
    o4 = o4.reshape(b, d4, SEQ // d4, QW)
    o16 = o16.reshape(b, d16, SEQ // d16, QW)
    l4 = l4.reshape(b, d4, SEQ // d4, LANES)
    l16 = l16.reshape(b, d16, SEQ // d16, LANES)
    per_b = lambda i, c: (i, 0, 0)
    tile3 = lambda i, c: (i, c, 0)
    tile4 = lambda i, c: (i, 0, c, 0)
    in_specs = (
        [pl.BlockSpec((1, TILE, D_MODEL), tile3),
         pl.BlockSpec((1, 1, D_MODEL), per_b), pl.BlockSpec((1, 1, D_MODEL), per_b),
         pl.BlockSpec((1, 1, D_MODEL), per_b)]
        + [_full(v.shape) for v in vecs] + _weight_specs(weights)
        + [pl.BlockSpec((1, TILE, QW), tile3),
           pl.BlockSpec((1, d4, TILE // d4, QW), tile4),
           pl.BlockSpec((1, d16, TILE // d16, QW), tile4),
           pl.BlockSpec((1, TILE, LANES), tile3),
           pl.BlockSpec((1, d4, TILE // d4, LANES), tile4),
           pl.BlockSpec((1, d16, TILE // d16, LANES), tile4)])
    return pl.pallas_call(
        functools.partial(_rest_kernel, sample=False),
        grid=(b, tiles),
        in_specs=in_specs,
        out_specs=pl.BlockSpec((1, TILE, D_MODEL), tile3),
        out_shape=jax.ShapeDtypeStruct((b, SEQ, D_MODEL), f32),
        scratch_shapes=[pltpu.VMEM((QW // LANES, TILE, LANES), f32), pltpu.VMEM((QW // LANES, TILE, LANES), f32),
                        pltpu.VMEM((TILE, LANES), f32), pltpu.VMEM((TILE, LANES), f32)],
        name="rest_p",
        compiler_params=pltpu.CompilerParams(
            dimension_semantics=("arbitrary", "arbitrary"), vmem_limit_bytes=VMEM_LIMIT),
    )(x, scale, shift, gate, *vecs, *weights, o1, o4, o16, l1, l4, l16)


def _rest_sample(x, scale, shift, gate, vecs, weights, attn):
    n = x.shape[0]
    args = (x, scale, shift, gate, *vecs, *weights, attn)
    return pl.pallas_call(
        functools.partial(_rest_kernel, sample=True),
        in_specs=[_full(a.shape) for a in args],
        out_specs=[_full((n, D_MODEL)), _full((n, D_MODEL))],
        out_shape=[jax.ShapeDtypeStruct((n, D_MODEL), f32), jax.ShapeDtypeStruct((n, D_MODEL), f32)],
        grid=(1,),
        name="rest_s",
        compiler_params=pltpu.CompilerParams(
            dimension_semantics=("arbitrary",), vmem_limit_bytes=VMEM_LIMIT),
    )(*args)


def _kv_rows(kv, batch, rows):
    return kv.reshape(1, batch, rows, 2, N_HEADS, HEAD_DIM)


def kernel(x_prompt, x_sample, cache_kv_w128, cache_kv_w512, cache_kv_w2048, c_prompt, c_sample, w_cond, b_cond, g_pre, w_in, ln_v_g, ln_v_b, w_spatial, b_spatial, w_proj_a, w_proj_b, w_out, g_post):
    assert w_in.shape[0] == 1, "single layer"
    bp, seq, _ = x_prompt.shape
    bs, t_new, _ = x_sample.shape
    assert seq == SEQ and t_new == T_NEW

    w_in_b = w_in[0].astype(bf16)
    w_a = w_in_b[:, :3 * D_MODEL]
    q0, k0, v0 = 3 * D_MODEL, 3 * D_MODEL + 3 * QW, 3 * D_MODEL + 6 * QW
    w_qkv_all = w_in_b[:, q0:q0 + 9 * QW]
    w_qkv_pat = jnp.stack([
        jnp.concatenate([w_in_b[:, q0 + g * QW:q0 + (g + 1) * QW],
                         w_in_b[:, k0 + g * QW:k0 + (g + 1) * QW],
                         w_in_b[:, v0 + g * QW:v0 + (g + 1) * QW]], axis=1) for g in range(3)])
    zb0 = q0 + 9 * QW
    w_zb = w_in_b[:, zb0:zb0 + QW]
    w_gate = w_in_b[:, zb0 + QW:]
    causal = jnp.tril(jnp.ones((CHUNK, CHUNK), bool))
    w_sp = jnp.where(causal[None], w_spatial[0], 0.0)
    w_sp_p = w_sp.astype(bf16)
    w_sp_s = jnp.stack([jnp.kron(jnp.eye(bs, dtype=f32), w_sp[g, :T_NEW, :T_NEW]) for g in range(A_GROUPS)]).astype(bf16)
    b_sp_p = b_spatial[0].T
    b_sp_s = jnp.tile(b_spatial[0][:, :T_NEW].T, (bs, 1))
    weights_tail = (w_proj_a[0].astype(bf16), w_proj_b[0].astype(bf16), w_out[0].astype(bf16))
    vecs = (g_pre, g_post, ln_v_g, ln_v_b)

    mod = _cond(jnp.concatenate([c_prompt, c_sample], axis=0), w_cond[0].astype(bf16), b_cond)
    shift, scale, gate = mod[:, :D_MODEL], mod[:, D_MODEL:2 * D_MODEL], mod[:, 2 * D_MODEL:]
    mp = lambda a: a[:bp].reshape(bp, 1, D_MODEL)
    ms = lambda a: jnp.repeat(a[bp:], T_NEW, axis=0)

    res = [_attn_prompt(x_prompt, mp(scale), mp(shift), g_pre, w_qkv_pat, g) for g in range(3)]
    (o1, l1, kv1), (o4, l4, kv4), (o16, l16, kv16) = res
    y_p = _rest_prompt(x_prompt, mp(scale), mp(shift), mp(gate), vecs,
                       (w_a, w_zb, w_gate, w_sp_p, b_sp_p) + weights_tail, o1, o4, o16, l1, l4, l16)
    kv_p128 = _kv_rows(kv1[:, SEQ - WINDOWS[0]:], bp, WINDOWS[0])
    d4, d16 = DILATIONS[1], DILATIONS[2]
    kv4 = kv4.reshape(bp, d4, SEQ // d4, 2 * QW)[:, :, (SEQ - WINDOWS[1]) // d4:]
    kv_p512 = _kv_rows(jnp.swapaxes(kv4, 1, 2), bp, WINDOWS[1])
    kv16 = kv16.reshape(bp, d16, SEQ // d16, 2 * QW)
    kv_p2048 = _kv_rows(jnp.swapaxes(kv16, 1, 2), bp, WINDOWS[2])

    n_s = bs * T_NEW
    xs = x_sample.reshape(n_s, D_MODEL)
    qkv_s = _qkv_sample(xs, ms(scale), ms(shift), g_pre, w_qkv_all)
    caches_t = []
    for cache, win in zip((cache_kv_w128, cache_kv_w512, cache_kv_w2048), WINDOWS):
        caches_t.append(jnp.transpose(cache[0], (0, 2, 3, 4, 1)).reshape(bs, 2 * QW, win))
    attn_s = _attn_sample(qkv_s.reshape(bs, T_NEW, 9 * QW), *caches_t).reshape(n_s, QW)
    y_s, v_n_s = _rest_sample(xs, ms(scale), ms(shift), ms(gate), vecs,
                              (w_a, w_zb, w_gate, w_sp_s, b_sp_s) + weights_tail, attn_s)
    kv_s = []
    for g in range(3):
        kq = qkv_s[:, 3 * QW + g * QW:3 * QW + (g + 1) * QW]
        vq = qkv_s[:, 6 * QW + g * QW:6 * QW + (g + 1) * QW]
        kv_s.append(_kv_rows(jnp.concatenate([kq, vq], axis=1), bs, T_NEW))

    return (y_p, y_s.reshape(bs, T_NEW, D_MODEL), kv_p128, kv_p512, kv_p2048,
            kv_s[0], kv_s[1], kv_s[2], v_n_s.reshape(1, bs, T_NEW, D_MODEL))
```

```python
import functools

import jax
import jax.numpy as jnp
from jax import lax
from jax.experimental import pallas as pl
from jax.experimental.pallas import tpu as pltpu

D_MODEL = 1024
SEQ = 2048
HEAD_DIM = 64
N_HEADS = 8
QW = N_HEADS * HEAD_DIM
WINDOWS = (128, 512, 2048)
DILATIONS = (1, 4, 16)
N_BACK = 128
CHUNK = 128
A_GROUPS = 8
EPS = 1e-6
NEG = -1e30
LANES = 128
VMEM_LIMIT = 56 * 1024 * 1024

f32 = jnp.float32
bf16 = jnp.bfloat16


def _silu(x):
    return x * jax.nn.sigmoid(x)


def _norm_mod(x, g_pre, scale, shift):
    y = x * lax.rsqrt(jnp.mean(x * x, axis=-1, keepdims=True) + EPS) * g_pre
    return y * (1.0 + scale) + shift


def _dot(a, b):
    return jnp.dot(a, b, preferred_element_type=f32)


def _dot_nt(a, b):
    return lax.dot_general(a, b, (((1,), (1,)), ((), ())), preferred_element_type=f32)


def _cond_kernel(c_ref, w_ref, b_ref, o_ref):
    o_ref[...] = _dot(_silu(c_ref[...]).astype(bf16), w_ref[...]) + b_ref[...]


def _cond(c_all, w_cond, b_cond):
    n = c_all.shape[0]
    return pl.pallas_call(
        _cond_kernel,
        out_shape=jax.ShapeDtypeStruct((n, 3 * D_MODEL), f32),
        name="cond",
        compiler_params=pltpu.CompilerParams(vmem_limit_bytes=VMEM_LIMIT),
    )(c_all, w_cond, b_cond)


ROWS_PER_STEP = 512
Q_BLOCK = 128
NORM_ROWS = 256


def _attn_kernel(x_ref, scale_ref, shift_ref, gpre_ref, w_ref, o_ref, m_ref, l_ref, kv_ref,
                 h_sc, k_sc, ve_sc, vo_sc, *, dil):
    c = pl.program_id(1)
    seg = SEQ // dil
    n_lc = D_MODEL // LANES

    @pl.when(c == 0)
    def _():
        zeros = jnp.zeros((Q_BLOCK, QW), bf16)
        k_sc[0:Q_BLOCK, :] = zeros
        ve_sc[0:Q_BLOCK, :] = zeros
        vo_sc[0:Q_BLOCK, :] = zeros
        for rb in range(SEQ // NORM_ROWS):
            rs = slice(rb * NORM_ROWS, (rb + 1) * NORM_ROWS)
            hb = _norm_mod(x_ref[0, rs, :], gpre_ref[...], scale_ref[0], shift_ref[0])
            for k in range(n_lc):
                h_sc[k, rs, :] = hb[:, k * LANES:(k + 1) * LANES]

    def rows_of(start, n):
        idx = pl.ds(pl.multiple_of(start, n), n) if dil == 1 else pl.ds(start, n, stride=dil)
        return jnp.concatenate([h_sc[k, idx, :] for k in range(n_lc)], axis=1)

    if seg >= ROWS_PER_STEP:
        per = seg // ROWS_PER_STEP
        h = rows_of((c // per) + (c % per) * ROWS_PER_STEP * dil, ROWS_PER_STEP)
    else:
        per = ROWS_PER_STEP // seg
        h = jnp.concatenate([rows_of(per * c + i, seg) for i in range(per)], axis=0)
    h = h.astype(bf16)
    qkv = _dot(h, w_ref[0])
    kv_ref[0] = qkv[:, QW:]
    base = pl.multiple_of(c * ROWS_PER_STEP, ROWS_PER_STEP)
    low_w = (lax.broadcasted_iota(jnp.int32, (ROWS_PER_STEP, QW), 1) % LANES) < HEAD_DIM
    v = qkv[:, 2 * QW:]
    new_rows = pl.ds(Q_BLOCK + base, ROWS_PER_STEP)
    k_sc[new_rows, :] = qkv[:, QW:2 * QW].astype(bf16)
    ve_sc[new_rows, :] = jnp.where(low_w, v, 1.0).astype(bf16)
    vo_sc[new_rows, :] = jnp.where(low_w, 1.0, v).astype(bf16)
    q = qkv[:, :QW] * (HEAD_DIM ** -0.5)
    q_even = jnp.where(low_w, q, 0.0).astype(bf16)
    q_odd = jnp.where(low_w, 0.0, q).astype(bf16)

    n_blocks = ROWS_PER_STEP // Q_BLOCK
    pairs = N_HEADS // 2
    qi = lax.broadcasted_iota(jnp.int32, (2 * Q_BLOCK, 2 * Q_BLOCK), 0) % Q_BLOCK
    kj = lax.broadcasted_iota(jnp.int32, (2 * Q_BLOCK, 2 * Q_BLOCK), 1)
    lane = lax.broadcasted_iota(jnp.int32, (Q_BLOCK, LANES), 1)
    low = lane < HEAD_DIM

    def has_prev(j):
        return seg > ROWS_PER_STEP or (j * Q_BLOCK) % seg != 0

    def key_rows(j):
        r0 = pl.multiple_of(base + j * Q_BLOCK, Q_BLOCK)
        return pl.ds(r0, 2 * Q_BLOCK) if has_prev(j) else pl.ds(r0 + Q_BLOCK, Q_BLOCK)

    def mask_of(j):
        if not has_prev(j):
            own = (2 * Q_BLOCK, Q_BLOCK)
            return (lax.broadcasted_iota(jnp.int32, own, 1)
                    <= lax.broadcasted_iota(jnp.int32, own, 0) % Q_BLOCK)
        off = jnp.where((base % seg) != 0, 0, Q_BLOCK) if (seg > ROWS_PER_STEP and j == 0) else 0
        in_prev = jnp.logical_and(kj < Q_BLOCK, kj >= qi + off)
        in_cur = jnp.logical_and(kj >= Q_BLOCK, kj - Q_BLOCK <= qi)
        return jnp.logical_or(in_prev, in_cur)

    def qk(j):
        rows = slice(j * Q_BLOCK, (j + 1) * Q_BLOCK)
        keys = key_rows(j)
        out = []
        for hp in range(pairs):
            cols = slice(hp * LANES, (hp + 1) * LANES)
            q_st = jnp.concatenate([q_even[rows, cols], q_odd[rows, cols]], axis=0)
            out.append(_dot_nt(q_st, k_sc[keys, cols]))
        return out

    def finish(j, scores):
        rows = slice(j * Q_BLOCK, (j + 1) * Q_BLOCK)
        keys = key_rows(j)
        mask = mask_of(j)
        ps, ms = [], []
        for hp in range(pairs):
            s = jnp.where(mask, scores[hp], NEG)
            for half in (s[:Q_BLOCK], s[Q_BLOCK:]):
                m = jnp.max(half, axis=-1, keepdims=True)
                ps.append(jnp.exp(half - m).astype(bf16))
                ms.append(m)
        o_parts = []
        m_tile = jnp.zeros((Q_BLOCK, LANES), f32)
        l_tile = jnp.ones((Q_BLOCK, LANES), f32)
        for hp in range(pairs):
            cols = slice(hp * LANES, (hp + 1) * LANES)
            acc_e = _dot(ps[2 * hp], ve_sc[keys, cols])
            acc_o = _dot(ps[2 * hp + 1], vo_sc[keys, cols])
            o_parts.append(jnp.where(low, acc_e, acc_o))
            l_tile = jnp.where(lane == HEAD_DIM + hp, acc_e, jnp.where(lane == hp, acc_o, l_tile))
            m_tile = jnp.where(lane == HEAD_DIM + hp, ms[2 * hp],
                               jnp.where(lane == hp, ms[2 * hp + 1], m_tile))
        o_ref[0, rows, :] = jnp.concatenate(o_parts, axis=1).astype(bf16)
        m_ref[0, rows, :] = m_tile
        l_ref[0, rows, :] = l_tile

    scores = qk(0)
    for j in range(n_blocks):
        nxt = qk(j + 1) if j + 1 < n_blocks else None
        finish(j, scores)
        scores = nxt


def _attn_prompt(x, scale, shift, g_pre, w_qkv, pattern):
    b = x.shape[0]
    steps = SEQ // ROWS_PER_STEP
    dil = DILATIONS[pattern]
    return pl.pallas_call(
        functools.partial(_attn_kernel, dil=dil),
        grid=(b, steps),
        in_specs=[
            pl.BlockSpec((1, SEQ, D_MODEL), lambda i, c: (i, 0, 0)),
            pl.BlockSpec((1, 1, D_MODEL), lambda i, c: (i, 0, 0)),
            pl.BlockSpec((1, 1, D_MODEL), lambda i, c: (i, 0, 0)),
            pl.BlockSpec((1, D_MODEL), lambda i, c: (0, 0)),
            pl.BlockSpec((1, D_MODEL, 3 * QW), lambda i, c: (pattern, 0, 0)),
        ],
        out_specs=[
            pl.BlockSpec((1, ROWS_PER_STEP, QW), lambda i, c: (i, c, 0)),
            pl.BlockSpec((1, ROWS_PER_STEP, LANES), lambda i, c: (i, c, 0)),
            pl.BlockSpec((1, ROWS_PER_STEP, LANES), lambda i, c: (i, c, 0)),
            pl.BlockSpec((1, ROWS_PER_STEP, 2 * QW), lambda i, c: (i, c, 0)),
        ],
        out_shape=[
            jax.ShapeDtypeStruct((b, SEQ, QW), bf16),
            jax.ShapeDtypeStruct((b, SEQ, LANES), f32),
            jax.ShapeDtypeStruct((b, SEQ, LANES), f32),
            jax.ShapeDtypeStruct((b, SEQ, 2 * QW), f32),
        ],
        scratch_shapes=[pltpu.VMEM((D_MODEL // LANES, SEQ, LANES), f32),
                        pltpu.VMEM((SEQ + Q_BLOCK, QW), bf16), pltpu.VMEM((SEQ + Q_BLOCK, QW), bf16),
                        pltpu.VMEM((SEQ + Q_BLOCK, QW), bf16)],
        name=f"attn_d{dil}",
        compiler_params=pltpu.CompilerParams(
            dimension_semantics=("arbitrary", "arbitrary"), vmem_limit_bytes=VMEM_LIMIT),
    )(x, scale, shift, g_pre, w_qkv)


def _qkv_s_kernel(x_ref, scale_ref, shift_ref, gpre_ref, w_ref, o_ref):
    h = _norm_mod(x_ref[...], gpre_ref[...], scale_ref[...], shift_ref[...]).astype(bf16)
    o_ref[...] = _dot(h, w_ref[...])


def _qkv_sample(x, scale, shift, g_pre, w):
    n = x.shape[0]
    cols = w.shape[1]
    tile = 3 * QW
    return pl.pallas_call(
        _qkv_s_kernel,
        grid=(cols // tile,),
        in_specs=[
            pl.BlockSpec((n, D_MODEL), lambda j: (0, 0)),
            pl.BlockSpec((n, D_MODEL), lambda j: (0, 0)),
            pl.BlockSpec((n, D_MODEL), lambda j: (0, 0)),
            pl.BlockSpec((1, D_MODEL), lambda j: (0, 0)),
            pl.BlockSpec((D_MODEL, tile), lambda j: (0, j)),
        ],
        out_specs=pl.BlockSpec((n, tile), lambda j: (0, j)),
        out_shape=jax.ShapeDtypeStruct((n, cols), f32),
        name="qkv_s",
        compiler_params=pltpu.CompilerParams(
            dimension_semantics=("arbitrary",), vmem_limit_bytes=VMEM_LIMIT),
    )(x, scale, shift, g_pre, w)


T_NEW = 8


def _attn_s_kernel(qkv_ref, c0_ref, c1_ref, c2_ref, o_ref, new_sc):
    @pl.when(pl.program_id(0) == 0)
    def _():
        new_sc[...] = jnp.zeros_like(new_sc)

    qkv = qkv_ref[0]
    new_sc[0:T_NEW, :] = qkv[:, 3 * QW:]
    q_all = qkv[:, :3 * QW] * (HEAD_DIM ** -0.5)

    lane8 = lax.broadcasted_iota(jnp.int32, (2 * T_NEW, LANES), 1)
    low = lane8 < HEAD_DIM
    t_idx = lax.broadcasted_iota(jnp.int32, (2 * T_NEW, LANES), 0) % T_NEW
    caches = (c0_ref, c1_ref, c2_ref)
    o_parts = []
    for hp in range(N_HEADS // 2):
        m_run = jnp.full((2 * T_NEW, 1), NEG, f32)
        l_run = jnp.zeros((2 * T_NEW, 1), f32)
        acc_run = jnp.zeros((2 * T_NEW, LANES), f32)
        for g, dil in enumerate(DILATIONS):
            rows = WINDOWS[g]
            cref = caches[g]
            q2 = q_all[:, g * QW + hp * LANES: g * QW + (hp + 1) * LANES]
            q_st = jnp.concatenate([jnp.where(low[:T_NEW], q2, 0.0), jnp.where(low[:T_NEW], 0.0, q2)],
                                   axis=0).astype(bf16)
            kt = cref[0, hp * LANES:(hp + 1) * LANES, :].astype(bf16)
            vt = cref[0, QW + hp * LANES: QW + (hp + 1) * LANES, :].astype(bf16)
            k_new = new_sc[:, g * QW + hp * LANES: g * QW + (hp + 1) * LANES].astype(bf16)
            v_new = new_sc[:, 3 * QW + g * QW + hp * LANES: 3 * QW + g * QW + (hp + 1) * LANES].astype(bf16)

            s_c = _dot(q_st, kt)
            rho = lax.broadcasted_iota(jnp.int32, (2 * T_NEW, rows), 1)
            tq = lax.broadcasted_iota(jnp.int32, (2 * T_NEW, rows), 0) % T_NEW
            ok_c = jnp.logical_and(rho >= tq, ((rho - tq) & (dil - 1)) == 0)
            s_c = jnp.where(ok_c, s_c, NEG)
            s_n = _dot_nt(q_st, k_new)
            ok_n = jnp.logical_and(lane8 <= t_idx, ((t_idx - lane8) & (dil - 1)) == 0)
            s_n = jnp.where(ok_n, s_n, NEG)

            m_g = jnp.maximum(jnp.max(s_c, axis=-1, keepdims=True), jnp.max(s_n, axis=-1, keepdims=True))
            m_new = jnp.maximum(m_run, m_g)
            p_c = jnp.exp(s_c - m_new)
            p_n = jnp.exp(s_n - m_new)
            alpha = jnp.exp(m_run - m_new)
            l_run = alpha * l_run + jnp.sum(p_c, axis=-1, keepdims=True) + jnp.sum(p_n, axis=-1, keepdims=True)
            acc_run = alpha * acc_run + _dot_nt(p_c.astype(bf16), vt) + _dot(p_n.astype(bf16), v_new)
            m_run = m_new
        out = acc_run / l_run
        o_parts.append(jnp.where(low[:T_NEW], out[:T_NEW], out[T_NEW:]))
    o_ref[0] = jnp.concatenate(o_parts, axis=1)


def _attn_sample(qkv, c0, c1, c2):
    n = qkv.shape[0]
    return pl.pallas_call(
        _attn_s_kernel,
        grid=(n,),
        in_specs=[
            pl.BlockSpec((1, T_NEW, 9 * QW), lambda i: (i, 0, 0)),
            pl.BlockSpec((1, 2 * QW, WINDOWS[0]), lambda i: (i, 0, 0)),
            pl.BlockSpec((1, 2 * QW, WINDOWS[1]), lambda i: (i, 0, 0)),
            pl.BlockSpec((1, 2 * QW, WINDOWS[2]), lambda i: (i, 0, 0)),
        ],
        out_specs=pl.BlockSpec((1, T_NEW, QW), lambda i: (i, 0, 0)),
        out_shape=jax.ShapeDtypeStruct((n, T_NEW, QW), f32),
        scratch_shapes=[pltpu.VMEM((LANES, 6 * QW), f32)],
        name="attn_s",
        compiler_params=pltpu.CompilerParams(
            dimension_semantics=("arbitrary",), vmem_limit_bytes=VMEM_LIMIT),
    )(qkv, c0, c1, c2)


TILE = 256


def _expand_heads(w):
    r = lax.broadcasted_iota(jnp.int32, (LANES, QW), 0)
    head = lax.broadcasted_iota(jnp.int32, (LANES, QW), 1) // HEAD_DIM
    expand = (r == head // 2 + HEAD_DIM * (1 - head % 2)).astype(bf16)
    hi = w.astype(bf16)
    lo = (w - hi.astype(f32)).astype(bf16)
    return _dot(hi, expand) + _dot(lo, expand)


def _rest_kernel(*refs, sample):
    if sample:
        (x_ref, scale_ref, shift_ref, gate_ref, gpre_ref, gpost_ref, lng_ref, lnb_ref,
         wa_ref, wzb_ref, wg_ref, wsp_ref, bsp_ref, wpa_ref, wpb_ref, wout_ref,
         attn_ref, y_ref, vn_ref) = refs
        x = x_ref[...]
        scale, shift, gate = scale_ref[...], shift_ref[...], gate_ref[...]
    else:
        (x_ref, scale_ref, shift_ref, gate_ref, gpre_ref, gpost_ref, lng_ref, lnb_ref,
         wa_ref, wzb_ref, wg_ref, wsp_ref, bsp_ref, wpa_ref, wpb_ref, wout_ref,
         o1_ref, m1_ref, l1_ref, o4_ref, m4_ref, l4_ref, o16_ref, m16_ref, l16_ref, y_ref,
         o4_sc, m4_sc, l4_sc, o16_sc, m16_sc, l16_sc) = refs
        x = x_ref[0]
        scale, shift, gate = scale_ref[0], shift_ref[0], gate_ref[0]

    rows = x.shape[0]
    h = _norm_mod(x, gpre_ref[...], scale, shift).astype(bf16)

    pa = _dot(h, wa_ref[...])
    u_a, v_a, z_a = pa[:, :D_MODEL], pa[:, D_MODEL:2 * D_MODEL], pa[:, 2 * D_MODEL:]
    mu = jnp.mean(v_a, axis=-1, keepdims=True)
    cen = v_a - mu
    var = jnp.mean(cen * cen, axis=-1, keepdims=True)
    v_n = cen * lax.rsqrt(var + EPS) * lng_ref[...] + lnb_ref[...]
    v_nb = v_n.astype(bf16)
    if sample:
        vn_ref[...] = v_n
        zs = jnp.concatenate(
            [_dot(wsp_ref[g], v_nb[:, g * LANES:(g + 1) * LANES]) + bsp_ref[:, g:g + 1]
             for g in range(A_GROUPS)], axis=1)
    else:
        chunks = []
        for ck in range(rows // CHUNK):
            rs = slice(ck * CHUNK, (ck + 1) * CHUNK)
            chunks.append(jnp.concatenate(
                [_dot(wsp_ref[g], v_nb[rs, g * LANES:(g + 1) * LANES]) + bsp_ref[:, g:g + 1]
                 for g in range(A_GROUPS)], axis=1))
        zs = jnp.concatenate(chunks, axis=0)
    y_a = u_a * zs * _silu(z_a)

    if sample:
        attn = attn_ref[...]
    else:
        for osc, msc, lsc, oref, mref, lref, dil in (
                (o4_sc, m4_sc, l4_sc, o4_ref, m4_ref, l4_ref, DILATIONS[1]),
                (o16_sc, m16_sc, l16_sc, o16_ref, m16_ref, l16_ref, DILATIONS[2])):
            n = rows // dil
            for r in range(dil):
                o_r = oref[0, r].astype(f32)
                for k in range(QW // LANES):
                    osc[k, pl.ds(r, n, stride=dil), :] = o_r[:, k * LANES:(k + 1) * LANES]
                msc[pl.ds(r, n, stride=dil), :] = mref[0, r]
                lsc[pl.ds(r, n, stride=dil), :] = lref[0, r]
        ms = (m1_ref[0], m4_sc[...], m16_sc[...])
        ls = (l1_ref[0], l4_sc[...], l16_sc[...])
        unchunk = lambda sc: jnp.concatenate([sc[k] for k in range(QW // LANES)], axis=1)
        outs = (o1_ref[0].astype(f32), unchunk(o4_sc), unchunk(o16_sc))
        m_all = jnp.maximum(jnp.maximum(ms[0], ms[1]), ms[2])
        ws = [jnp.exp(m - m_all) for m in ms]
        den = ws[0] * ls[0] + ws[1] * ls[1] + ws[2] * ls[2]
        attn = sum(_expand_heads(w / den) * o for w, o in zip(ws, outs))

    z_b = _dot(h, wzb_ref[...])
    y_b = (attn * _silu(z_b)).astype(bf16)

    gl = _dot(h, wg_ref[...])
    p_a = _dot(y_a.astype(bf16), wpa_ref[...])
    p_b = _dot(y_b, wpb_ref[...])
    merged = jax.nn.sigmoid(gl[:, :D_MODEL]) * p_a + jax.nn.sigmoid(gl[:, D_MODEL:]) * p_b
    out = _dot(merged.astype(bf16), wout_ref[...])
    normed = out * lax.rsqrt(jnp.mean(out * out, axis=-1, keepdims=True) + EPS) * gpost_ref[...]
    y = x + gate * normed
    if sample:
        y_ref[...] = y
    else:
        y_ref[0] = y


def _full(shape):
    nd = len(shape)
    return pl.BlockSpec(shape, lambda *_: (0,) * nd)


def _weight_specs(weights):
    return [_full(w.shape) for w in weights]


def _rest_prompt(x, scale, shift, gate, vecs, weights, attn_parts):
    b = x.shape[0]
    tiles = SEQ // TILE
    per_b = lambda i, c: (i, 0, 0)
    tile3 = lambda i, c: (i, c, 0)
    tile4 = lambda i, c: (i, 0, c, 0)
    part_args, part_specs, scratch = [], [], []
    for (o, m, l), dil in zip(attn_parts, DILATIONS):
        for a, width in ((o, QW), (m, LANES), (l, LANES)):
            if dil == 1:
                part_args.append(a)
                part_specs.append(pl.BlockSpec((1, TILE, width), tile3))
            else:
                part_args.append(a.reshape(b, dil, SEQ // dil, width))
                part_specs.append(pl.BlockSpec((1, dil, TILE // dil, width), tile4))
        if dil != 1:
            scratch += [pltpu.VMEM((QW // LANES, TILE, LANES), f32),
                        pltpu.VMEM((TILE, LANES), f32), pltpu.VMEM((TILE, LANES), f32)]
    in_specs = (
        [pl.BlockSpec((1, TILE, D_MODEL), tile3),
         pl.BlockSpec((1, 1, D_MODEL), per_b), pl.BlockSpec((1, 1, D_MODEL), per_b),
         pl.BlockSpec((1, 1, D_MODEL), per_b)]
        + [_full(v.shape) for v in vecs] + _weight_specs(weights) + part_specs)
    return pl.pallas_call(
        functools.partial(_rest_kernel, sample=False),
        grid=(b, tiles),
        in_specs=in_specs,
        out_specs=pl.BlockSpec((1, TILE, D_MODEL), tile3),
        out_shape=jax.ShapeDtypeStruct((b, SEQ, D_MODEL), f32),
        scratch_shapes=scratch,
        name="rest_p",
        compiler_params=pltpu.CompilerParams(
            dimension_semantics=("arbitrary", "arbitrary"), vmem_limit_bytes=VMEM_LIMIT),
    )(x, scale, shift, gate, *vecs, *weights, *part_args)


def _rest_sample(x, scale, shift, gate, vecs, weights, attn):
    n = x.shape[0]
    args = (x, scale, shift, gate, *vecs, *weights, attn)
    return pl.pallas_call(
        functools.partial(_rest_kernel, sample=True),
        in_specs=[_full(a.shape) for a in args],
        out_specs=[_full((n, D_MODEL)), _full((n, D_MODEL))],
        out_shape=[jax.ShapeDtypeStruct((n, D_MODEL), f32), jax.ShapeDtypeStruct((n, D_MODEL), f32)],
        grid=(1,),
        name="rest_s",
        compiler_params=pltpu.CompilerParams(
            dimension_semantics=("arbitrary",), vmem_limit_bytes=VMEM_LIMIT),
    )(*args)


def _kv_rows(kv, batch, rows):
    return kv.reshape(1, batch, rows, 2, N_HEADS, HEAD_DIM)


def kernel(x_prompt, x_sample, cache_kv_w128, cache_kv_w512, cache_kv_w2048, c_prompt, c_sample, w_cond, b_cond, g_pre, w_in, ln_v_g, ln_v_b, w_spatial, b_spatial, w_proj_a, w_proj_b, w_out, g_post):
    assert w_in.shape[0] == 1, "single layer"
    bp, seq, _ = x_prompt.shape
    bs, t_new, _ = x_sample.shape
    assert seq == SEQ and t_new == T_NEW

    w_in_b = w_in[0].astype(bf16)
    w_a = w_in_b[:, :3 * D_MODEL]
    q0, k0, v0 = 3 * D_MODEL, 3 * D_MODEL + 3 * QW, 3 * D_MODEL + 6 * QW
    w_qkv_all = w_in_b[:, q0:q0 + 9 * QW]
    w_qkv_pat = jnp.stack([
        jnp.concatenate([w_in_b[:, q0 + g * QW:q0 + (g + 1) * QW],
                         w_in_b[:, k0 + g * QW:k0 + (g + 1) * QW],
                         w_in_b[:, v0 + g * QW:v0 + (g + 1) * QW]], axis=1) for g in range(3)])
    zb0 = q0 + 9 * QW
    w_zb = w_in_b[:, zb0:zb0 + QW]
    w_gate = w_in_b[:, zb0 + QW:]
    causal = jnp.tril(jnp.ones((CHUNK, CHUNK), bool))
    w_sp = jnp.where(causal[None], w_spatial[0], 0.0)
    w_sp_p = w_sp.astype(bf16)
    w_sp_s = jnp.stack([jnp.kron(jnp.eye(bs, dtype=f32), w_sp[g, :T_NEW, :T_NEW]) for g in range(A_GROUPS)]).astype(bf16)
    b_sp_p = b_spatial[0].T
    b_sp_s = jnp.tile(b_spatial[0][:, :T_NEW].T, (bs, 1))
    weights_tail = (w_proj_a[0].astype(bf16), w_proj_b[0].astype(bf16), w_out[0].astype(bf16))
    vecs = (g_pre, g_post, ln_v_g, ln_v_b)

    mod = _cond(jnp.concatenate([c_prompt, c_sample], axis=0), w_cond[0].astype(bf16), b_cond)
    shift, scale, gate = mod[:, :D_MODEL], mod[:, D_MODEL:2 * D_MODEL], mod[:, 2 * D_MODEL:]
    mp = lambda a: a[:bp].reshape(bp, 1, D_MODEL)
    ms = lambda a: jnp.repeat(a[bp:], T_NEW, axis=0)

    res = [_attn_prompt(x_prompt, mp(scale), mp(shift), g_pre, w_qkv_pat, g) for g in range(3)]
    kv1, kv4, kv16 = (r[3] for r in res)
    y_p = _rest_prompt(x_prompt, mp(scale), mp(shift), mp(gate), vecs,
                       (w_a, w_zb, w_gate, w_sp_p, b_sp_p) + weights_tail, [r[:3] for r in res])
    kv_p128 = _kv_rows(kv1[:, SEQ - WINDOWS[0]:], bp, WINDOWS[0])
    d4, d16 = DILATIONS[1], DILATIONS[2]
    kv4 = kv4.reshape(bp, d4, SEQ // d4, 2 * QW)[:, :, (SEQ - WINDOWS[1]) // d4:]
    kv_p512 = _kv_rows(jnp.swapaxes(kv4, 1, 2), bp, WINDOWS[1])
    kv16 = kv16.reshape(bp, d16, SEQ // d16, 2 * QW)
    kv_p2048 = _kv_rows(jnp.swapaxes(kv16, 1, 2), bp, WINDOWS[2])

    n_s = bs * T_NEW
    xs = x_sample.reshape(n_s, D_MODEL)
    qkv_s = _qkv_sample(xs, ms(scale), ms(shift), g_pre, w_qkv_all)
    caches_t = []
    for cache, win in zip((cache_kv_w128, cache_kv_w512, cache_kv_w2048), WINDOWS):
        caches_t.append(jnp.transpose(cache[0], (0, 2, 3, 4, 1)).reshape(bs, 2 * QW, win))
    attn_s = _attn_sample(qkv_s.reshape(bs, T_NEW, 9 * QW), *caches_t).reshape(n_s, QW)
    y_s, v_n_s = _rest_sample(xs, ms(scale), ms(shift), ms(gate), vecs,
                              (w_a, w_zb, w_gate, w_sp_s, b_sp_s) + weights_tail, attn_s)
    kv_s = []
    for g in range(3):
        kq = qkv_s[:, 3 * QW + g * QW:3 * QW + (g + 1) * QW]
        vq = qkv_s[:, 6 * QW + g * QW:6 * QW + (g + 1) * QW]
        kv_s.append(_kv_rows(jnp.concatenate([kq, vq], axis=1), bs, T_NEW))

    return (y_p, y_s.reshape(bs, T_NEW, D_MODEL), kv_p128, kv_p512, kv_p2048,
            kv_s[0], kv_s[1], kv_s[2], v_n_s.reshape(1, bs, T_NEW, D_MODEL))
```

```python
import functools

import jax
import jax.numpy as jnp
from jax import lax
from jax.experimental import pallas as pl
from jax.experimental.pallas import tpu as pltpu

D_MODEL = 1024
SEQ = 2048
HEAD_DIM = 64
N_HEADS = 8
QW = N_HEADS * HEAD_DIM
WINDOWS = (128, 512, 2048)
DILATIONS = (1, 4, 16)
N_BACK = 128
CHUNK = 128
A_GROUPS = 8
EPS = 1e-6
NEG = -1e30
LANES = 128
VMEM_LIMIT = 56 * 1024 * 1024
Q_COL0 = 3 * D_MODEL
K_COL0 = Q_COL0 + 3 * QW
V_COL0 = K_COL0 + 3 * QW
ZB_COL0 = V_COL0 + 3 * QW
GATE_COL0 = ZB_COL0 + QW

f32 = jnp.float32
bf16 = jnp.bfloat16


def _silu(x):
    return x * jax.nn.sigmoid(x)


def _norm_mod(x, g_pre, scale, shift):
    y = x * lax.rsqrt(jnp.mean(x * x, axis=-1, keepdims=True) + EPS) * g_pre
    return y * (1.0 + scale) + shift


def _dot(a, b):
    return jnp.dot(a, b, preferred_element_type=f32)


def _dot_nt(a, b):
    return lax.dot_general(a, b, (((1,), (1,)), ((), ())), preferred_element_type=f32)


def _cond_kernel(c_ref, w_ref, b_ref, o_ref):
    o_ref[...] = _dot(_silu(c_ref[...]).astype(bf16), w_ref[...]) + b_ref[...]


def _cond(c_all, w_cond, b_cond):
    n = c_all.shape[0]
    return pl.pallas_call(
        _cond_kernel,
        out_shape=jax.ShapeDtypeStruct((n, 3 * D_MODEL), f32),
        name="cond",
        compiler_params=pltpu.CompilerParams(vmem_limit_bytes=VMEM_LIMIT),
    )(c_all, w_cond, b_cond)


ROWS_PER_STEP = 512
Q_BLOCK = 128
NORM_ROWS = 256


def _attn_kernel(x_ref, scale_ref, shift_ref, gpre_ref, wq_ref, wk_ref, wv_ref, wkvt_ref,
                 o_ref, m_ref, l_ref, kvt_ref, h_sc, k_sc, ve_sc, vo_sc, *, dil, window):
    c = pl.program_id(1)
    seg = SEQ // dil
    n_lc = D_MODEL // LANES

    @pl.when(c == 0)
    def _():
        zeros = jnp.zeros((Q_BLOCK, QW), bf16)
        k_sc[0:Q_BLOCK, :] = zeros
        ve_sc[0:Q_BLOCK, :] = zeros
        vo_sc[0:Q_BLOCK, :] = zeros
        for rb in range(SEQ // NORM_ROWS):
            rs = slice(rb * NORM_ROWS, (rb + 1) * NORM_ROWS)
            hb = _norm_mod(x_ref[0, rs, :], gpre_ref[...], scale_ref[0], shift_ref[0])
            for k in range(n_lc):
                h_sc[k, rs, :] = hb[:, k * LANES:(k + 1) * LANES]

    def rows_of(start, n):
        idx = pl.ds(pl.multiple_of(start, n), n) if dil == 1 else pl.ds(start, n, stride=dil)
        return jnp.concatenate([h_sc[k, idx, :] for k in range(n_lc)], axis=1)

    if seg >= ROWS_PER_STEP:
        per = seg // ROWS_PER_STEP
        h = rows_of((c // per) + (c % per) * ROWS_PER_STEP * dil, ROWS_PER_STEP)
    else:
        per = ROWS_PER_STEP // seg
        h = jnp.concatenate([rows_of(per * c + i, seg) for i in range(per)], axis=0)
    h = h.astype(bf16)
    base = pl.multiple_of(c * ROWS_PER_STEP, ROWS_PER_STEP)

    def tokens_t(start, n):
        return jnp.concatenate([h_sc[k, pl.ds(start, n), :] for k in range(n_lc)], axis=1).astype(bf16)

    if window == SEQ:
        kvt_ref[0] = _dot_nt(wkvt_ref[0], tokens_t(base, ROWS_PER_STEP))
    else:
        @pl.when(c == pl.num_programs(1) - 1)
        def _():
            kvt_ref[0] = _dot_nt(wkvt_ref[0], tokens_t(SEQ - window, window))

    low_w = (lax.broadcasted_iota(jnp.int32, (ROWS_PER_STEP, QW), 1) % LANES) < HEAD_DIM
    v = _dot(h, wv_ref[...])
    new_rows = pl.ds(Q_BLOCK + base, ROWS_PER_STEP)
    k_sc[new_rows, :] = _dot(h, wk_ref[...]).astype(bf16)
    ve_sc[new_rows, :] = jnp.where(low_w, v, 1.0).astype(bf16)
    vo_sc[new_rows, :] = jnp.where(low_w, 1.0, v).astype(bf16)
    q = _dot(h, wq_ref[...]) * (HEAD_DIM ** -0.5)
    q_even = jnp.where(low_w, q, 0.0).astype(bf16)
    q_odd = jnp.where(low_w, 0.0, q).astype(bf16)

    n_blocks = ROWS_PER_STEP // Q_BLOCK
    pairs = N_HEADS // 2
    qi = lax.broadcasted_iota(jnp.int32, (2 * Q_BLOCK, 2 * Q_BLOCK), 0) % Q_BLOCK
    kj = lax.broadcasted_iota(jnp.int32, (2 * Q_BLOCK, 2 * Q_BLOCK), 1)
    lane = lax.broadcasted_iota(jnp.int32, (Q_BLOCK, LANES), 1)
    low = lane < HEAD_DIM

    def has_prev(j):
        return seg > ROWS_PER_STEP or (j * Q_BLOCK) % seg != 0

    def key_rows(j):
        r0 = pl.multiple_of(base + j * Q_BLOCK, Q_BLOCK)
        return pl.ds(r0, 2 * Q_BLOCK) if has_prev(j) else pl.ds(r0 + Q_BLOCK, Q_BLOCK)

    def mask_of(j):
        if not has_prev(j):
            own = (2 * Q_BLOCK, Q_BLOCK)
            return (lax.broadcasted_iota(jnp.int32, own, 1)
                    <= lax.broadcasted_iota(jnp.int32, own, 0) % Q_BLOCK)
        off = jnp.where((base % seg) != 0, 0, Q_BLOCK) if (seg > ROWS_PER_STEP and j == 0) else 0
        in_prev = jnp.logical_and(kj < Q_BLOCK, kj >= qi + off)
        in_cur = jnp.logical_and(kj >= Q_BLOCK, kj - Q_BLOCK <= qi)
        return jnp.logical_or(in_prev, in_cur)

    def qk(j):
        rows = slice(j * Q_BLOCK, (j + 1) * Q_BLOCK)
        keys = key_rows(j)
        out = []
        for hp in range(pairs):
            cols = slice(hp * LANES, (hp + 1) * LANES)
            q_st = jnp.concatenate([q_even[rows, cols], q_odd[rows, cols]], axis=0)
            out.append(_dot_nt(q_st, k_sc[keys, cols]))
        return out

    def finish(j, scores):
        rows = slice(j * Q_BLOCK, (j + 1) * Q_BLOCK)
        keys = key_rows(j)
        mask = mask_of(j)
        ps, ms = [], []
        for hp in range(pairs):
            s = jnp.where(mask, scores[hp], NEG)
            for half in (s[:Q_BLOCK], s[Q_BLOCK:]):
                m = jnp.max(half, axis=-1, keepdims=True)
                ps.append(jnp.exp(half - m).astype(bf16))
                ms.append(m)
        o_parts = []
        m_tile = jnp.zeros((Q_BLOCK, LANES), f32)
        l_tile = jnp.ones((Q_BLOCK, LANES), f32)
        for hp in range(pairs):
            cols = slice(hp * LANES, (hp + 1) * LANES)
            acc_e = _dot(ps[2 * hp], ve_sc[keys, cols])
            acc_o = _dot(ps[2 * hp + 1], vo_sc[keys, cols])
            o_parts.append(jnp.where(low, acc_e, acc_o))
            l_tile = jnp.where(lane == HEAD_DIM + hp, acc_e, jnp.where(lane == hp, acc_o, l_tile))
            m_tile = jnp.where(lane == HEAD_DIM + hp, ms[2 * hp],
                               jnp.where(lane == hp, ms[2 * hp + 1], m_tile))
        o_ref[0, rows, :] = jnp.concatenate(o_parts, axis=1).astype(bf16)
        m_ref[0, rows, :] = m_tile
        l_ref[0, rows, :] = l_tile

    scores = qk(0)
    for j in range(n_blocks):
        nxt = qk(j + 1) if j + 1 < n_blocks else None
        finish(j, scores)
        scores = nxt


def _attn_prompt(x, scale, shift, g_pre, w_in_b, w_kvt, pattern):
    b = x.shape[0]
    steps = SEQ // ROWS_PER_STEP
    dil, window = DILATIONS[pattern], WINDOWS[pattern]
    q_blk, k_blk, v_blk = (Q_COL0 // QW + pattern, K_COL0 // QW + pattern, V_COL0 // QW + pattern)
    if window == SEQ:
        kvt_spec = pl.BlockSpec((1, 2 * QW, ROWS_PER_STEP), lambda i, c: (i, 0, c))
    else:
        kvt_spec = pl.BlockSpec((1, 2 * QW, window), lambda i, c: (i, 0, 0))
    return pl.pallas_call(
        functools.partial(_attn_kernel, dil=dil, window=window),
        grid=(b, steps),
        in_specs=[
            pl.BlockSpec((1, SEQ, D_MODEL), lambda i, c: (i, 0, 0)),
            pl.BlockSpec((1, 1, D_MODEL), lambda i, c: (i, 0, 0)),
            pl.BlockSpec((1, 1, D_MODEL), lambda i, c: (i, 0, 0)),
            pl.BlockSpec((1, D_MODEL), lambda i, c: (0, 0)),
            pl.BlockSpec((D_MODEL, QW), lambda i, c: (0, q_blk)),
            pl.BlockSpec((D_MODEL, QW), lambda i, c: (0, k_blk)),
            pl.BlockSpec((D_MODEL, QW), lambda i, c: (0, v_blk)),
            pl.BlockSpec((1, 2 * QW, D_MODEL), lambda i, c: (pattern, 0, 0)),
        ],
        out_specs=[
            pl.BlockSpec((1, ROWS_PER_STEP, QW), lambda i, c: (i, c, 0)),
            pl.BlockSpec((1, ROWS_PER_STEP, LANES), lambda i, c: (i, c, 0)),
            pl.BlockSpec((1, ROWS_PER_STEP, LANES), lambda i, c: (i, c, 0)),
            kvt_spec,
        ],
        out_shape=[
            jax.ShapeDtypeStruct((b, SEQ, QW), bf16),
            jax.ShapeDtypeStruct((b, SEQ, LANES), f32),
            jax.ShapeDtypeStruct((b, SEQ, LANES), f32),
            jax.ShapeDtypeStruct((b, 2 * QW, window), f32),
        ],
        scratch_shapes=[pltpu.VMEM((D_MODEL // LANES, SEQ, LANES), f32),
                        pltpu.VMEM((SEQ + Q_BLOCK, QW), bf16), pltpu.VMEM((SEQ + Q_BLOCK, QW), bf16),
                        pltpu.VMEM((SEQ + Q_BLOCK, QW), bf16)],
        name=f"attn_d{dil}",
        compiler_params=pltpu.CompilerParams(
            dimension_semantics=("arbitrary", "arbitrary"), vmem_limit_bytes=VMEM_LIMIT),
    )(x, scale, shift, g_pre, w_in_b, w_in_b, w_in_b, w_kvt)


def _qkv_s_kernel(x_ref, scale_ref, shift_ref, gpre_ref, w_ref, o_ref):
    h = _norm_mod(x_ref[...], gpre_ref[...], scale_ref[...], shift_ref[...]).astype(bf16)
    o_ref[...] = _dot(h, w_ref[...])


def _qkv_sample(x, scale, shift, g_pre, w_in_b):
    n = x.shape[0]
    cols = 9 * QW
    tile = 3 * QW
    first = Q_COL0 // tile
    return pl.pallas_call(
        _qkv_s_kernel,
        grid=(cols // tile,),
        in_specs=[
            pl.BlockSpec((n, D_MODEL), lambda j: (0, 0)),
            pl.BlockSpec((n, D_MODEL), lambda j: (0, 0)),
            pl.BlockSpec((n, D_MODEL), lambda j: (0, 0)),
            pl.BlockSpec((1, D_MODEL), lambda j: (0, 0)),
            pl.BlockSpec((D_MODEL, tile), lambda j: (0, first + j)),
        ],
        out_specs=pl.BlockSpec((n, tile), lambda j: (0, j)),
        out_shape=jax.ShapeDtypeStruct((n, cols), f32),
        name="qkv_s",
        compiler_params=pltpu.CompilerParams(
            dimension_semantics=("arbitrary",), vmem_limit_bytes=VMEM_LIMIT),
    )(x, scale, shift, g_pre, w_in_b)


T_NEW = 8


def _attn_s_kernel(qkv_ref, c0_ref, c1_ref, c2_ref, o_ref, new_sc):
    @pl.when(pl.program_id(0) == 0)
    def _():
        new_sc[...] = jnp.zeros_like(new_sc)

    qkv = qkv_ref[0]
    new_sc[0:T_NEW, :] = qkv[:, 3 * QW:]
    q_all = qkv[:, :3 * QW] * (HEAD_DIM ** -0.5)

    lane8 = lax.broadcasted_iota(jnp.int32, (2 * T_NEW, LANES), 1)
    low = lane8 < HEAD_DIM
    t_idx = lax.broadcasted_iota(jnp.int32, (2 * T_NEW, LANES), 0) % T_NEW
    caches = (c0_ref, c1_ref, c2_ref)
    o_parts = []
    for hp in range(N_HEADS // 2):
        m_run = jnp.full((2 * T_NEW, 1), NEG, f32)
        l_run = jnp.zeros((2 * T_NEW, 1), f32)
        acc_run = jnp.zeros((2 * T_NEW, LANES), f32)
        for g, dil in enumerate(DILATIONS):
            rows = WINDOWS[g]
            cref = caches[g]
            q2 = q_all[:, g * QW + hp * LANES: g * QW + (hp + 1) * LANES]
            q_st = jnp.concatenate([jnp.where(low[:T_NEW], q2, 0.0), jnp.where(low[:T_NEW], 0.0, q2)],
                                   axis=0).astype(bf16)
            kt = cref[0, hp * LANES:(hp + 1) * LANES, :].astype(bf16)
            vt = cref[0, QW + hp * LANES: QW + (hp + 1) * LANES, :].astype(bf16)
            k_new = new_sc[:, g * QW + hp * LANES: g * QW + (hp + 1) * LANES].astype(bf16)
            v_new = new_sc[:, 3 * QW + g * QW + hp * LANES: 3 * QW + g * QW + (hp + 1) * LANES].astype(bf16)

            s_c = _dot(q_st, kt)
            rho = lax.broadcasted_iota(jnp.int32, (2 * T_NEW, rows), 1)
            tq = lax.broadcasted_iota(jnp.int32, (2 * T_NEW, rows), 0) % T_NEW
            ok_c = jnp.logical_and(rho >= tq, ((rho - tq) & (dil - 1)) == 0)
            s_c = jnp.where(ok_c, s_c, NEG)
            s_n = _dot_nt(q_st, k_new)
            ok_n = jnp.logical_and(lane8 <= t_idx, ((t_idx - lane8) & (dil - 1)) == 0)
            s_n = jnp.where(ok_n, s_n, NEG)

            m_g = jnp.maximum(jnp.max(s_c, axis=-1, keepdims=True), jnp.max(s_n, axis=-1, keepdims=True))
            m_new = jnp.maximum(m_run, m_g)
            p_c = jnp.exp(s_c - m_new)
            p_n = jnp.exp(s_n - m_new)
            alpha = jnp.exp(m_run - m_new)
            l_run = alpha * l_run + jnp.sum(p_c, axis=-1, keepdims=True) + jnp.sum(p_n, axis=-1, keepdims=True)
            acc_run = alpha * acc_run + _dot_nt(p_c.astype(bf16), vt) + _dot(p_n.astype(bf16), v_new)
            m_run = m_new
        out = acc_run / l_run
        o_parts.append(jnp.where(low[:T_NEW], out[:T_NEW], out[T_NEW:]))
    o_ref[0] = jnp.concatenate(o_parts, axis=1)


def _attn_sample(qkv, c0, c1, c2):
    n = qkv.shape[0]
    return pl.pallas_call(
        _attn_s_kernel,
        grid=(n,),
        in_specs=[
            pl.BlockSpec((1, T_NEW, 9 * QW), lambda i: (i, 0, 0)),
            pl.BlockSpec((1, 2 * QW, WINDOWS[0]), lambda i: (i, 0, 0)),
            pl.BlockSpec((1, 2 * QW, WINDOWS[1]), lambda i: (i, 0, 0)),
            pl.BlockSpec((1, 2 * QW, WINDOWS[2]), lambda i: (i, 0, 0)),
        ],
        out_specs=pl.BlockSpec((1, T_NEW, QW), lambda i: (i, 0, 0)),
        out_shape=jax.ShapeDtypeStruct((n, T_NEW, QW), f32),
        scratch_shapes=[pltpu.VMEM((LANES, 6 * QW), f32)],
        name="attn_s",
        compiler_params=pltpu.CompilerParams(
            dimension_semantics=("arbitrary",), vmem_limit_bytes=VMEM_LIMIT),
    )(qkv, c0, c1, c2)


TILE = 256


def _expand_heads(w):
    r = lax.broadcasted_iota(jnp.int32, (LANES, QW), 0)
    head = lax.broadcasted_iota(jnp.int32, (LANES, QW), 1) // HEAD_DIM
    expand = (r == head // 2 + HEAD_DIM * (1 - head % 2)).astype(bf16)
    hi = w.astype(bf16)
    lo = (w - hi.astype(f32)).astype(bf16)
    return _dot(hi, expand) + _dot(lo, expand)


def _rest_kernel(*refs, sample):
    if sample:
        (x_ref, scale_ref, shift_ref, gate_ref, gpre_ref, gpost_ref, lng_ref, lnb_ref,
         wa_ref, wzb_ref, wg_ref, wsp_ref, bsp_ref, wpa_ref, wpb_ref, wout_ref,
         attn_ref, y_ref, vn_ref) = refs
        x = x_ref[...]
        scale, shift, gate = scale_ref[...], shift_ref[...], gate_ref[...]
    else:
        (x_ref, scale_ref, shift_ref, gate_ref, gpre_ref, gpost_ref, lng_ref, lnb_ref,
         wa_ref, wzb_ref, wg_ref, wsp_ref, bsp_ref, wpa_ref, wpb_ref, wout_ref,
         o1_ref, m1_ref, l1_ref, o4_ref, m4_ref, l4_ref, o16_ref, m16_ref, l16_ref, y_ref,
         o4_sc, m4_sc, l4_sc, o16_sc, m16_sc, l16_sc) = refs
        x = x_ref[0]
        scale, shift, gate = scale_ref[0], shift_ref[0], gate_ref[0]

    rows = x.shape[0]
    h = _norm_mod(x, gpre_ref[...], scale, shift).astype(bf16)

    pa = _dot(h, wa_ref[...])
    u_a, v_a, z_a = pa[:, :D_MODEL], pa[:, D_MODEL:2 * D_MODEL], pa[:, 2 * D_MODEL:]
    mu = jnp.mean(v_a, axis=-1, keepdims=True)
    cen = v_a - mu
    var = jnp.mean(cen * cen, axis=-1, keepdims=True)
    v_n = cen * lax.rsqrt(var + EPS) * lng_ref[...] + lnb_ref[...]
    v_nb = v_n.astype(bf16)
    if sample:
        vn_ref[...] = v_n
        zs = jnp.concatenate(
            [_dot(wsp_ref[g], v_nb[:, g * LANES:(g + 1) * LANES]) + bsp_ref[:, g:g + 1]
             for g in range(A_GROUPS)], axis=1)
    else:
        chunks = []
        for ck in range(rows // CHUNK):
            rs = slice(ck * CHUNK, (ck + 1) * CHUNK)
            chunks.append(jnp.concatenate(
                [_dot(wsp_ref[g], v_nb[rs, g * LANES:(g + 1) * LANES]) + bsp_ref[:, g:g + 1]
                 for g in range(A_GROUPS)], axis=1))
        zs = jnp.concatenate(chunks, axis=0)
    y_a = u_a * zs * _silu(z_a)

    if sample:
        attn = attn_ref[...]
    else:
        for osc, msc, lsc, oref, mref, lref, dil in (
                (o4_sc, m4_sc, l4_sc, o4_ref, m4_ref, l4_ref, DILATIONS[1]),
                (o16_sc, m16_sc, l16_sc, o16_ref, m16_ref, l16_ref, DILATIONS[2])):
            n = rows // dil
            for r in range(dil):
                o_r = oref[0, r].astype(f32)
                for k in range(QW // LANES):
                    osc[k, pl.ds(r, n, stride=dil), :] = o_r[:, k * LANES:(k + 1) * LANES]
                msc[pl.ds(r, n, stride=dil), :] = mref[0, r]
                lsc[pl.ds(r, n, stride=dil), :] = lref[0, r]
        ms = (m1_ref[0], m4_sc[...], m16_sc[...])
        ls = (l1_ref[0], l4_sc[...], l16_sc[...])
        unchunk = lambda sc: jnp.concatenate([sc[k] for k in range(QW // LANES)], axis=1)
        outs = (o1_ref[0].astype(f32), unchunk(o4_sc), unchunk(o16_sc))
        m_all = jnp.maximum(jnp.maximum(ms[0], ms[1]), ms[2])
        ws = [jnp.exp(m - m_all) for m in ms]
        den = ws[0] * ls[0] + ws[1] * ls[1] + ws[2] * ls[2]
        attn = sum(_expand_heads(w / den) * o for w, o in zip(ws, outs))

    z_b = _dot(h, wzb_ref[...])
    y_b = (attn * _silu(z_b)).astype(bf16)

    gl = _dot(h, wg_ref[...])
    p_a = _dot(y_a.astype(bf16), wpa_ref[...])
    p_b = _dot(y_b, wpb_ref[...])
    merged = jax.nn.sigmoid(gl[:, :D_MODEL]) * p_a + jax.nn.sigmoid(gl[:, D_MODEL:]) * p_b
    out = _dot(merged.astype(bf16), wout_ref[...])
    normed = out * lax.rsqrt(jnp.mean(out * out, axis=-1, keepdims=True) + EPS) * gpost_ref[...]
    y = x + gate * normed
    if sample:
        y_ref[...] = y
    else:
        y_ref[0] = y


def _full(shape):
    nd = len(shape)
    return pl.BlockSpec(shape, lambda *_: (0,) * nd)


def _weight_specs(weights):
    in_proj = [pl.BlockSpec((D_MODEL, 3 * D_MODEL), lambda *_: (0, 0)),
               pl.BlockSpec((D_MODEL, QW), lambda *_: (0, ZB_COL0 // QW)),
               pl.BlockSpec((D_MODEL, 2 * D_MODEL), lambda *_: (0, GATE_COL0 // (2 * D_MODEL)))]
    return in_proj + [_full(w.shape) for w in weights[3:]]


def _rest_prompt(x, scale, shift, gate, vecs, weights, attn_parts):
    b = x.shape[0]
    tiles = SEQ // TILE
    per_b = lambda i, c: (i, 0, 0)
    tile3 = lambda i, c: (i, c, 0)
    tile4 = lambda i, c: (i, 0, c, 0)
    part_args, part_specs, scratch = [], [], []
    for (o, m, l), dil in zip(attn_parts, DILATIONS):
        for a, width in ((o, QW), (m, LANES), (l, LANES)):
            if dil == 1:
                part_args.append(a)
                part_specs.append(pl.BlockSpec((1, TILE, width), tile3))
            else:
                part_args.append(a.reshape(b, dil, SEQ // dil, width))
                part_specs.append(pl.BlockSpec((1, dil, TILE // dil, width), tile4))
        if dil != 1:
            scratch += [pltpu.VMEM((QW // LANES, TILE, LANES), f32),
                        pltpu.VMEM((TILE, LANES), f32), pltpu.VMEM((TILE, LANES), f32)]
    in_specs = (
        [pl.BlockSpec((1, TILE, D_MODEL), tile3),
         pl.BlockSpec((1, 1, D_MODEL), per_b), pl.BlockSpec((1, 1, D_MODEL), per_b),
         pl.BlockSpec((1, 1, D_MODEL), per_b)]
        + [_full(v.shape) for v in vecs] + _weight_specs(weights) + part_specs)
    return pl.pallas_call(
        functools.partial(_rest_kernel, sample=False),
        grid=(b, tiles),
        in_specs=in_specs,
        out_specs=pl.BlockSpec((1, TILE, D_MODEL), tile3),
        out_shape=jax.ShapeDtypeStruct((b, SEQ, D_MODEL), f32),
        scratch_shapes=scratch,
        name="rest_p",
        compiler_params=pltpu.CompilerParams(
            dimension_semantics=("arbitrary", "arbitrary"), vmem_limit_bytes=VMEM_LIMIT),
    )(x, scale, shift, gate, *vecs, *weights, *part_args)


def _rest_sample(x, scale, shift, gate, vecs, weights, attn):
    n = x.shape[0]
    args = (x, scale, shift, gate, *vecs, *weights, attn)
    in_specs = ([_full(a.shape) for a in (x, scale, shift, gate, *vecs)] + _weight_specs(weights)
                + [_full(attn.shape)])
    return pl.pallas_call(
        functools.partial(_rest_kernel, sample=True),
        in_specs=in_specs,
        out_specs=[_full((n, D_MODEL)), _full((n, D_MODEL))],
        out_shape=[jax.ShapeDtypeStruct((n, D_MODEL), f32), jax.ShapeDtypeStruct((n, D_MODEL), f32)],
        grid=(1,),
        name="rest_s",
        compiler_params=pltpu.CompilerParams(
            dimension_semantics=("arbitrary",), vmem_limit_bytes=VMEM_LIMIT),
    )(*args)


def _kv_rows(kv, batch, rows):
    return kv.reshape(1, batch, rows, 2, N_HEADS, HEAD_DIM)


def kernel(x_prompt, x_sample, cache_kv_w128, cache_kv_w512, cache_kv_w2048, c_prompt, c_sample, w_cond, b_cond, g_pre, w_in, ln_v_g, ln_v_b, w_spatial, b_spatial, w_proj_a, w_proj_b, w_out, g_post):
    assert w_in.shape[0] == 1, "single layer"
    bp, seq, _ = x_prompt.shape
    bs, t_new, _ = x_sample.shape
    assert seq == SEQ and t_new == T_NEW

    w_in_b = w_in[0].astype(bf16)
    w_kvt = jnp.transpose(w_in[0][:, K_COL0:ZB_COL0].reshape(D_MODEL, 2, 3, QW), (2, 1, 3, 0))
    w_kvt = w_kvt.reshape(3, 2 * QW, D_MODEL).astype(bf16)
    causal = jnp.tril(jnp.ones((CHUNK, CHUNK), bool))
    w_sp = jnp.where(causal[None], w_spatial[0], 0.0)
    w_sp_p = w_sp.astype(bf16)
    n_s = bs * T_NEW
    eye = jnp.eye(bs, dtype=f32)
    w_sp_s = (eye[None, :, None, :, None] * w_sp[:, None, :T_NEW, None, :T_NEW])
    w_sp_s = w_sp_s.reshape(A_GROUPS, n_s, n_s).astype(bf16)
    b_sp_p = b_spatial[0].T
    b_sp_s = jnp.tile(b_spatial[0][:, :T_NEW].T, (bs, 1))
    weights_tail = (w_proj_a[0].astype(bf16), w_proj_b[0].astype(bf16), w_out[0].astype(bf16))
    vecs = (g_pre, g_post, ln_v_g, ln_v_b)

    mod = _cond(jnp.concatenate([c_prompt, c_sample], axis=0), w_cond[0].astype(bf16), b_cond)
    shift, scale, gate = mod[:, :D_MODEL], mod[:, D_MODEL:2 * D_MODEL], mod[:, 2 * D_MODEL:]
    mp = lambda a: a[:bp].reshape(bp, 1, D_MODEL)
    ms = lambda a: jnp.repeat(a[bp:], T_NEW, axis=0)

    res = [_attn_prompt(x_prompt, mp(scale), mp(shift), g_pre, w_in_b, w_kvt, g) for g in range(3)]
    in_proj = (w_in_b, w_in_b, w_in_b)
    y_p = _rest_prompt(x_prompt, mp(scale), mp(shift), mp(gate), vecs,
                       in_proj + (w_sp_p, b_sp_p) + weights_tail, [r[:3] for r in res])
    kv_p = [jnp.transpose(r[3].reshape(bp, 2, N_HEADS, HEAD_DIM, win), (0, 4, 1, 2, 3))[None]
            for r, win in zip(res, WINDOWS)]

    xs = x_sample.reshape(n_s, D_MODEL)
    qkv_s = _qkv_sample(xs, ms(scale), ms(shift), g_pre, w_in_b)
    caches_t = []
    for cache, win in zip((cache_kv_w128, cache_kv_w512, cache_kv_w2048), WINDOWS):
        caches_t.append(jnp.transpose(cache[0], (0, 2, 3, 4, 1)).reshape(bs, 2 * QW, win))
    attn_s = _attn_sample(qkv_s.reshape(bs, T_NEW, 9 * QW), *caches_t).reshape(n_s, QW)
    y_s, v_n_s = _rest_sample(xs, ms(scale), ms(shift), ms(gate), vecs,
                              in_proj + (w_sp_s, b_sp_s) + weights_tail, attn_s)
    kv_s = []
    for g in range(3):
        kq = qkv_s[:, 3 * QW + g * QW:3 * QW + (g + 1) * QW]
        vq = qkv_s[:, 6 * QW + g * QW:6 * QW + (g + 1) * QW]
        kv_s.append(_kv_rows(jnp.concatenate([kq, vq], axis=1), bs, T_NEW))

    return (y_p, y_s.reshape(bs, T_NEW, D_MODEL), kv_p[0], kv_p[1], kv_p[2],
            kv_s[0], kv_s[1], kv_s[2], v_n_s.reshape(1, bs, T_NEW, D_MODEL))
```

```python
import functools

import jax
import jax.numpy as jnp
from jax import lax
from jax.experimental import pallas as pl
from jax.experimental.pallas import tpu as pltpu

D_MODEL = 1024
SEQ = 2048
HEAD_DIM = 64
N_HEADS = 8
QW = N_HEADS * HEAD_DIM
WINDOWS = (128, 512, 2048)
DILATIONS = (1, 4, 16)
N_BACK = 128
CHUNK = 128
A_GROUPS = 8
EPS = 1e-6
NEG = -1e30
LANES = 128
VMEM_LIMIT = 56 * 1024 * 1024
Q_COL0 = 3 * D_MODEL
K_COL0 = Q_COL0 + 3 * QW
V_COL0 = K_COL0 + 3 * QW
ZB_COL0 = V_COL0 + 3 * QW
GATE_COL0 = ZB_COL0 + QW

f32 = jnp.float32
bf16 = jnp.bfloat16


def _silu(x):
    return x * jax.nn.sigmoid(x)


def _norm_mod(x, g_pre, scale, shift):
    y = x * lax.rsqrt(jnp.mean(x * x, axis=-1, keepdims=True) + EPS) * g_pre
    return y * (1.0 + scale) + shift


def _dot(a, b):
    return jnp.dot(a, b, preferred_element_type=f32)


def _dot_nt(a, b):
    return lax.dot_general(a, b, (((1,), (1,)), ((), ())), preferred_element_type=f32)


def _cond_kernel(c_ref, w_ref, b_ref, o_ref):
    o_ref[...] = _dot(_silu(c_ref[...]).astype(bf16), w_ref[...].astype(bf16)) + b_ref[...]


def _cond(c_all, w_cond, b_cond):
    n = c_all.shape[0]
    return pl.pallas_call(
        _cond_kernel,
        out_shape=jax.ShapeDtypeStruct((n, 3 * D_MODEL), f32),
        name="cond",
        compiler_params=pltpu.CompilerParams(vmem_limit_bytes=VMEM_LIMIT),
    )(c_all, w_cond, b_cond)


ROWS_PER_STEP = 512
Q_BLOCK = 128
NORM_ROWS = 256


def _attn_kernel(x_ref, scale_ref, shift_ref, gpre_ref, wq_ref, wk_ref, wv_ref, wkvt_ref,
                 o_ref, m_ref, l_ref, kvt_ref, h_sc, k_sc, ve_sc, vo_sc, *, dil, window):
    c = pl.program_id(1)
    seg = SEQ // dil
    n_lc = D_MODEL // LANES

    @pl.when(c == 0)
    def _():
        zeros = jnp.zeros((Q_BLOCK, QW), bf16)
        k_sc[0:Q_BLOCK, :] = zeros
        ve_sc[0:Q_BLOCK, :] = zeros
        vo_sc[0:Q_BLOCK, :] = zeros
        for rb in range(SEQ // NORM_ROWS):
            rs = slice(rb * NORM_ROWS, (rb + 1) * NORM_ROWS)
            hb = _norm_mod(x_ref[0, rs, :], gpre_ref[...], scale_ref[0], shift_ref[0])
            for k in range(n_lc):
                h_sc[k, rs, :] = hb[:, k * LANES:(k + 1) * LANES]

    def rows_of(start, n):
        idx = pl.ds(pl.multiple_of(start, n), n) if dil == 1 else pl.ds(start, n, stride=dil)
        return jnp.concatenate([h_sc[k, idx, :] for k in range(n_lc)], axis=1)

    if seg >= ROWS_PER_STEP:
        per = seg // ROWS_PER_STEP
        h = rows_of((c // per) + (c % per) * ROWS_PER_STEP * dil, ROWS_PER_STEP)
    else:
        per = ROWS_PER_STEP // seg
        h = jnp.concatenate([rows_of(per * c + i, seg) for i in range(per)], axis=0)
    h = h.astype(bf16)
    base = pl.multiple_of(c * ROWS_PER_STEP, ROWS_PER_STEP)

    def tokens_t(start, n):
        return jnp.concatenate([h_sc[k, pl.ds(start, n), :] for k in range(n_lc)], axis=1).astype(bf16)

    if window == SEQ:
        kvt_ref[0] = _dot_nt(wkvt_ref[0], tokens_t(base, ROWS_PER_STEP))
    else:
        @pl.when(c == pl.num_programs(1) - 1)
        def _():
            kvt_ref[0] = _dot_nt(wkvt_ref[0], tokens_t(SEQ - window, window))

    low_w = (lax.broadcasted_iota(jnp.int32, (ROWS_PER_STEP, QW), 1) % LANES) < HEAD_DIM
    v = _dot(h, wv_ref[...])
    new_rows = pl.ds(Q_BLOCK + base, ROWS_PER_STEP)
    k_sc[new_rows, :] = _dot(h, wk_ref[...]).astype(bf16)
    ve_sc[new_rows, :] = jnp.where(low_w, v, 1.0).astype(bf16)
    vo_sc[new_rows, :] = jnp.where(low_w, 1.0, v).astype(bf16)
    q = _dot(h, wq_ref[...]) * (HEAD_DIM ** -0.5)
    q_even = jnp.where(low_w, q, 0.0).astype(bf16)
    q_odd = jnp.where(low_w, 0.0, q).astype(bf16)

    n_blocks = ROWS_PER_STEP // Q_BLOCK
    pairs = N_HEADS // 2
    qi = lax.broadcasted_iota(jnp.int32, (2 * Q_BLOCK, 2 * Q_BLOCK), 0) % Q_BLOCK
    kj = lax.broadcasted_iota(jnp.int32, (2 * Q_BLOCK, 2 * Q_BLOCK), 1)
    lane = lax.broadcasted_iota(jnp.int32, (Q_BLOCK, LANES), 1)
    low = lane < HEAD_DIM

    def has_prev(j):
        return seg > ROWS_PER_STEP or (j * Q_BLOCK) % seg != 0

    def key_rows(j):
        r0 = pl.multiple_of(base + j * Q_BLOCK, Q_BLOCK)
        return pl.ds(r0, 2 * Q_BLOCK) if has_prev(j) else pl.ds(r0 + Q_BLOCK, Q_BLOCK)

    def mask_of(j):
        if not has_prev(j):
            own = (2 * Q_BLOCK, Q_BLOCK)
            return (lax.broadcasted_iota(jnp.int32, own, 1)
                    <= lax.broadcasted_iota(jnp.int32, own, 0) % Q_BLOCK)
        off = jnp.where((base % seg) != 0, 0, Q_BLOCK) if (seg > ROWS_PER_STEP and j == 0) else 0
        in_prev = jnp.logical_and(kj < Q_BLOCK, kj >= qi + off)
        in_cur = jnp.logical_and(kj >= Q_BLOCK, kj - Q_BLOCK <= qi)
        return jnp.logical_or(in_prev, in_cur)

    def qk(j):
        rows = slice(j * Q_BLOCK, (j + 1) * Q_BLOCK)
        keys = key_rows(j)
        out = []
        for hp in range(pairs):
            cols = slice(hp * LANES, (hp + 1) * LANES)
            q_st = jnp.concatenate([q_even[rows, cols], q_odd[rows, cols]], axis=0)
            out.append(_dot_nt(q_st, k_sc[keys, cols]))
        return out

    def finish(j, scores):
        rows = slice(j * Q_BLOCK, (j + 1) * Q_BLOCK)
        keys = key_rows(j)
        mask = mask_of(j)
        ps, ms = [], []
        for hp in range(pairs):
            s = jnp.where(mask, scores[hp], NEG)
            for half in (s[:Q_BLOCK], s[Q_BLOCK:]):
                m = jnp.max(half, axis=-1, keepdims=True)
                ps.append(jnp.exp(half - m).astype(bf16))
                ms.append(m)
        o_parts = []
        m_tile = jnp.zeros((Q_BLOCK, LANES), f32)
        l_tile = jnp.ones((Q_BLOCK, LANES), f32)
        for hp in range(pairs):
            cols = slice(hp * LANES, (hp + 1) * LANES)
            acc_e = _dot(ps[2 * hp], ve_sc[keys, cols])
            acc_o = _dot(ps[2 * hp + 1], vo_sc[keys, cols])
            o_parts.append(jnp.where(low, acc_e, acc_o))
            l_tile = jnp.where(lane == HEAD_DIM + hp, acc_e, jnp.where(lane == hp, acc_o, l_tile))
            m_tile = jnp.where(lane == HEAD_DIM + hp, ms[2 * hp],
                               jnp.where(lane == hp, ms[2 * hp + 1], m_tile))
        o_ref[0, rows, :] = jnp.concatenate(o_parts, axis=1).astype(bf16)
        m_ref[0, rows, :] = m_tile
        l_ref[0, rows, :] = l_tile

    scores = qk(0)
    for j in range(n_blocks):
        nxt = qk(j + 1) if j + 1 < n_blocks else None
        finish(j, scores)
        scores = nxt


def _attn_prompt(x, scale, shift, g_pre, w_in_b, w_kvt, pattern):
    b = x.shape[0]
    steps = SEQ // ROWS_PER_STEP
    dil, window = DILATIONS[pattern], WINDOWS[pattern]
    q_blk, k_blk, v_blk = (Q_COL0 // QW + pattern, K_COL0 // QW + pattern, V_COL0 // QW + pattern)
    if window == SEQ:
        kvt_spec = pl.BlockSpec((1, 2 * QW, ROWS_PER_STEP), lambda i, c: (i, 0, c))
    else:
        kvt_spec = pl.BlockSpec((1, 2 * QW, window), lambda i, c: (i, 0, 0))
    return pl.pallas_call(
        functools.partial(_attn_kernel, dil=dil, window=window),
        grid=(b, steps),
        in_specs=[
            pl.BlockSpec((1, SEQ, D_MODEL), lambda i, c: (i, 0, 0)),
            pl.BlockSpec((1, 1, D_MODEL), lambda i, c: (i, 0, 0)),
            pl.BlockSpec((1, 1, D_MODEL), lambda i, c: (i, 0, 0)),
            pl.BlockSpec((1, D_MODEL), lambda i, c: (0, 0)),
            pl.BlockSpec((D_MODEL, QW), lambda i, c: (0, q_blk)),
            pl.BlockSpec((D_MODEL, QW), lambda i, c: (0, k_blk)),
            pl.BlockSpec((D_MODEL, QW), lambda i, c: (0, v_blk)),
            pl.BlockSpec((1, 2 * QW, D_MODEL), lambda i, c: (pattern, 0, 0)),
        ],
        out_specs=[
            pl.BlockSpec((1, ROWS_PER_STEP, QW), lambda i, c: (i, c, 0)),
            pl.BlockSpec((1, ROWS_PER_STEP, LANES), lambda i, c: (i, c, 0)),
            pl.BlockSpec((1, ROWS_PER_STEP, LANES), lambda i, c: (i, c, 0)),
            kvt_spec,
        ],
        out_shape=[
            jax.ShapeDtypeStruct((b, SEQ, QW), bf16),
            jax.ShapeDtypeStruct((b, SEQ, LANES), f32),
            jax.ShapeDtypeStruct((b, SEQ, LANES), f32),
            jax.ShapeDtypeStruct((b, 2 * QW, window), f32),
        ],
        scratch_shapes=[pltpu.VMEM((D_MODEL // LANES, SEQ, LANES), f32),
                        pltpu.VMEM((SEQ + Q_BLOCK, QW), bf16), pltpu.VMEM((SEQ + Q_BLOCK, QW), bf16),
                        pltpu.VMEM((SEQ + Q_BLOCK, QW), bf16)],
        name=f"attn_d{dil}",
        compiler_params=pltpu.CompilerParams(
            dimension_semantics=("arbitrary", "arbitrary"), vmem_limit_bytes=VMEM_LIMIT),
    )(x, scale, shift, g_pre, w_in_b, w_in_b, w_in_b, w_kvt)


def _qkv_s_kernel(x_ref, scale_ref, shift_ref, gpre_ref, w_ref, o_ref):
    h = _norm_mod(x_ref[...], gpre_ref[...], scale_ref[...], shift_ref[...]).astype(bf16)
    o_ref[...] = _dot(h, w_ref[...])


def _qkv_sample(x, scale, shift, g_pre, w_in_b):
    n = x.shape[0]
    cols = 9 * QW
    tile = 3 * QW
    first = Q_COL0 // tile
    return pl.pallas_call(
        _qkv_s_kernel,
        grid=(cols // tile,),
        in_specs=[
            pl.BlockSpec((n, D_MODEL), lambda j: (0, 0)),
            pl.BlockSpec((n, D_MODEL), lambda j: (0, 0)),
            pl.BlockSpec((n, D_MODEL), lambda j: (0, 0)),
            pl.BlockSpec((1, D_MODEL), lambda j: (0, 0)),
            pl.BlockSpec((D_MODEL, tile), lambda j: (0, first + j)),
        ],
        out_specs=pl.BlockSpec((n, tile), lambda j: (0, j)),
        out_shape=jax.ShapeDtypeStruct((n, cols), f32),
        name="qkv_s",
        compiler_params=pltpu.CompilerParams(
            dimension_semantics=("arbitrary",), vmem_limit_bytes=VMEM_LIMIT),
    )(x, scale, shift, g_pre, w_in_b)


T_NEW = 8


def _attn_s_kernel(qkv_ref, c0_ref, c1_ref, c2_ref, o_ref, new_sc):
    @pl.when(pl.program_id(0) == 0)
    def _():
        new_sc[...] = jnp.zeros_like(new_sc)

    qkv = qkv_ref[0]
    new_sc[0:T_NEW, :] = qkv[:, 3 * QW:]
    q_all = qkv[:, :3 * QW] * (HEAD_DIM ** -0.5)

    lane8 = lax.broadcasted_iota(jnp.int32, (2 * T_NEW, LANES), 1)
    low = lane8 < HEAD_DIM
    t_idx = lax.broadcasted_iota(jnp.int32, (2 * T_NEW, LANES), 0) % T_NEW
    caches = (c0_ref, c1_ref, c2_ref)
    o_parts = []
    for hp in range(N_HEADS // 2):
        m_run = jnp.full((2 * T_NEW, 1), NEG, f32)
        l_run = jnp.zeros((2 * T_NEW, 1), f32)
        acc_run = jnp.zeros((2 * T_NEW, LANES), f32)
        for g, dil in enumerate(DILATIONS):
            rows = WINDOWS[g]
            cref = caches[g]
            q2 = q_all[:, g * QW + hp * LANES: g * QW + (hp + 1) * LANES]
            q_st = jnp.concatenate([jnp.where(low[:T_NEW], q2, 0.0), jnp.where(low[:T_NEW], 0.0, q2)],
                                   axis=0).astype(bf16)
            kt = cref[0, hp * LANES:(hp + 1) * LANES, :].astype(bf16)
            vt = cref[0, QW + hp * LANES: QW + (hp + 1) * LANES, :].astype(bf16)
            k_new = new_sc[:, g * QW + hp * LANES: g * QW + (hp + 1) * LANES].astype(bf16)
            v_new = new_sc[:, 3 * QW + g * QW + hp * LANES: 3 * QW + g * QW + (hp + 1) * LANES].astype(bf16)

            s_c = _dot(q_st, kt)
            rho = lax.broadcasted_iota(jnp.int32, (2 * T_NEW, rows), 1)
            tq = lax.broadcasted_iota(jnp.int32, (2 * T_NEW, rows), 0) % T_NEW
            ok_c = jnp.logical_and(rho >= tq, ((rho - tq) & (dil - 1)) == 0)
            s_c = jnp.where(ok_c, s_c, NEG)
            s_n = _dot_nt(q_st, k_new)
            ok_n = jnp.logical_and(lane8 <= t_idx, ((t_idx - lane8) & (dil - 1)) == 0)
            s_n = jnp.where(ok_n, s_n, NEG)

            m_g = jnp.maximum(jnp.max(s_c, axis=-1, keepdims=True), jnp.max(s_n, axis=-1, keepdims=True))
            m_new = jnp.maximum(m_run, m_g)
            p_c = jnp.exp(s_c - m_new)
            p_n = jnp.exp(s_n - m_new)
            alpha = jnp.exp(m_run - m_new)
            l_run = alpha * l_run + jnp.sum(p_c, axis=-1, keepdims=True) + jnp.sum(p_n, axis=-1, keepdims=True)
            acc_run = alpha * acc_run + _dot_nt(p_c.astype(bf16), vt) + _dot(p_n.astype(bf16), v_new)
            m_run = m_new
        out = acc_run / l_run
        o_parts.append(jnp.where(low[:T_NEW], out[:T_NEW], out[T_NEW:]))
    o_ref[0] = jnp.concatenate(o_parts, axis=1)


def _attn_sample(qkv, c0, c1, c2):
    n = qkv.shape[0]
    return pl.pallas_call(
        _attn_s_kernel,
        grid=(n,),
        in_specs=[
            pl.BlockSpec((1, T_NEW, 9 * QW), lambda i: (i, 0, 0)),
            pl.BlockSpec((1, 2 * QW, WINDOWS[0]), lambda i: (i, 0, 0)),
            pl.BlockSpec((1, 2 * QW, WINDOWS[1]), lambda i: (i, 0, 0)),
            pl.BlockSpec((1, 2 * QW, WINDOWS[2]), lambda i: (i, 0, 0)),
        ],
        out_specs=pl.BlockSpec((1, T_NEW, QW), lambda i: (i, 0, 0)),
        out_shape=jax.ShapeDtypeStruct((n, T_NEW, QW), f32),
        scratch_shapes=[pltpu.VMEM((LANES, 6 * QW), f32)],
        name="attn_s",
        compiler_params=pltpu.CompilerParams(
            dimension_semantics=("arbitrary",), vmem_limit_bytes=VMEM_LIMIT),
    )(qkv, c0, c1, c2)


TILE = 512


def _expand_heads(w):
    r = lax.broadcasted_iota(jnp.int32, (LANES, QW), 0)
    head = lax.broadcasted_iota(jnp.int32, (LANES, QW), 1) // HEAD_DIM
    expand = (r == head // 2 + HEAD_DIM * (1 - head % 2)).astype(bf16)
    hi = w.astype(bf16)
    lo = (w - hi.astype(f32)).astype(bf16)
    return _dot(hi, expand) + _dot(lo, expand)


def _rest_kernel(*refs, sample):
    if sample:
        (x_ref, scale_ref, shift_ref, gate_ref, gpre_ref, gpost_ref, lng_ref, lnb_ref,
         wa_ref, wzb_ref, wg_ref, wsp_ref, bsp_ref, wpa_ref, wpb_ref, wout_ref,
         attn_ref, y_ref, vn_ref) = refs
        x = x_ref[...]
        scale, shift, gate = scale_ref[...], shift_ref[...], gate_ref[...]
    else:
        (x_ref, scale_ref, shift_ref, gate_ref, gpre_ref, gpost_ref, lng_ref, lnb_ref,
         wa_ref, wzb_ref, wg_ref, wsp_ref, bsp_ref, wpa_ref, wpb_ref, wout_ref,
         o1_ref, m1_ref, l1_ref, o4_ref, m4_ref, l4_ref, o16_ref, m16_ref, l16_ref, y_ref,
         o4_sc, m4_sc, l4_sc, o16_sc, m16_sc, l16_sc) = refs
        x = x_ref[0]
        scale, shift, gate = scale_ref[0], shift_ref[0], gate_ref[0]

    rows = x.shape[0]
    h = _norm_mod(x, gpre_ref[...], scale, shift).astype(bf16)

    pa = _dot(h, wa_ref[...])
    u_a, v_a, z_a = pa[:, :D_MODEL], pa[:, D_MODEL:2 * D_MODEL], pa[:, 2 * D_MODEL:]
    mu = jnp.mean(v_a, axis=-1, keepdims=True)
    cen = v_a - mu
    var = jnp.mean(cen * cen, axis=-1, keepdims=True)
    v_n = cen * lax.rsqrt(var + EPS) * lng_ref[...] + lnb_ref[...]
    v_nb = v_n.astype(bf16)
    if sample:
        vn_ref[...] = v_n
        pick = (lax.broadcasted_iota(jnp.int32, (rows, CHUNK), 1)
                == lax.broadcasted_iota(jnp.int32, (rows, CHUNK), 0) % T_NEW).astype(bf16)
        same_seq = (lax.broadcasted_iota(jnp.int32, (rows, rows), 0) // T_NEW
                    == lax.broadcasted_iota(jnp.int32, (rows, rows), 1) // T_NEW)
        cols = []
        for g in range(A_GROUPS):
            tiled = _dot_nt(_dot(pick, wsp_ref[g]).astype(bf16), pick)
            w_blk = jnp.where(same_seq, tiled, 0.0).astype(bf16)
            cols.append(_dot(w_blk, v_nb[:, g * LANES:(g + 1) * LANES]) + bsp_ref[:, g:g + 1])
        zs = jnp.concatenate(cols, axis=1)
    else:
        n_ck = rows // CHUNK
        per_group = []
        for g in range(A_GROUPS):
            rhs = jnp.concatenate(
                [v_nb[ck * CHUNK:(ck + 1) * CHUNK, g * LANES:(g + 1) * LANES] for ck in range(n_ck)], axis=1)
            per_group.append(_dot(wsp_ref[g], rhs) + bsp_ref[:, g:g + 1])
        zs = jnp.concatenate(
            [jnp.concatenate([pg[:, ck * LANES:(ck + 1) * LANES] for pg in per_group], axis=1)
             for ck in range(n_ck)], axis=0)
    y_a = u_a * zs * _silu(z_a)

    if sample:
        attn = attn_ref[...]
    else:
        for osc, msc, lsc, oref, mref, lref, dil in (
                (o4_sc, m4_sc, l4_sc, o4_ref, m4_ref, l4_ref, DILATIONS[1]),
                (o16_sc, m16_sc, l16_sc, o16_ref, m16_ref, l16_ref, DILATIONS[2])):
            n = rows // dil
            for r in range(dil):
                o_r = oref[0, r].astype(f32)
                for k in range(QW // LANES):
                    osc[k, pl.ds(r, n, stride=dil), :] = o_r[:, k * LANES:(k + 1) * LANES]
                msc[pl.ds(r, n, stride=dil), :] = mref[0, r]
                lsc[pl.ds(r, n, stride=dil), :] = lref[0, r]
        ms = (m1_ref[0], m4_sc[...], m16_sc[...])
        ls = (l1_ref[0], l4_sc[...], l16_sc[...])
        unchunk = lambda sc: jnp.concatenate([sc[k] for k in range(QW // LANES)], axis=1)
        outs = (o1_ref[0].astype(f32), unchunk(o4_sc), unchunk(o16_sc))
        m_all = jnp.maximum(jnp.maximum(ms[0], ms[1]), ms[2])
        ws = [jnp.exp(m - m_all) for m in ms]
        den = ws[0] * ls[0] + ws[1] * ls[1] + ws[2] * ls[2]
        attn = sum(_expand_heads(w / den) * o for w, o in zip(ws, outs))

    z_b = _dot(h, wzb_ref[...])
    y_b = (attn * _silu(z_b)).astype(bf16)

    gl = _dot(h, wg_ref[...])
    p_a = _dot(y_a.astype(bf16), wpa_ref[...])
    p_b = _dot(y_b, wpb_ref[...])
    merged = jax.nn.sigmoid(gl[:, :D_MODEL]) * p_a + jax.nn.sigmoid(gl[:, D_MODEL:]) * p_b
    out = _dot(merged.astype(bf16), wout_ref[...])
    normed = out * lax.rsqrt(jnp.mean(out * out, axis=-1, keepdims=True) + EPS) * gpost_ref[...]
    y = x + gate * normed
    if sample:
        y_ref[...] = y
    else:
        y_ref[0] = y


def _full(shape):
    nd = len(shape)
    return pl.BlockSpec(shape, lambda *_: (0,) * nd)


def _weight_specs(weights):
    in_proj = [pl.BlockSpec((D_MODEL, 3 * D_MODEL), lambda *_: (0, 0)),
               pl.BlockSpec((D_MODEL, QW), lambda *_: (0, ZB_COL0 // QW)),
               pl.BlockSpec((D_MODEL, 2 * D_MODEL), lambda *_: (0, GATE_COL0 // (2 * D_MODEL)))]
    return in_proj + [_full(w.shape) for w in weights[3:]]


def _rest_prompt(x, scale, shift, gate, vecs, weights, attn_parts):
    b = x.shape[0]
    tiles = SEQ // TILE
    per_b = lambda i, c: (i, 0, 0)
    tile3 = lambda i, c: (i, c, 0)
    tile4 = lambda i, c: (i, 0, c, 0)
    part_args, part_specs, scratch = [], [], []
    for (o, m, l), dil in zip(attn_parts, DILATIONS):
        for a, width in ((o, QW), (m, LANES), (l, LANES)):
            if dil == 1:
                part_args.append(a)
                part_specs.append(pl.BlockSpec((1, TILE, width), tile3))
            else:
                part_args.append(a.reshape(b, dil, SEQ // dil, width))
                part_specs.append(pl.BlockSpec((1, dil, TILE // dil, width), tile4))
        if dil != 1:
            scratch += [pltpu.VMEM((QW // LANES, TILE, LANES), f32),
                        pltpu.VMEM((TILE, LANES), f32), pltpu.VMEM((TILE, LANES), f32)]
    in_specs = (
        [pl.BlockSpec((1, TILE, D_MODEL), tile3),
         pl.BlockSpec((1, 1, D_MODEL), per_b), pl.BlockSpec((1, 1, D_MODEL), per_b),
         pl.BlockSpec((1, 1, D_MODEL), per_b)]
        + [_full(v.shape) for v in vecs] + _weight_specs(weights) + part_specs)
    return pl.pallas_call(
        functools.partial(_rest_kernel, sample=False),
        grid=(b, tiles),
        in_specs=in_specs,
        out_specs=pl.BlockSpec((1, TILE, D_MODEL), tile3),
        out_shape=jax.ShapeDtypeStruct((b, SEQ, D_MODEL), f32),
        scratch_shapes=scratch,
        name="rest_p",
        compiler_params=pltpu.CompilerParams(
            dimension_semantics=("arbitrary", "arbitrary"), vmem_limit_bytes=VMEM_LIMIT),
    )(x, scale, shift, gate, *vecs, *weights, *part_args)


def _rest_sample(x, scale, shift, gate, vecs, weights, attn):
    n = x.shape[0]
    args = (x, scale, shift, gate, *vecs, *weights, attn)
    in_specs = ([_full(a.shape) for a in (x, scale, shift, gate, *vecs)] + _weight_specs(weights)
                + [_full(attn.shape)])
    return pl.pallas_call(
        functools.partial(_rest_kernel, sample=True),
        in_specs=in_specs,
        out_specs=[_full((n, D_MODEL)), _full((n, D_MODEL))],
        out_shape=[jax.ShapeDtypeStruct((n, D_MODEL), f32), jax.ShapeDtypeStruct((n, D_MODEL), f32)],
        grid=(1,),
        name="rest_s",
        compiler_params=pltpu.CompilerParams(
            dimension_semantics=("arbitrary",), vmem_limit_bytes=VMEM_LIMIT),
    )(*args)


def _kv_rows(kv, batch, rows):
    return kv.reshape(1, batch, rows, 2, N_HEADS, HEAD_DIM)


def kernel(x_prompt, x_sample, cache_kv_w128, cache_kv_w512, cache_kv_w2048, c_prompt, c_sample, w_cond, b_cond, g_pre, w_in, ln_v_g, ln_v_b, w_spatial, b_spatial, w_proj_a, w_proj_b, w_out, g_post):
    assert w_in.shape[0] == 1, "single layer"
    bp, seq, _ = x_prompt.shape
    bs, t_new, _ = x_sample.shape
    assert seq == SEQ and t_new == T_NEW

    w_in_b = w_in[0].astype(bf16)
    w_kvt = jnp.transpose(w_in[0][:, K_COL0:ZB_COL0].reshape(D_MODEL, 2, 3, QW), (2, 1, 3, 0))
    w_kvt = w_kvt.reshape(3, 2 * QW, D_MODEL).astype(bf16)
    causal = jnp.tril(jnp.ones((CHUNK, CHUNK), bool))
    w_sp = jnp.where(causal[None], w_spatial[0], 0.0)
    w_sp_p = w_sp.astype(bf16)
    n_s = bs * T_NEW
    b_sp_p = b_spatial[0].T
    b_sp_s = jnp.tile(b_spatial[0][:, :T_NEW].T, (bs, 1))
    weights_tail = (w_proj_a[0].astype(bf16), w_proj_b[0].astype(bf16), w_out[0].astype(bf16))
    vecs = (g_pre, g_post, ln_v_g, ln_v_b)

    mod = _cond(jnp.concatenate([c_prompt, c_sample], axis=0), w_cond[0], b_cond)
    shift, scale, gate = mod[:, :D_MODEL], mod[:, D_MODEL:2 * D_MODEL], mod[:, 2 * D_MODEL:]
    mp = lambda a: a[:bp].reshape(bp, 1, D_MODEL)
    ms = lambda a: jnp.repeat(a[bp:], T_NEW, axis=0)

    res = [_attn_prompt(x_prompt, mp(scale), mp(shift), g_pre, w_in_b, w_kvt, g) for g in range(3)]
    in_proj = (w_in_b, w_in_b, w_in_b)
    y_p = _rest_prompt(x_prompt, mp(scale), mp(shift), mp(gate), vecs,
                       in_proj + (w_sp_p, b_sp_p) + weights_tail, [r[:3] for r in res])
    kv_p = [jnp.transpose(r[3].reshape(bp, 2, N_HEADS, HEAD_DIM, win), (0, 4, 1, 2, 3))[None]
            for r, win in zip(res, WINDOWS)]

    xs = x_sample.reshape(n_s, D_MODEL)
    qkv_s = _qkv_sample(xs, ms(scale), ms(shift), g_pre, w_in_b)
    caches_t = []
    for cache, win in zip((cache_kv_w128, cache_kv_w512, cache_kv_w2048), WINDOWS):
        caches_t.append(jnp.transpose(cache[0], (0, 2, 3, 4, 1)).reshape(bs, 2 * QW, win))
    attn_s = _attn_sample(qkv_s.reshape(bs, T_NEW, 9 * QW), *caches_t).reshape(n_s, QW)
    y_s, v_n_s = _rest_sample(xs, ms(scale), ms(shift), ms(gate), vecs,
                              in_proj + (w_sp_p, b_sp_s) + weights_tail, attn_s)
    kv_s = []
    for g in range(3):
        kq = qkv_s[:, 3 * QW + g * QW:3 * QW + (g + 1) * QW]
        vq = qkv_s[:, 6 * QW + g * QW:6 * QW + (g + 1) * QW]
        kv_s.append(_kv_rows(jnp.concatenate([kq, vq], axis=1), bs, T_NEW))

    return (y_p, y_s.reshape(bs, T_NEW, D_MODEL), kv_p[0], kv_p[1], kv_p[2],
            kv_s[0], kv_s[1], kv_s[2], v_n_s.reshape(1, bs, T_NEW, D_MODEL))
```

```python
import functools

import jax
import jax.numpy as jnp
from jax import lax
from jax.experimental import pallas as pl
from jax.experimental.pallas import tpu as pltpu

D_MODEL = 1024
SEQ = 2048
HEAD_DIM = 64
N_HEADS = 8
QW = N_HEADS * HEAD_DIM
WINDOWS = (128, 512, 2048)
DILATIONS = (1, 4, 16)
N_BACK = 128
CHUNK = 128
A_GROUPS = 8
EPS = 1e-6
NEG = -1e30
LANES = 128
VMEM_LIMIT = 56 * 1024 * 1024
Q_COL0 = 3 * D_MODEL
K_COL0 = Q_COL0 + 3 * QW
V_COL0 = K_COL0 + 3 * QW
ZB_COL0 = V_COL0 + 3 * QW
GATE_COL0 = ZB_COL0 + QW

f32 = jnp.float32
bf16 = jnp.bfloat16


def _silu(x):
    return x * jax.nn.sigmoid(x)


def _norm_mod(x, g_pre, scale, shift):
    y = x * lax.rsqrt(jnp.mean(x * x, axis=-1, keepdims=True) + EPS) * g_pre
    return y * (1.0 + scale) + shift


def _dot(a, b):
    return jnp.dot(a, b, preferred_element_type=f32)


def _dot_nt(a, b):
    return lax.dot_general(a, b, (((1,), (1,)), ((), ())), preferred_element_type=f32)


def _cond_kernel(c_ref, w_ref, b_ref, o_ref):
    o_ref[...] = _dot(_silu(c_ref[...]).astype(bf16), w_ref[...].astype(bf16)) + b_ref[...]


def _cond(c_all, w_cond, b_cond):
    n = c_all.shape[0]
    return pl.pallas_call(
        _cond_kernel,
        out_shape=jax.ShapeDtypeStruct((n, 3 * D_MODEL), f32),
        name="cond",
        compiler_params=pltpu.CompilerParams(vmem_limit_bytes=VMEM_LIMIT),
    )(c_all, w_cond, b_cond)


ROWS_PER_STEP = 1024
Q_BLOCK = 128
NORM_ROWS = 256


def _attn_kernel(x_ref, scale_ref, shift_ref, gpre_ref, wq_ref, wk_ref, wv_ref, wkvt_ref,
                 o_ref, m_ref, l_ref, kvt_ref, h_sc, k_sc, ve_sc, vo_sc, *, dil, window):
    c = pl.program_id(1)
    seg = SEQ // dil
    n_lc = D_MODEL // LANES

    @pl.when(c == 0)
    def _():
        zeros = jnp.zeros((Q_BLOCK, QW), bf16)
        k_sc[0:Q_BLOCK, :] = zeros
        ve_sc[0:Q_BLOCK, :] = zeros
        vo_sc[0:Q_BLOCK, :] = zeros
        for rb in range(SEQ // NORM_ROWS):
            rs = slice(rb * NORM_ROWS, (rb + 1) * NORM_ROWS)
            hb = _norm_mod(x_ref[0, rs, :], gpre_ref[...], scale_ref[0], shift_ref[0])
            for k in range(n_lc):
                h_sc[k, rs, :] = hb[:, k * LANES:(k + 1) * LANES]

    def rows_of(start, n):
        idx = pl.ds(pl.multiple_of(start, n), n) if dil == 1 else pl.ds(start, n, stride=dil)
        return jnp.concatenate([h_sc[k, idx, :] for k in range(n_lc)], axis=1)

    if seg >= ROWS_PER_STEP:
        per = seg // ROWS_PER_STEP
        h = rows_of((c // per) + (c % per) * ROWS_PER_STEP * dil, ROWS_PER_STEP)
    else:
        per = ROWS_PER_STEP // seg
        h = jnp.concatenate([rows_of(per * c + i, seg) for i in range(per)], axis=0)
    h = h.astype(bf16)
    base = pl.multiple_of(c * ROWS_PER_STEP, ROWS_PER_STEP)

    def tokens_t(start, n):
        return jnp.concatenate([h_sc[k, pl.ds(start, n), :] for k in range(n_lc)], axis=1).astype(bf16)

    n_blocks = ROWS_PER_STEP // Q_BLOCK
    pairs = N_HEADS // 2

    kvt_chunk = None
    if window == SEQ:
        kvt_ref[0] = _dot_nt(wkvt_ref[0], tokens_t(base, ROWS_PER_STEP))
    else:

        @pl.when(c == pl.num_programs(1) - 1)
        def _():
            kvt_ref[0] = _dot_nt(wkvt_ref[0], tokens_t(SEQ - window, window))

    low_w = (lax.broadcasted_iota(jnp.int32, (ROWS_PER_STEP, QW), 1) % LANES) < HEAD_DIM
    v = _dot(h, wv_ref[...])
    new_rows = pl.ds(Q_BLOCK + base, ROWS_PER_STEP)
    k_sc[new_rows, :] = _dot(h, wk_ref[...]).astype(bf16)
    ve_sc[new_rows, :] = jnp.where(low_w, v, 1.0).astype(bf16)
    vo_sc[new_rows, :] = jnp.where(low_w, 1.0, v).astype(bf16)
    low_q = (lax.broadcasted_iota(jnp.int32, (Q_BLOCK, QW), 1) % LANES) < HEAD_DIM
    qi = lax.broadcasted_iota(jnp.int32, (2 * Q_BLOCK, 2 * Q_BLOCK), 0) % Q_BLOCK
    kj = lax.broadcasted_iota(jnp.int32, (2 * Q_BLOCK, 2 * Q_BLOCK), 1)
    lane = lax.broadcasted_iota(jnp.int32, (Q_BLOCK, LANES), 1)
    low = lane < HEAD_DIM

    def has_prev(j):
        return seg > ROWS_PER_STEP or (j * Q_BLOCK) % seg != 0

    def key_rows(j):
        r0 = pl.multiple_of(base + j * Q_BLOCK, Q_BLOCK)
        return pl.ds(r0, 2 * Q_BLOCK) if has_prev(j) else pl.ds(r0 + Q_BLOCK, Q_BLOCK)

    def mask_of(j):
        if not has_prev(j):
            own = (2 * Q_BLOCK, Q_BLOCK)
            return (lax.broadcasted_iota(jnp.int32, own, 1)
                    <= lax.broadcasted_iota(jnp.int32, own, 0) % Q_BLOCK)
        off = jnp.where((base % seg) != 0, 0, Q_BLOCK) if (seg > ROWS_PER_STEP and j == 0) else 0
        in_prev = jnp.logical_and(kj < Q_BLOCK, kj >= qi + off)
        in_cur = jnp.logical_and(kj >= Q_BLOCK, kj - Q_BLOCK <= qi)
        return jnp.logical_or(in_prev, in_cur)

    def qk(j):
        rows = slice(j * Q_BLOCK, (j + 1) * Q_BLOCK)
        keys = key_rows(j)
        q = _dot(h[rows], wq_ref[...]) * (HEAD_DIM ** -0.5)
        q_even = jnp.where(low_q, q, 0.0).astype(bf16)
        q_odd = jnp.where(low_q, 0.0, q).astype(bf16)
        out = []
        for hp in range(pairs):
            cols = slice(hp * LANES, (hp + 1) * LANES)
            q_st = jnp.concatenate([q_even[:, cols], q_odd[:, cols]], axis=0)
            out.append(_dot_nt(q_st, k_sc[keys, cols]))
        return out

    def finish(j, scores):
        rows = slice(j * Q_BLOCK, (j + 1) * Q_BLOCK)
        keys = key_rows(j)
        mask = mask_of(j)
        ps, ms = [], []
        for hp in range(pairs):
            s = jnp.where(mask, scores[hp], NEG)
            for half in (s[:Q_BLOCK], s[Q_BLOCK:]):
                m = jnp.max(half, axis=-1, keepdims=True)
                ps.append(jnp.exp(half - m).astype(bf16))
                ms.append(m)
        o_parts = []
        m_tile = jnp.zeros((Q_BLOCK, LANES), f32)
        l_tile = jnp.ones((Q_BLOCK, LANES), f32)
        for hp in range(pairs):
            cols = slice(hp * LANES, (hp + 1) * LANES)
            acc_e = _dot(ps[2 * hp], ve_sc[keys, cols])
            acc_o = _dot(ps[2 * hp + 1], vo_sc[keys, cols])
            o_parts.append(jnp.where(low, acc_e, acc_o))
            l_tile = jnp.where(lane == HEAD_DIM + hp, acc_e, jnp.where(lane == hp, acc_o, l_tile))
            m_tile = jnp.where(lane == HEAD_DIM + hp, ms[2 * hp],
                               jnp.where(lane == hp, ms[2 * hp + 1], m_tile))
        o_ref[0, rows, :] = jnp.concatenate(o_parts, axis=1).astype(bf16)
        m_ref[0, rows, :] = m_tile
        l_ref[0, rows, :] = l_tile

    scores = qk(0)
    for j in range(n_blocks):
        nxt = qk(j + 1) if j + 1 < n_blocks else None
        if kvt_chunk is not None:
            kvt_chunk(j)
        finish(j, scores)
        scores = nxt


def _attn_prompt(x, scale, shift, g_pre, w_in_b, w_kvt, pattern):
    b = x.shape[0]
    steps = SEQ // ROWS_PER_STEP
    dil, window = DILATIONS[pattern], WINDOWS[pattern]
    q_blk, k_blk, v_blk = (Q_COL0 // QW + pattern, K_COL0 // QW + pattern, V_COL0 // QW + pattern)
    if window == SEQ:
        kvt_spec = pl.BlockSpec((1, 2 * QW, ROWS_PER_STEP), lambda i, c: (i, 0, c))
    else:
        kvt_spec = pl.BlockSpec((1, 2 * QW, window), lambda i, c: (i, 0, 0))
    return pl.pallas_call(
        functools.partial(_attn_kernel, dil=dil, window=window),
        grid=(b, steps),
        in_specs=[
            pl.BlockSpec((1, SEQ, D_MODEL), lambda i, c: (i, 0, 0)),
            pl.BlockSpec((1, 1, D_MODEL), lambda i, c: (i, 0, 0)),
            pl.BlockSpec((1, 1, D_MODEL), lambda i, c: (i, 0, 0)),
            pl.BlockSpec((1, D_MODEL), lambda i, c: (0, 0)),
            pl.BlockSpec((D_MODEL, QW), lambda i, c: (0, q_blk)),
            pl.BlockSpec((D_MODEL, QW), lambda i, c: (0, k_blk)),
            pl.BlockSpec((D_MODEL, QW), lambda i, c: (0, v_blk)),
            pl.BlockSpec((1, 2 * QW, D_MODEL), lambda i, c: (pattern, 0, 0)),
        ],
        out_specs=[
            pl.BlockSpec((1, ROWS_PER_STEP, QW), lambda i, c: (i, c, 0)),
            pl.BlockSpec((1, ROWS_PER_STEP, LANES), lambda i, c: (i, c, 0)),
            pl.BlockSpec((1, ROWS_PER_STEP, LANES), lambda i, c: (i, c, 0)),
            kvt_spec,
        ],
        out_shape=[
            jax.ShapeDtypeStruct((b, SEQ, QW), bf16),
            jax.ShapeDtypeStruct((b, SEQ, LANES), f32),
            jax.ShapeDtypeStruct((b, SEQ, LANES), f32),
            jax.ShapeDtypeStruct((b, 2 * QW, window), f32),
        ],
        scratch_shapes=[pltpu.VMEM((D_MODEL // LANES, SEQ, LANES), f32),
                        pltpu.VMEM((SEQ + Q_BLOCK, QW), bf16), pltpu.VMEM((SEQ + Q_BLOCK, QW), bf16),
                        pltpu.VMEM((SEQ + Q_BLOCK, QW), bf16)],
        name=f"attn_d{dil}",
        compiler_params=pltpu.CompilerParams(
            dimension_semantics=("arbitrary", "arbitrary"), vmem_limit_bytes=VMEM_LIMIT),
    )(x, scale, shift, g_pre, w_in_b, w_in_b, w_in_b, w_kvt)


def _qkv_s_kernel(x_ref, scale_ref, shift_ref, gpre_ref, w_ref, o_ref):
    h = _norm_mod(x_ref[...], gpre_ref[...], scale_ref[...], shift_ref[...]).astype(bf16)
    o_ref[...] = _dot(h, w_ref[...])


def _qkv_sample(x, scale, shift, g_pre, w_in_b):
    n = x.shape[0]
    cols = 9 * QW
    tile = 3 * QW
    first = Q_COL0 // tile
    return pl.pallas_call(
        _qkv_s_kernel,
        grid=(cols // tile,),
        in_specs=[
            pl.BlockSpec((n, D_MODEL), lambda j: (0, 0)),
            pl.BlockSpec((n, D_MODEL), lambda j: (0, 0)),
            pl.BlockSpec((n, D_MODEL), lambda j: (0, 0)),
            pl.BlockSpec((1, D_MODEL), lambda j: (0, 0)),
            pl.BlockSpec((D_MODEL, tile), lambda j: (0, first + j)),
        ],
        out_specs=pl.BlockSpec((n, tile), lambda j: (0, j)),
        out_shape=jax.ShapeDtypeStruct((n, cols), f32),
        name="qkv_s",
        compiler_params=pltpu.CompilerParams(
            dimension_semantics=("arbitrary",), vmem_limit_bytes=VMEM_LIMIT),
    )(x, scale, shift, g_pre, w_in_b)


T_NEW = 8


def _attn_s_kernel(qkv_ref, c0_ref, c1_ref, c2_ref, o_ref, new_sc):
    @pl.when(pl.program_id(0) == 0)
    def _():
        new_sc[...] = jnp.zeros_like(new_sc)

    qkv = qkv_ref[0]
    new_sc[0:T_NEW, :] = qkv[:, 3 * QW:]
    q_all = qkv[:, :3 * QW] * (HEAD_DIM ** -0.5)

    lane8 = lax.broadcasted_iota(jnp.int32, (2 * T_NEW, LANES), 1)
    low = lane8 < HEAD_DIM
    t_idx = lax.broadcasted_iota(jnp.int32, (2 * T_NEW, LANES), 0) % T_NEW
    caches = (c0_ref, c1_ref, c2_ref)
    pairs = range(N_HEADS // 2)
    pats = range(len(DILATIONS))

    def pair_cols(g, hp, base=0):
        return slice(base + g * QW + hp * LANES, base + g * QW + (hp + 1) * LANES)

    s_cache, s_new = {}, {}
    for hp in pairs:
        for g in pats:
            q2 = q_all[:, pair_cols(g, hp)]
            q_st = jnp.concatenate([jnp.where(low[:T_NEW], q2, 0.0), jnp.where(low[:T_NEW], 0.0, q2)],
                                   axis=0).astype(bf16)
            kt = caches[g][0, hp * LANES:(hp + 1) * LANES, :].astype(bf16)
            k_new = new_sc[:, pair_cols(g, hp)].astype(bf16)
            s_cache[hp, g] = _dot(q_st, kt)
            s_new[hp, g] = _dot_nt(q_st, k_new)

    probs = {}
    for hp in pairs:
        m = jnp.full((2 * T_NEW, 1), NEG, f32)
        for g, dil in enumerate(DILATIONS):
            rows = WINDOWS[g]
            rho = lax.broadcasted_iota(jnp.int32, (2 * T_NEW, rows), 1)
            tq = lax.broadcasted_iota(jnp.int32, (2 * T_NEW, rows), 0) % T_NEW
            ok_c = jnp.logical_and(rho >= tq, ((rho - tq) & (dil - 1)) == 0)
            ok_n = jnp.logical_and(lane8 <= t_idx, ((t_idx - lane8) & (dil - 1)) == 0)
            s_cache[hp, g] = jnp.where(ok_c, s_cache[hp, g], NEG)
            s_new[hp, g] = jnp.where(ok_n, s_new[hp, g], NEG)
            m = jnp.maximum(m, jnp.maximum(jnp.max(s_cache[hp, g], axis=-1, keepdims=True),
                                           jnp.max(s_new[hp, g], axis=-1, keepdims=True)))
        l = jnp.zeros((2 * T_NEW, 1), f32)
        for g in pats:
            p_c = jnp.exp(s_cache[hp, g] - m)
            p_n = jnp.exp(s_new[hp, g] - m)
            l = l + jnp.sum(p_c, axis=-1, keepdims=True) + jnp.sum(p_n, axis=-1, keepdims=True)
            probs[hp, g] = (p_c.astype(bf16), p_n.astype(bf16))
        probs[hp] = l

    o_parts = []
    for hp in pairs:
        acc = jnp.zeros((2 * T_NEW, LANES), f32)
        for g in pats:
            vt = caches[g][0, QW + hp * LANES: QW + (hp + 1) * LANES, :].astype(bf16)
            v_new = new_sc[:, pair_cols(g, hp, 3 * QW)].astype(bf16)
            p_c, p_n = probs[hp, g]
            acc = acc + _dot_nt(p_c, vt) + _dot(p_n, v_new)
        out = acc / probs[hp]
        o_parts.append(jnp.where(low[:T_NEW], out[:T_NEW], out[T_NEW:]))
    o_ref[0] = jnp.concatenate(o_parts, axis=1)


def _attn_sample(qkv, c0, c1, c2):
    n = qkv.shape[0]
    return pl.pallas_call(
        _attn_s_kernel,
        grid=(n,),
        in_specs=[
            pl.BlockSpec((1, T_NEW, 9 * QW), lambda i: (i, 0, 0)),
            pl.BlockSpec((1, 2 * QW, WINDOWS[0]), lambda i: (i, 0, 0)),
            pl.BlockSpec((1, 2 * QW, WINDOWS[1]), lambda i: (i, 0, 0)),
            pl.BlockSpec((1, 2 * QW, WINDOWS[2]), lambda i: (i, 0, 0)),
        ],
        out_specs=pl.BlockSpec((1, T_NEW, QW), lambda i: (i, 0, 0)),
        out_shape=jax.ShapeDtypeStruct((n, T_NEW, QW), f32),
        scratch_shapes=[pltpu.VMEM((LANES, 6 * QW), f32)],
        name="attn_s",
        compiler_params=pltpu.CompilerParams(
            dimension_semantics=("arbitrary",), vmem_limit_bytes=VMEM_LIMIT),
    )(qkv, c0, c1, c2)


TILE = 512


def _expand_heads(w):
    r = lax.broadcasted_iota(jnp.int32, (2 * LANES, QW), 0) % LANES
    head = lax.broadcasted_iota(jnp.int32, (2 * LANES, QW), 1) // HEAD_DIM
    expand = (r == head // 2 + HEAD_DIM * (1 - head % 2)).astype(bf16)
    hi = w.astype(bf16)
    lo = (w - hi.astype(f32)).astype(bf16)
    return _dot(jnp.concatenate([hi, lo], axis=1), expand)


def _rest_kernel(*refs, sample):
    if sample:
        (x_ref, scale_ref, shift_ref, gate_ref, gpre_ref, gpost_ref, lng_ref, lnb_ref,
         wa_ref, wzb_ref, wg_ref, wsp_ref, bsp_ref, wpa_ref, wpb_ref, wout_ref,
         attn_ref, y_ref, vn_ref) = refs
        x = x_ref[...]
        scale, shift, gate = scale_ref[...], shift_ref[...], gate_ref[...]
    else:
        (x_ref, scale_ref, shift_ref, gate_ref, gpre_ref, gpost_ref, lng_ref, lnb_ref,
         wa_ref, wzb_ref, wg_ref, wsp_ref, bsp_ref, wpa_ref, wpb_ref, wout_ref,
         o1_ref, m1_ref, l1_ref, o4_ref, m4_ref, l4_ref, o16_ref, m16_ref, l16_ref, y_ref,
         o4_sc, m4_sc, l4_sc, o16_sc, m16_sc, l16_sc) = refs
        x = x_ref[0]
        scale, shift, gate = scale_ref[0], shift_ref[0], gate_ref[0]

    rows = x.shape[0]
    h = _norm_mod(x, gpre_ref[...], scale, shift).astype(bf16)

    pa = _dot(h, wa_ref[...])
    u_a, v_a, z_a = pa[:, :D_MODEL], pa[:, D_MODEL:2 * D_MODEL], pa[:, 2 * D_MODEL:]
    mu = jnp.mean(v_a, axis=-1, keepdims=True)
    cen = v_a - mu
    var = jnp.mean(cen * cen, axis=-1, keepdims=True)
    v_n = cen * lax.rsqrt(var + EPS) * lng_ref[...] + lnb_ref[...]
    v_nb = v_n.astype(bf16)
    if sample:
        vn_ref[...] = v_n
        pick = (lax.broadcasted_iota(jnp.int32, (rows, CHUNK), 1)
                == lax.broadcasted_iota(jnp.int32, (rows, CHUNK), 0) % T_NEW).astype(bf16)
        same_seq = (lax.broadcasted_iota(jnp.int32, (rows, rows), 0) // T_NEW
                    == lax.broadcasted_iota(jnp.int32, (rows, rows), 1) // T_NEW)
        cols = []
        for g in range(A_GROUPS):
            tiled = _dot_nt(_dot(pick, wsp_ref[g]).astype(bf16), pick)
            w_blk = jnp.where(same_seq, tiled, 0.0).astype(bf16)
            cols.append(_dot(w_blk, v_nb[:, g * LANES:(g + 1) * LANES]) + bsp_ref[:, g:g + 1])
        zs = jnp.concatenate(cols, axis=1)
    else:
        n_ck = rows // CHUNK
        per_group = []
        for g in range(A_GROUPS):
            rhs = jnp.concatenate(
                [v_nb[ck * CHUNK:(ck + 1) * CHUNK, g * LANES:(g + 1) * LANES] for ck in range(n_ck)], axis=1)
            per_group.append(_dot(wsp_ref[g], rhs) + bsp_ref[:, g:g + 1])
        zs = jnp.concatenate(
            [jnp.concatenate([pg[:, ck * LANES:(ck + 1) * LANES] for pg in per_group], axis=1)
             for ck in range(n_ck)], axis=0)
    y_a = u_a * zs * _silu(z_a)

    if sample:
        attn = attn_ref[...]
    else:
        for osc, msc, lsc, oref, mref, lref, dil in (
                (o4_sc, m4_sc, l4_sc, o4_ref, m4_ref, l4_ref, DILATIONS[1]),
                (o16_sc, m16_sc, l16_sc, o16_ref, m16_ref, l16_ref, DILATIONS[2])):
            n = rows // dil
            for r in range(dil):
                o_r = oref[0, r].astype(f32)
                for k in range(QW // LANES):
                    osc[k, pl.ds(r, n, stride=dil), :] = o_r[:, k * LANES:(k + 1) * LANES]
                msc[pl.ds(r, n, stride=dil), :] = mref[0, r]
                lsc[pl.ds(r, n, stride=dil), :] = lref[0, r]
        ms = (m1_ref[0], m4_sc[...], m16_sc[...])
        ls = (l1_ref[0], l4_sc[...], l16_sc[...])
        unchunk = lambda sc: jnp.concatenate([sc[k] for k in range(QW // LANES)], axis=1)
        outs = (o1_ref[0].astype(f32), unchunk(o4_sc), unchunk(o16_sc))
        m_all = jnp.maximum(jnp.maximum(ms[0], ms[1]), ms[2])
        ws = [jnp.exp(m - m_all) for m in ms]
        den = ws[0] * ls[0] + ws[1] * ls[1] + ws[2] * ls[2]
        attn = sum(_expand_heads(w / den) * o for w, o in zip(ws, outs))

    z_b = _dot(h, wzb_ref[...])
    y_b = (attn * _silu(z_b)).astype(bf16)

    gl = _dot(h, wg_ref[...])
    p_a = _dot(y_a.astype(bf16), wpa_ref[...])
    p_b = _dot(y_b, wpb_ref[...])
    merged = jax.nn.sigmoid(gl[:, :D_MODEL]) * p_a + jax.nn.sigmoid(gl[:, D_MODEL:]) * p_b
    out = _dot(merged.astype(bf16), wout_ref[...])
    normed = out * lax.rsqrt(jnp.mean(out * out, axis=-1, keepdims=True) + EPS) * gpost_ref[...]
    y = x + gate * normed
    if sample:
        y_ref[...] = y
    else:
        y_ref[0] = y


def _full(shape):
    nd = len(shape)
    return pl.BlockSpec(shape, lambda *_: (0,) * nd)


def _weight_specs(weights):
    in_proj = [pl.BlockSpec((D_MODEL, 3 * D_MODEL), lambda *_: (0, 0)),
               pl.BlockSpec((D_MODEL, QW), lambda *_: (0, ZB_COL0 // QW)),
               pl.BlockSpec((D_MODEL, 2 * D_MODEL), lambda *_: (0, GATE_COL0 // (2 * D_MODEL)))]
    return in_proj + [_full(w.shape) for w in weights[3:]]


def _rest_prompt(x, scale, shift, gate, vecs, weights, attn_parts):
    b = x.shape[0]
    tiles = SEQ // TILE
    per_b = lambda i, c: (i, 0, 0)
    tile3 = lambda i, c: (i, c, 0)
    tile4 = lambda i, c: (i, 0, c, 0)
    part_args, part_specs, scratch = [], [], []
    for (o, m, l), dil in zip(attn_parts, DILATIONS):
        for a, width in ((o, QW), (m, LANES), (l, LANES)):
            if dil == 1:
                part_args.append(a)
                part_specs.append(pl.BlockSpec((1, TILE, width), tile3))
            else:
                part_args.append(a.reshape(b, dil, SEQ // dil, width))
                part_specs.append(pl.BlockSpec((1, dil, TILE // dil, width), tile4))
        if dil != 1:
            scratch += [pltpu.VMEM((QW // LANES, TILE, LANES), f32),
                        pltpu.VMEM((TILE, LANES), f32), pltpu.VMEM((TILE, LANES), f32)]
    in_specs = (
        [pl.BlockSpec((1, TILE, D_MODEL), tile3),
         pl.BlockSpec((1, 1, D_MODEL), per_b), pl.BlockSpec((1, 1, D_MODEL), per_b),
         pl.BlockSpec((1, 1, D_MODEL), per_b)]
        + [_full(v.shape) for v in vecs] + _weight_specs(weights) + part_specs)
    return pl.pallas_call(
        functools.partial(_rest_kernel, sample=False),
        grid=(b, tiles),
        in_specs=in_specs,
        out_specs=pl.BlockSpec((1, TILE, D_MODEL), tile3),
        out_shape=jax.ShapeDtypeStruct((b, SEQ, D_MODEL), f32),
        scratch_shapes=scratch,
        name="rest_p",
        compiler_params=pltpu.CompilerParams(
            dimension_semantics=("arbitrary", "arbitrary"), vmem_limit_bytes=VMEM_LIMIT),
    )(x, scale, shift, gate, *vecs, *weights, *part_args)


def _rest_sample(x, scale, shift, gate, vecs, weights, attn):
    n = x.shape[0]
    args = (x, scale, shift, gate, *vecs, *weights, attn)
    in_specs = ([_full(a.shape) for a in (x, scale, shift, gate, *vecs)] + _weight_specs(weights)
                + [_full(attn.shape)])
    return pl.pallas_call(
        functools.partial(_rest_kernel, sample=True),
        in_specs=in_specs,
        out_specs=[_full((n, D_MODEL)), _full((n, D_MODEL))],
        out_shape=[jax.ShapeDtypeStruct((n, D_MODEL), f32), jax.ShapeDtypeStruct((n, D_MODEL), f32)],
        grid=(1,),
        name="rest_s",
        compiler_params=pltpu.CompilerParams(
            dimension_semantics=("arbitrary",), vmem_limit_bytes=VMEM_LIMIT),
    )(*args)


def _kv_rows(kv, batch, rows):
    return kv.reshape(1, batch, rows, 2, N_HEADS, HEAD_DIM)


def kernel(x_prompt, x_sample, cache_kv_w128, cache_kv_w512, cache_kv_w2048, c_prompt, c_sample, w_cond, b_cond, g_pre, w_in, ln_v_g, ln_v_b, w_spatial, b_spatial, w_proj_a, w_proj_b, w_out, g_post):
    assert w_in.shape[0] == 1, "single layer"
    bp, seq, _ = x_prompt.shape
    bs, t_new, _ = x_sample.shape
    assert seq == SEQ and t_new == T_NEW

    w_in_b = w_in[0].astype(bf16)
    w_kvt = jnp.transpose(w_in[0][:, K_COL0:ZB_COL0].reshape(D_MODEL, 2, 3, QW), (2, 1, 3, 0))
    w_kvt = w_kvt.reshape(3, 2 * QW, D_MODEL).astype(bf16)
    causal = jnp.tril(jnp.ones((CHUNK, CHUNK), bool))
    w_sp = jnp.where(causal[None], w_spatial[0], 0.0)
    w_sp_p = w_sp.astype(bf16)
    n_s = bs * T_NEW
    b_sp_p = b_spatial[0].T
    b_sp_s = jnp.tile(b_spatial[0][:, :T_NEW].T, (bs, 1))
    weights_tail = (w_proj_a[0].astype(bf16), w_proj_b[0].astype(bf16), w_out[0].astype(bf16))
    vecs = (g_pre, g_post, ln_v_g, ln_v_b)

    mod = _cond(jnp.concatenate([c_prompt, c_sample], axis=0), w_cond[0], b_cond)
    shift, scale, gate = mod[:, :D_MODEL], mod[:, D_MODEL:2 * D_MODEL], mod[:, 2 * D_MODEL:]
    mp = lambda a: a[:bp].reshape(bp, 1, D_MODEL)
    ms = lambda a: jnp.repeat(a[bp:], T_NEW, axis=0)

    res = [_attn_prompt(x_prompt, mp(scale), mp(shift), g_pre, w_in_b, w_kvt, g) for g in range(3)]
    in_proj = (w_in_b, w_in_b, w_in_b)
    y_p = _rest_prompt(x_prompt, mp(scale), mp(shift), mp(gate), vecs,
                       in_proj + (w_sp_p, b_sp_p) + weights_tail, [r[:3] for r in res])
    kv_p = [jnp.transpose(r[3].reshape(bp, 2, N_HEADS, HEAD_DIM, win), (0, 4, 1, 2, 3))[None]
            for r, win in zip(res, WINDOWS)]

    xs = x_sample.reshape(n_s, D_MODEL)
    qkv_s = _qkv_sample(xs, ms(scale), ms(shift), g_pre, w_in_b)
    caches_t = []
    for cache, win in zip((cache_kv_w128, cache_kv_w512, cache_kv_w2048), WINDOWS):
        caches_t.append(jnp.transpose(cache[0], (0, 2, 3, 4, 1)).reshape(bs, 2 * QW, win))
    attn_s = _attn_sample(qkv_s.reshape(bs, T_NEW, 9 * QW), *caches_t).reshape(n_s, QW)
    y_s, v_n_s = _rest_sample(xs, ms(scale), ms(shift), ms(gate), vecs,
                              in_proj + (w_sp_p, b_sp_s) + weights_tail, attn_s)
    kv_s = []
    for g in range(3):
        kq = qkv_s[:, 3 * QW + g * QW:3 * QW + (g + 1) * QW]
        vq = qkv_s[:, 6 * QW + g * QW:6 * QW + (g + 1) * QW]
        kv_s.append(_kv_rows(jnp.concatenate([kq, vq], axis=1), bs, T_NEW))

    return (y_p, y_s.reshape(bs, T_NEW, D_MODEL), kv_p[0], kv_p[1], kv_p[2],
            kv_s[0], kv_s[1], kv_s[2], v_n_s.reshape(1, bs, T_NEW, D_MODEL))
```

```python
import functools

import jax
import jax.numpy as jnp
from jax import lax
from jax.experimental import pallas as pl
from jax.experimental.pallas import tpu as pltpu

D_MODEL = 1024
SEQ = 2048
HEAD_DIM = 64
N_HEADS = 8
QW = N_HEADS * HEAD_DIM
WINDOWS = (128, 512, 2048)
DILATIONS = (1, 4, 16)
N_BACK = 128
CHUNK = 128
A_GROUPS = 8
EPS = 1e-6
NEG = -1e30
LANES = 128
VMEM_LIMIT = 56 * 1024 * 1024
Q_COL0 = 3 * D_MODEL
K_COL0 = Q_COL0 + 3 * QW
V_COL0 = K_COL0 + 3 * QW
ZB_COL0 = V_COL0 + 3 * QW
GATE_COL0 = ZB_COL0 + QW

f32 = jnp.float32
bf16 = jnp.bfloat16


def _silu(x):
    return x * jax.nn.sigmoid(x)


def _norm_mod(x, g_pre, scale, shift):
    y = x * lax.rsqrt(jnp.mean(x * x, axis=-1, keepdims=True) + EPS) * g_pre
    return y * (1.0 + scale) + shift


def _dot(a, b):
    return jnp.dot(a, b, preferred_element_type=f32)


def _dot_nt(a, b):
    return lax.dot_general(a, b, (((1,), (1,)), ((), ())), preferred_element_type=f32)


def _cond_kernel(c_ref, w_ref, b_ref, o_ref):
    o_ref[...] = _dot(_silu(c_ref[...]).astype(bf16), w_ref[...].astype(bf16)) + b_ref[...]


def _cond(c_all, w_cond, b_cond):
    n = c_all.shape[0]
    return pl.pallas_call(
        _cond_kernel,
        out_shape=jax.ShapeDtypeStruct((n, 3 * D_MODEL), f32),
        name="cond",
        compiler_params=pltpu.CompilerParams(vmem_limit_bytes=VMEM_LIMIT),
    )(c_all, w_cond, b_cond)


ROWS_PER_STEP = 512
Q_BLOCK = 128
NORM_ROWS = 256


def _attn_kernel(x_ref, scale_ref, shift_ref, gpre_ref, wq_ref, wk_ref, wv_ref, wkvt_ref,
                 o_ref, m_ref, l_ref, kvt_ref, h_sc, k_sc, ve_sc, vo_sc, *, dil, window):
    c = pl.program_id(1)
    seg = SEQ // dil
    n_lc = D_MODEL // LANES

    @pl.when(c == 0)
    def _():
        zeros = jnp.zeros((Q_BLOCK, QW), bf16)
        k_sc[0:Q_BLOCK, :] = zeros
        ve_sc[0:Q_BLOCK, :] = zeros
        vo_sc[0:Q_BLOCK, :] = zeros
        for rb in range(SEQ // NORM_ROWS):
            rs = slice(rb * NORM_ROWS, (rb + 1) * NORM_ROWS)
            hb = _norm_mod(x_ref[0, rs, :], gpre_ref[...], scale_ref[0], shift_ref[0])
            for k in range(n_lc):
                h_sc[k, rs, :] = hb[:, k * LANES:(k + 1) * LANES]

    def rows_of(start, n):
        idx = pl.ds(pl.multiple_of(start, n), n) if dil == 1 else pl.ds(start, n, stride=dil)
        return jnp.concatenate([h_sc[k, idx, :] for k in range(n_lc)], axis=1)

    if seg >= ROWS_PER_STEP:
        per = seg // ROWS_PER_STEP
        h = rows_of((c // per) + (c % per) * ROWS_PER_STEP * dil, ROWS_PER_STEP)
    else:
        per = ROWS_PER_STEP // seg
        h = jnp.concatenate([rows_of(per * c + i, seg) for i in range(per)], axis=0)
    h = h.astype(bf16)
    base = pl.multiple_of(c * ROWS_PER_STEP, ROWS_PER_STEP)

    def tokens_t(start, n):
        return jnp.concatenate([h_sc[k, pl.ds(start, n), :] for k in range(n_lc)], axis=1).astype(bf16)

    if window == SEQ:
        kvt_ref[0] = _dot_nt(wkvt_ref[0], tokens_t(base, ROWS_PER_STEP))
    else:
        @pl.when(c == pl.num_programs(1) - 1)
        def _():
            kvt_ref[0] = _dot_nt(wkvt_ref[0], tokens_t(SEQ - window, window))

    low_w = (lax.broadcasted_iota(jnp.int32, (ROWS_PER_STEP, QW), 1) % LANES) < HEAD_DIM
    v = _dot(h, wv_ref[...])
    new_rows = pl.ds(Q_BLOCK + base, ROWS_PER_STEP)
    k_sc[new_rows, :] = _dot(h, wk_ref[...]).astype(bf16)
    ve_sc[new_rows, :] = jnp.where(low_w, v, 1.0).astype(bf16)
    vo_sc[new_rows, :] = jnp.where(low_w, 1.0, v).astype(bf16)
    q = _dot(h, wq_ref[...]) * (HEAD_DIM ** -0.5)
    q_even = jnp.where(low_w, q, 0.0).astype(bf16)
    q_odd = jnp.where(low_w, 0.0, q).astype(bf16)

    n_blocks = ROWS_PER_STEP // Q_BLOCK
    pairs = N_HEADS // 2
    qi = lax.broadcasted_iota(jnp.int32, (2 * Q_BLOCK, 2 * Q_BLOCK), 0) % Q_BLOCK
    kj = lax.broadcasted_iota(jnp.int32, (2 * Q_BLOCK, 2 * Q_BLOCK), 1)
    lane = lax.broadcasted_iota(jnp.int32, (Q_BLOCK, LANES), 1)
    low = lane < HEAD_DIM

    def has_prev(j):
        return seg > ROWS_PER_STEP or (j * Q_BLOCK) % seg != 0

    def key_rows(j):
        r0 = pl.multiple_of(base + j * Q_BLOCK, Q_BLOCK)
        return pl.ds(r0, 2 * Q_BLOCK) if has_prev(j) else pl.ds(r0 + Q_BLOCK, Q_BLOCK)

    def mask_of(j):
        if not has_prev(j):
            own = (2 * Q_BLOCK, Q_BLOCK)
            return (lax.broadcasted_iota(jnp.int32, own, 1)
                    <= lax.broadcasted_iota(jnp.int32, own, 0) % Q_BLOCK)
        off = jnp.where((base % seg) != 0, 0, Q_BLOCK) if (seg > ROWS_PER_STEP and j == 0) else 0
        in_prev = jnp.logical_and(kj < Q_BLOCK, kj >= qi + off)
        in_cur = jnp.logical_and(kj >= Q_BLOCK, kj - Q_BLOCK <= qi)
        return jnp.logical_or(in_prev, in_cur)

    def qk(j):
        rows = slice(j * Q_BLOCK, (j + 1) * Q_BLOCK)
        keys = key_rows(j)
        out = []
        for hp in range(pairs):
            cols = slice(hp * LANES, (hp + 1) * LANES)
            q_st = jnp.concatenate([q_even[rows, cols], q_odd[rows, cols]], axis=0)
            out.append(_dot_nt(q_st, k_sc[keys, cols]))
        return out

    def finish(j, scores):
        rows = slice(j * Q_BLOCK, (j + 1) * Q_BLOCK)
        keys = key_rows(j)
        mask = mask_of(j)
        ps, ms = [], []
        for hp in range(pairs):
            s = jnp.where(mask, scores[hp], NEG)
            for half in (s[:Q_BLOCK], s[Q_BLOCK:]):
                m = jnp.max(half, axis=-1, keepdims=True)
                ps.append(jnp.exp(half - m).astype(bf16))
                ms.append(m)
        o_parts = []
        m_tile = jnp.zeros((Q_BLOCK, LANES), f32)
        l_tile = jnp.ones((Q_BLOCK, LANES), f32)
        for hp in range(pairs):
            cols = slice(hp * LANES, (hp + 1) * LANES)
            acc_e = _dot(ps[2 * hp], ve_sc[keys, cols])
            acc_o = _dot(ps[2 * hp + 1], vo_sc[keys, cols])
            o_parts.append(jnp.where(low, acc_e, acc_o))
            l_tile = jnp.where(lane == HEAD_DIM + hp, acc_e, jnp.where(lane == hp, acc_o, l_tile))
            m_tile = jnp.where(lane == HEAD_DIM + hp, ms[2 * hp],
                               jnp.where(lane == hp, ms[2 * hp + 1], m_tile))
        o_ref[0, rows, :] = jnp.concatenate(o_parts, axis=1).astype(bf16)
        m_ref[0, rows, :] = m_tile
        l_ref[0, rows, :] = l_tile

    scores = qk(0)
    for j in range(n_blocks):
        nxt = qk(j + 1) if j + 1 < n_blocks else None
        finish(j, scores)
        scores = nxt


def _attn_prompt(x, scale, shift, g_pre, w_in_b, w_kvt, pattern):
    b = x.shape[0]
    steps = SEQ // ROWS_PER_STEP
    dil, window = DILATIONS[pattern], WINDOWS[pattern]
    q_blk, k_blk, v_blk = (Q_COL0 // QW + pattern, K_COL0 // QW + pattern, V_COL0 // QW + pattern)
    if window == SEQ:
        kvt_spec = pl.BlockSpec((1, 2 * QW, ROWS_PER_STEP), lambda i, c: (i, 0, c))
    else:
        kvt_spec = pl.BlockSpec((1, 2 * QW, window), lambda i, c: (i, 0, 0))
    return pl.pallas_call(
        functools.partial(_attn_kernel, dil=dil, window=window),
        grid=(b, steps),
        in_specs=[
            pl.BlockSpec((1, SEQ, D_MODEL), lambda i, c: (i, 0, 0)),
            pl.BlockSpec((1, 1, D_MODEL), lambda i, c: (i, 0, 0)),
            pl.BlockSpec((1, 1, D_MODEL), lambda i, c: (i, 0, 0)),
            pl.BlockSpec((1, D_MODEL), lambda i, c: (0, 0)),
            pl.BlockSpec((D_MODEL, QW), lambda i, c: (0, q_blk)),
            pl.BlockSpec((D_MODEL, QW), lambda i, c: (0, k_blk)),
            pl.BlockSpec((D_MODEL, QW), lambda i, c: (0, v_blk)),
            pl.BlockSpec((1, 2 * QW, D_MODEL), lambda i, c: (pattern, 0, 0)),
        ],
        out_specs=[
            pl.BlockSpec((1, ROWS_PER_STEP, QW), lambda i, c: (i, c, 0)),
            pl.BlockSpec((1, ROWS_PER_STEP, LANES), lambda i, c: (i, c, 0)),
            pl.BlockSpec((1, ROWS_PER_STEP, LANES), lambda i, c: (i, c, 0)),
            kvt_spec,
        ],
        out_shape=[
            jax.ShapeDtypeStruct((b, SEQ, QW), bf16),
            jax.ShapeDtypeStruct((b, SEQ, LANES), f32),
            jax.ShapeDtypeStruct((b, SEQ, LANES), f32),
            jax.ShapeDtypeStruct((b, 2 * QW, window), f32),
        ],
        scratch_shapes=[pltpu.VMEM((D_MODEL // LANES, SEQ, LANES), f32),
                        pltpu.VMEM((SEQ + Q_BLOCK, QW), bf16), pltpu.VMEM((SEQ + Q_BLOCK, QW), bf16),
                        pltpu.VMEM((SEQ + Q_BLOCK, QW), bf16)],
        name=f"attn_d{dil}",
        compiler_params=pltpu.CompilerParams(
            dimension_semantics=("arbitrary", "arbitrary"), vmem_limit_bytes=VMEM_LIMIT),
    )(x, scale, shift, g_pre, w_in_b, w_in_b, w_in_b, w_kvt)


def _qkv_s_kernel(x_ref, scale_ref, shift_ref, gpre_ref, w_ref, o_ref):
    h = _norm_mod(x_ref[...], gpre_ref[...], scale_ref[...], shift_ref[...]).astype(bf16)
    o_ref[...] = _dot(h, w_ref[...])


def _qkv_sample(x, scale, shift, g_pre, w_in_b):
    n = x.shape[0]
    cols = 9 * QW
    tile = 3 * QW
    first = Q_COL0 // tile
    return pl.pallas_call(
        _qkv_s_kernel,
        grid=(cols // tile,),
        in_specs=[
            pl.BlockSpec((n, D_MODEL), lambda j: (0, 0)),
            pl.BlockSpec((n, D_MODEL), lambda j: (0, 0)),
            pl.BlockSpec((n, D_MODEL), lambda j: (0, 0)),
            pl.BlockSpec((1, D_MODEL), lambda j: (0, 0)),
            pl.BlockSpec((D_MODEL, tile), lambda j: (0, first + j)),
        ],
        out_specs=pl.BlockSpec((n, tile), lambda j: (0, j)),
        out_shape=jax.ShapeDtypeStruct((n, cols), f32),
        name="qkv_s",
        compiler_params=pltpu.CompilerParams(
            dimension_semantics=("arbitrary",), vmem_limit_bytes=VMEM_LIMIT),
    )(x, scale, shift, g_pre, w_in_b)


T_NEW = 8


def _attn_s_kernel(qkv_ref, c0_ref, c1_ref, c2_ref, o_ref, new_sc):
    @pl.when(pl.program_id(0) == 0)
    def _():
        new_sc[...] = jnp.zeros_like(new_sc)

    qkv = qkv_ref[0]
    new_sc[0:T_NEW, :] = qkv[:, 3 * QW:]
    q_all = qkv[:, :3 * QW] * (HEAD_DIM ** -0.5)

    lane8 = lax.broadcasted_iota(jnp.int32, (2 * T_NEW, LANES), 1)
    low = lane8 < HEAD_DIM
    t_idx = lax.broadcasted_iota(jnp.int32, (2 * T_NEW, LANES), 0) % T_NEW
    caches = (c0_ref, c1_ref, c2_ref)
    pairs = range(N_HEADS // 2)
    pats = range(len(DILATIONS))

    def pair_cols(g, hp, base=0):
        return slice(base + g * QW + hp * LANES, base + g * QW + (hp + 1) * LANES)

    s_cache, s_new = {}, {}
    for hp in pairs:
        for g in pats:
            q2 = q_all[:, pair_cols(g, hp)]
            q_st = jnp.concatenate([jnp.where(low[:T_NEW], q2, 0.0), jnp.where(low[:T_NEW], 0.0, q2)],
                                   axis=0).astype(bf16)
            kt = caches[g][0, hp * LANES:(hp + 1) * LANES, :].astype(bf16)
            k_new = new_sc[:, pair_cols(g, hp)].astype(bf16)
            s_cache[hp, g] = _dot(q_st, kt)
            s_new[hp, g] = _dot_nt(q_st, k_new)

    probs = {}
    for hp in pairs:
        m = jnp.full((2 * T_NEW, 1), NEG, f32)
        for g, dil in enumerate(DILATIONS):
            rows = WINDOWS[g]
            rho = lax.broadcasted_iota(jnp.int32, (2 * T_NEW, rows), 1)
            tq = lax.broadcasted_iota(jnp.int32, (2 * T_NEW, rows), 0) % T_NEW
            ok_c = jnp.logical_and(rho >= tq, ((rho - tq) & (dil - 1)) == 0)
            ok_n = jnp.logical_and(lane8 <= t_idx, ((t_idx - lane8) & (dil - 1)) == 0)
            s_cache[hp, g] = jnp.where(ok_c, s_cache[hp, g], NEG)
            s_new[hp, g] = jnp.where(ok_n, s_new[hp, g], NEG)
            m = jnp.maximum(m, jnp.maximum(jnp.max(s_cache[hp, g], axis=-1, keepdims=True),
                                           jnp.max(s_new[hp, g], axis=-1, keepdims=True)))
        l = jnp.zeros((2 * T_NEW, 1), f32)
        for g in pats:
            p_c = jnp.exp(s_cache[hp, g] - m)
            p_n = jnp.exp(s_new[hp, g] - m)
            l = l + jnp.sum(p_c, axis=-1, keepdims=True) + jnp.sum(p_n, axis=-1, keepdims=True)
            probs[hp, g] = (p_c.astype(bf16), p_n.astype(bf16))
        probs[hp] = l

    o_parts = []
    for hp in pairs:
        acc = jnp.zeros((2 * T_NEW, LANES), f32)
        for g in pats:
            vt = caches[g][0, QW + hp * LANES: QW + (hp + 1) * LANES, :].astype(bf16)
            v_new = new_sc[:, pair_cols(g, hp, 3 * QW)].astype(bf16)
            p_c, p_n = probs[hp, g]
            acc = acc + _dot_nt(p_c, vt) + _dot(p_n, v_new)
        out = acc / probs[hp]
        o_parts.append(jnp.where(low[:T_NEW], out[:T_NEW], out[T_NEW:]))
    o_ref[0] = jnp.concatenate(o_parts, axis=1)


def _attn_sample(qkv, c0, c1, c2):
    n = qkv.shape[0]
    return pl.pallas_call(
        _attn_s_kernel,
        grid=(n,),
        in_specs=[
            pl.BlockSpec((1, T_NEW, 9 * QW), lambda i: (i, 0, 0)),
            pl.BlockSpec((1, 2 * QW, WINDOWS[0]), lambda i: (i, 0, 0)),
            pl.BlockSpec((1, 2 * QW, WINDOWS[1]), lambda i: (i, 0, 0)),
            pl.BlockSpec((1, 2 * QW, WINDOWS[2]), lambda i: (i, 0, 0)),
        ],
        out_specs=pl.BlockSpec((1, T_NEW, QW), lambda i: (i, 0, 0)),
        out_shape=jax.ShapeDtypeStruct((n, T_NEW, QW), f32),
        scratch_shapes=[pltpu.VMEM((LANES, 6 * QW), f32)],
        name="attn_s",
        compiler_params=pltpu.CompilerParams(
            dimension_semantics=("arbitrary",), vmem_limit_bytes=VMEM_LIMIT),
    )(qkv, c0, c1, c2)


TILE = 512


def _expand_heads(w):
    r = lax.broadcasted_iota(jnp.int32, (2 * LANES, QW), 0) % LANES
    head = lax.broadcasted_iota(jnp.int32, (2 * LANES, QW), 1) // HEAD_DIM
    expand = (r == head // 2 + HEAD_DIM * (1 - head % 2)).astype(bf16)
    hi = w.astype(bf16)
    lo = (w - hi.astype(f32)).astype(bf16)
    return _dot(jnp.concatenate([hi, lo], axis=1), expand)


def _rest_kernel(*refs, sample):
    if sample:
        (x_ref, scale_ref, shift_ref, gate_ref, gpre_ref, gpost_ref, lng_ref, lnb_ref,
         wa_ref, wzb_ref, wg_ref, wsp_ref, bsp_ref, wpa_ref, wpb_ref, wout_ref,
         attn_ref, y_ref, vn_ref) = refs
        x = x_ref[...]
        scale, shift, gate = scale_ref[...], shift_ref[...], gate_ref[...]
    else:
        (x_ref, scale_ref, shift_ref, gate_ref, gpre_ref, gpost_ref, lng_ref, lnb_ref,
         wa_ref, wzb_ref, wg_ref, wsp_ref, bsp_ref, wpa_ref, wpb_ref, wout_ref,
         o1_ref, m1_ref, l1_ref, o4_ref, m4_ref, l4_ref, o16_ref, m16_ref, l16_ref, y_ref,
         o4_sc, m4_sc, l4_sc, o16_sc, m16_sc, l16_sc) = refs
        x = x_ref[0]
        scale, shift, gate = scale_ref[0], shift_ref[0], gate_ref[0]

    rows = x.shape[0]
    h = _norm_mod(x, gpre_ref[...], scale, shift).astype(bf16)

    pa = _dot(h, wa_ref[...])
    u_a, v_a, z_a = pa[:, :D_MODEL], pa[:, D_MODEL:2 * D_MODEL], pa[:, 2 * D_MODEL:]
    mu = jnp.mean(v_a, axis=-1, keepdims=True)
    cen = v_a - mu
    var = jnp.mean(cen * cen, axis=-1, keepdims=True)
    v_n = cen * lax.rsqrt(var + EPS) * lng_ref[...] + lnb_ref[...]
    v_nb = v_n.astype(bf16)
    if sample:
        vn_ref[...] = v_n
        pick = (lax.broadcasted_iota(jnp.int32, (rows, CHUNK), 1)
                == lax.broadcasted_iota(jnp.int32, (rows, CHUNK), 0) % T_NEW).astype(bf16)
        same_seq = (lax.broadcasted_iota(jnp.int32, (rows, rows), 0) // T_NEW
                    == lax.broadcasted_iota(jnp.int32, (rows, rows), 1) // T_NEW)
        cols = []
        for g in range(A_GROUPS):
            tiled = _dot_nt(_dot(pick, wsp_ref[g]).astype(bf16), pick)
            w_blk = jnp.where(same_seq, tiled, 0.0).astype(bf16)
            cols.append(_dot(w_blk, v_nb[:, g * LANES:(g + 1) * LANES]) + bsp_ref[:, g:g + 1])
        zs = jnp.concatenate(cols, axis=1)
    else:
        n_ck = rows // CHUNK
        per_group = []
        for g in range(A_GROUPS):
            rhs = jnp.concatenate(
                [v_nb[ck * CHUNK:(ck + 1) * CHUNK, g * LANES:(g + 1) * LANES] for ck in range(n_ck)], axis=1)
            per_group.append(_dot(wsp_ref[g], rhs) + bsp_ref[:, g:g + 1])
        zs = jnp.concatenate(
            [jnp.concatenate([pg[:, ck * LANES:(ck + 1) * LANES] for pg in per_group], axis=1)
             for ck in range(n_ck)], axis=0)
    y_a = u_a * zs * _silu(z_a)

    if sample:
        attn = attn_ref[...]
    else:
        for osc, msc, lsc, oref, mref, lref, dil in (
                (o4_sc, m4_sc, l4_sc, o4_ref, m4_ref, l4_ref, DILATIONS[1]),
                (o16_sc, m16_sc, l16_sc, o16_ref, m16_ref, l16_ref, DILATIONS[2])):
            n = rows // dil
            for r in range(dil):
                o_r = oref[0, r].astype(f32)
                for k in range(QW // LANES):
                    osc[k, pl.ds(r, n, stride=dil), :] = o_r[:, k * LANES:(k + 1) * LANES]
                msc[pl.ds(r, n, stride=dil), :] = mref[0, r]
                lsc[pl.ds(r, n, stride=dil), :] = lref[0, r]
        ms = (m1_ref[0], m4_sc[...], m16_sc[...])
        ls = (l1_ref[0], l4_sc[...], l16_sc[...])
        unchunk = lambda sc: jnp.concatenate([sc[k] for k in range(QW // LANES)], axis=1)
        outs = (o1_ref[0].astype(f32), unchunk(o4_sc), unchunk(o16_sc))
        m_all = jnp.maximum(jnp.maximum(ms[0], ms[1]), ms[2])
        ws = [jnp.exp(m - m_all) for m in ms]
        den = ws[0] * ls[0] + ws[1] * ls[1] + ws[2] * ls[2]
        attn = sum(_expand_heads(w / den) * o for w, o in zip(ws, outs))

    z_b = _dot(h, wzb_ref[...])
    y_b = (attn * _silu(z_b)).astype(bf16)

    gl = _dot(h, wg_ref[...])
    p_a = _dot(y_a.astype(bf16), wpa_ref[...])
    p_b = _dot(y_b, wpb_ref[...])
    merged = jax.nn.sigmoid(gl[:, :D_MODEL]) * p_a + jax.nn.sigmoid(gl[:, D_MODEL:]) * p_b
    out = _dot(merged.astype(bf16), wout_ref[...])
    normed = out * lax.rsqrt(jnp.mean(out * out, axis=-1, keepdims=True) + EPS) * gpost_ref[...]
    y = x + gate * normed
    if sample:
        y_ref[...] = y
    else:
        y_ref[0] = y


def _full(shape):
    nd = len(shape)
    return pl.BlockSpec(shape, lambda *_: (0,) * nd)


def _weight_specs(weights):
    in_proj = [pl.BlockSpec((D_MODEL, 3 * D_MODEL), lambda *_: (0, 0)),
               pl.BlockSpec((D_MODEL, QW), lambda *_: (0, ZB_COL0 // QW)),
               pl.BlockSpec((D_MODEL, 2 * D_MODEL), lambda *_: (0, GATE_COL0 // (2 * D_MODEL)))]
    return in_proj + [_full(w.shape) for w in weights[3:]]


def _rest_prompt(x, scale, shift, gate, vecs, weights, attn_parts):
    b = x.shape[0]
    tiles = SEQ // TILE
    per_b = lambda i, c: (i, 0, 0)
    tile3 = lambda i, c: (i, c, 0)
    tile4 = lambda i, c: (i, 0, c, 0)
    part_args, part_specs, scratch = [], [], []
    for (o, m, l), dil in zip(attn_parts, DILATIONS):
        for a, width in ((o, QW), (m, LANES), (l, LANES)):
            if dil == 1:
                part_args.append(a)
                part_specs.append(pl.BlockSpec((1, TILE, width), tile3))
            else:
                part_args.append(a.reshape(b, dil, SEQ // dil, width))
                part_specs.append(pl.BlockSpec((1, dil, TILE // dil, width), tile4))
        if dil != 1:
            scratch += [pltpu.VMEM((QW // LANES, TILE, LANES), f32),
                        pltpu.VMEM((TILE, LANES), f32), pltpu.VMEM((TILE, LANES), f32)]
    in_specs = (
        [pl.BlockSpec((1, TILE, D_MODEL), tile3),
         pl.BlockSpec((1, 1, D_MODEL), per_b), pl.BlockSpec((1, 1, D_MODEL), per_b),
         pl.BlockSpec((1, 1, D_MODEL), per_b)]
        + [_full(v.shape) for v in vecs] + _weight_specs(weights) + part_specs)
    return pl.pallas_call(
        functools.partial(_rest_kernel, sample=False),
        grid=(b, tiles),
        in_specs=in_specs,
        out_specs=pl.BlockSpec((1, TILE, D_MODEL), tile3),
        out_shape=jax.ShapeDtypeStruct((b, SEQ, D_MODEL), f32),
        scratch_shapes=scratch,
        name="rest_p",
        compiler_params=pltpu.CompilerParams(
            dimension_semantics=("arbitrary", "arbitrary"), vmem_limit_bytes=VMEM_LIMIT),
    )(x, scale, shift, gate, *vecs, *weights, *part_args)


def _rest_sample(x, scale, shift, gate, vecs, weights, attn):
    n = x.shape[0]
    args = (x, scale, shift, gate, *vecs, *weights, attn)
    in_specs = ([_full(a.shape) for a in (x, scale, shift, gate, *vecs)] + _weight_specs(weights)
                + [_full(attn.shape)])
    return pl.pallas_call(
        functools.partial(_rest_kernel, sample=True),
        in_specs=in_specs,
        out_specs=[_full((n, D_MODEL)), _full((n, D_MODEL))],
        out_shape=[jax.ShapeDtypeStruct((n, D_MODEL), f32), jax.ShapeDtypeStruct((n, D_MODEL), f32)],
        grid=(1,),
        name="rest_s",
        compiler_params=pltpu.CompilerParams(
            dimension_semantics=("arbitrary",), vmem_limit_bytes=VMEM_LIMIT),
    )(*args)


def _kv_rows(kv, batch, rows):
    return kv.reshape(1, batch, rows, 2, N_HEADS, HEAD_DIM)


def kernel(x_prompt, x_sample, cache_kv_w128, cache_kv_w512, cache_kv_w2048, c_prompt, c_sample, w_cond, b_cond, g_pre, w_in, ln_v_g, ln_v_b, w_spatial, b_spatial, w_proj_a, w_proj_b, w_out, g_post):
    assert w_in.shape[0] == 1, "single layer"
    bp, seq, _ = x_prompt.shape
    bs, t_new, _ = x_sample.shape
    assert seq == SEQ and t_new == T_NEW

    w_in_b = w_in[0].astype(bf16)
    w_kvt = jnp.transpose(w_in[0][:, K_COL0:ZB_COL0].reshape(D_MODEL, 2, 3, QW), (2, 1, 3, 0))
    w_kvt = w_kvt.reshape(3, 2 * QW, D_MODEL).astype(bf16)
    causal = jnp.tril(jnp.ones((CHUNK, CHUNK), bool))
    w_sp = jnp.where(causal[None], w_spatial[0], 0.0)
    w_sp_p = w_sp.astype(bf16)
    n_s = bs * T_NEW
    b_sp_p = b_spatial[0].T
    b_sp_s = jnp.tile(b_spatial[0][:, :T_NEW].T, (bs, 1))
    weights_tail = (w_proj_a[0].astype(bf16), w_proj_b[0].astype(bf16), w_out[0].astype(bf16))
    vecs = (g_pre, g_post, ln_v_g, ln_v_b)

    mod = _cond(jnp.concatenate([c_prompt, c_sample], axis=0), w_cond[0], b_cond)
    shift, scale, gate = mod[:, :D_MODEL], mod[:, D_MODEL:2 * D_MODEL], mod[:, 2 * D_MODEL:]
    mp = lambda a: a[:bp].reshape(bp, 1, D_MODEL)
    ms = lambda a: jnp.repeat(a[bp:], T_NEW, axis=0)

    res = [_attn_prompt(x_prompt, mp(scale), mp(shift), g_pre, w_in_b, w_kvt, g) for g in range(3)]
    in_proj = (w_in_b, w_in_b, w_in_b)
    y_p = _rest_prompt(x_prompt, mp(scale), mp(shift), mp(gate), vecs,
                       in_proj + (w_sp_p, b_sp_p) + weights_tail, [r[:3] for r in res])
    kv_p = [jnp.transpose(r[3].reshape(bp, 2, N_HEADS, HEAD_DIM, win), (0, 4, 1, 2, 3))[None]
            for r, win in zip(res, WINDOWS)]

    xs = x_sample.reshape(n_s, D_MODEL)
    qkv_s = _qkv_sample(xs, ms(scale), ms(shift), g_pre, w_in_b)
    caches_t = []
    for cache, win in zip((cache_kv_w128, cache_kv_w512, cache_kv_w2048), WINDOWS):
        caches_t.append(jnp.transpose(cache[0], (0, 2, 3, 4, 1)).reshape(bs, 2 * QW, win))
    attn_s = _attn_sample(qkv_s.reshape(bs, T_NEW, 9 * QW), *caches_t).reshape(n_s, QW)
    y_s, v_n_s = _rest_sample(xs, ms(scale), ms(shift), ms(gate), vecs,
                              in_proj + (w_sp_p, b_sp_s) + weights_tail, attn_s)
    kv_s = []
    for g in range(3):
        kq = qkv_s[:, 3 * QW + g * QW:3 * QW + (g + 1) * QW]
        vq = qkv_s[:, 6 * QW + g * QW:6 * QW + (g + 1) * QW]
        kv_s.append(_kv_rows(jnp.concatenate([kq, vq], axis=1), bs, T_NEW))

    return (y_p, y_s.reshape(bs, T_NEW, D_MODEL), kv_p[0], kv_p[1], kv_p[2],
            kv_s[0], kv_s[1], kv_s[2], v_n_s.reshape(1, bs, T_NEW, D_MODEL))
```

```python
import functools

import jax
import jax.numpy as jnp
from jax import lax
from jax.experimental import pallas as pl
from jax.experimental.pallas import tpu as pltpu

D_MODEL = 1024
SEQ = 2048
HEAD_DIM = 64
N_HEADS = 8
QW = N_HEADS * HEAD_DIM
WINDOWS = (128, 512, 2048)
DILATIONS = (1, 4, 16)
N_BACK = 128
CHUNK = 128
A_GROUPS = 8
EPS = 1e-6
NEG = -1e30
LANES = 128
VMEM_LIMIT = 56 * 1024 * 1024
Q_COL0 = 3 * D_MODEL
K_COL0 = Q_COL0 + 3 * QW
V_COL0 = K_COL0 + 3 * QW
ZB_COL0 = V_COL0 + 3 * QW
GATE_COL0 = ZB_COL0 + QW

f32 = jnp.float32
bf16 = jnp.bfloat16


def _silu(x):
    return x * jax.nn.sigmoid(x)


def _norm_mod(x, g_pre, scale, shift):
    y = x * lax.rsqrt(jnp.mean(x * x, axis=-1, keepdims=True) + EPS) * g_pre
    return y * (1.0 + scale) + shift


def _dot(a, b):
    return jnp.dot(a, b, preferred_element_type=f32)


def _dot_nt(a, b):
    return lax.dot_general(a, b, (((1,), (1,)), ((), ())), preferred_element_type=f32)


def _cond_kernel(c_ref, w_ref, b_ref, o_ref):
    o_ref[...] = _dot(_silu(c_ref[...]).astype(bf16), w_ref[...].astype(bf16)) + b_ref[...]


def _cond(c_all, w_cond, b_cond):
    n = c_all.shape[0]
    return pl.pallas_call(
        _cond_kernel,
        out_shape=jax.ShapeDtypeStruct((n, 3 * D_MODEL), f32),
        name="cond",
        compiler_params=pltpu.CompilerParams(vmem_limit_bytes=VMEM_LIMIT),
    )(c_all, w_cond, b_cond)


ROWS_PER_STEP = 512
Q_BLOCK = 128
NORM_ROWS = 256


def _attn_kernel(x_ref, scale_ref, shift_ref, gpre_ref, wq_ref, wk_ref, wv_ref, wkvt_ref,
                 o_ref, m_ref, l_ref, kvt_ref, h_sc, k_sc, ve_sc, vo_sc, *, dil, window):
    c = pl.program_id(1)
    seg = SEQ // dil
    n_lc = D_MODEL // LANES

    @pl.when(c == 0)
    def _():
        zeros = jnp.zeros((Q_BLOCK, QW), bf16)
        k_sc[0:Q_BLOCK, :] = zeros
        ve_sc[0:Q_BLOCK, :] = zeros
        vo_sc[0:Q_BLOCK, :] = zeros
        for rb in range(SEQ // NORM_ROWS):
            rs = slice(rb * NORM_ROWS, (rb + 1) * NORM_ROWS)
            hb = _norm_mod(x_ref[0, rs, :], gpre_ref[...], scale_ref[0], shift_ref[0])
            for k in range(n_lc):
                h_sc[k, rs, :] = hb[:, k * LANES:(k + 1) * LANES]

    def rows_of(start, n):
        idx = pl.ds(pl.multiple_of(start, n), n) if dil == 1 else pl.ds(start, n, stride=dil)
        return jnp.concatenate([h_sc[k, idx, :] for k in range(n_lc)], axis=1)

    if seg >= ROWS_PER_STEP:
        per = seg // ROWS_PER_STEP
        h = rows_of((c // per) + (c % per) * ROWS_PER_STEP * dil, ROWS_PER_STEP)
    else:
        per = ROWS_PER_STEP // seg
        h = jnp.concatenate([rows_of(per * c + i, seg) for i in range(per)], axis=0)
    h = h.astype(bf16)
    base = pl.multiple_of(c * ROWS_PER_STEP, ROWS_PER_STEP)

    def tokens_t(start, n):
        return jnp.concatenate([h_sc[k, pl.ds(start, n), :] for k in range(n_lc)], axis=1).astype(bf16)

    if window == SEQ:
        kvt_ref[0] = _dot_nt(wkvt_ref[0], tokens_t(base, ROWS_PER_STEP))
    else:
        @pl.when(c == pl.num_programs(1) - 1)
        def _():
            kvt_ref[0] = _dot_nt(wkvt_ref[0], tokens_t(SEQ - window, window))

    low_w = (lax.broadcasted_iota(jnp.int32, (ROWS_PER_STEP, QW), 1) % LANES) < HEAD_DIM
    v = _dot(h, wv_ref[...])
    new_rows = pl.ds(Q_BLOCK + base, ROWS_PER_STEP)
    k_sc[new_rows, :] = _dot(h, wk_ref[...]).astype(bf16)
    ve_sc[new_rows, :] = jnp.where(low_w, v, 1.0).astype(bf16)
    vo_sc[new_rows, :] = jnp.where(low_w, 1.0, v).astype(bf16)
    q = _dot(h, wq_ref[...]) * (HEAD_DIM ** -0.5)
    q_even = jnp.where(low_w, q, 0.0).astype(bf16)
    q_odd = jnp.where(low_w, 0.0, q).astype(bf16)

    n_blocks = ROWS_PER_STEP // Q_BLOCK
    pairs = N_HEADS // 2
    qi = lax.broadcasted_iota(jnp.int32, (2 * Q_BLOCK, 2 * Q_BLOCK), 0) % Q_BLOCK
    kj = lax.broadcasted_iota(jnp.int32, (2 * Q_BLOCK, 2 * Q_BLOCK), 1)
    lane = lax.broadcasted_iota(jnp.int32, (Q_BLOCK, LANES), 1)
    low = lane < HEAD_DIM

    def has_prev(j):
        return seg > ROWS_PER_STEP or (j * Q_BLOCK) % seg != 0

    def key_rows(j):
        r0 = pl.multiple_of(base + j * Q_BLOCK, Q_BLOCK)
        return pl.ds(r0, 2 * Q_BLOCK) if has_prev(j) else pl.ds(r0 + Q_BLOCK, Q_BLOCK)

    def mask_of(j):
        if not has_prev(j):
            own = (2 * Q_BLOCK, Q_BLOCK)
            return (lax.broadcasted_iota(jnp.int32, own, 1)
                    <= lax.broadcasted_iota(jnp.int32, own, 0) % Q_BLOCK)
        off = jnp.where((base % seg) != 0, 0, Q_BLOCK) if (seg > ROWS_PER_STEP and j == 0) else 0
        in_prev = jnp.logical_and(kj < Q_BLOCK, kj >= qi + off)
        in_cur = jnp.logical_and(kj >= Q_BLOCK, kj - Q_BLOCK <= qi)
        return jnp.logical_or(in_prev, in_cur)

    def qk(j):
        rows = slice(j * Q_BLOCK, (j + 1) * Q_BLOCK)
        keys = key_rows(j)
        out = []
        for hp in range(pairs):
            cols = slice(hp * LANES, (hp + 1) * LANES)
            q_st = jnp.concatenate([q_even[rows, cols], q_odd[rows, cols]], axis=0)
            out.append(_dot_nt(q_st, k_sc[keys, cols]))
        return out

    def finish(j, scores):
        rows = slice(j * Q_BLOCK, (j + 1) * Q_BLOCK)
        keys = key_rows(j)
        mask = mask_of(j)
        ps, ms = [], []
        for hp in range(pairs):
            s = jnp.where(mask, scores[hp], NEG)
            for half in (s[:Q_BLOCK], s[Q_BLOCK:]):
                m = jnp.max(half, axis=-1, keepdims=True)
                ps.append(jnp.exp(half - m).astype(bf16))
                ms.append(m)
        o_parts = []
        m_tile = jnp.zeros((Q_BLOCK, LANES), f32)
        l_tile = jnp.ones((Q_BLOCK, LANES), f32)
        for hp in range(pairs):
            cols = slice(hp * LANES, (hp + 1) * LANES)
            acc_e = _dot(ps[2 * hp], ve_sc[keys, cols])
            acc_o = _dot(ps[2 * hp + 1], vo_sc[keys, cols])
            o_parts.append(jnp.where(low, acc_e, acc_o))
            l_tile = jnp.where(lane == HEAD_DIM + hp, acc_e, jnp.where(lane == hp, acc_o, l_tile))
            m_tile = jnp.where(lane == HEAD_DIM + hp, ms[2 * hp],
                               jnp.where(lane == hp, ms[2 * hp + 1], m_tile))
        o_ref[0, rows, :] = jnp.concatenate(o_parts, axis=1).astype(bf16)
        m_ref[0, rows, :] = m_tile
        l_ref[0, rows, :] = l_tile

    scores = qk(0)
    for j in range(n_blocks):
        nxt = qk(j + 1) if j + 1 < n_blocks else None
        finish(j, scores)
        scores = nxt


def _attn_prompt(x, scale, shift, g_pre, w_in_b, w_kvt, pattern):
    b = x.shape[0]
    steps = SEQ // ROWS_PER_STEP
    dil, window = DILATIONS[pattern], WINDOWS[pattern]
    q_blk, k_blk, v_blk = (Q_COL0 // QW + pattern, K_COL0 // QW + pattern, V_COL0 // QW + pattern)
    if window == SEQ:
        kvt_spec = pl.BlockSpec((1, 2 * QW, ROWS_PER_STEP), lambda i, c: (i, 0, c))
    else:
        kvt_spec = pl.BlockSpec((1, 2 * QW, window), lambda i, c: (i, 0, 0))
    return pl.pallas_call(
        functools.partial(_attn_kernel, dil=dil, window=window),
        grid=(b, steps),
        in_specs=[
            pl.BlockSpec((1, SEQ, D_MODEL), lambda i, c: (i, 0, 0)),
            pl.BlockSpec((1, 1, D_MODEL), lambda i, c: (i, 0, 0)),
            pl.BlockSpec((1, 1, D_MODEL), lambda i, c: (i, 0, 0)),
            pl.BlockSpec((1, D_MODEL), lambda i, c: (0, 0)),
            pl.BlockSpec((D_MODEL, QW), lambda i, c: (0, q_blk)),
            pl.BlockSpec((D_MODEL, QW), lambda i, c: (0, k_blk)),
            pl.BlockSpec((D_MODEL, QW), lambda i, c: (0, v_blk)),
            pl.BlockSpec((1, 2 * QW, D_MODEL), lambda i, c: (pattern, 0, 0)),
        ],
        out_specs=[
            pl.BlockSpec((1, ROWS_PER_STEP, QW), lambda i, c: (i, c, 0)),
            pl.BlockSpec((1, ROWS_PER_STEP, LANES), lambda i, c: (i, c, 0)),
            pl.BlockSpec((1, ROWS_PER_STEP, LANES), lambda i, c: (i, c, 0)),
            kvt_spec,
        ],
        out_shape=[
            jax.ShapeDtypeStruct((b, SEQ, QW), bf16),
            jax.ShapeDtypeStruct((b, SEQ, LANES), f32),
            jax.ShapeDtypeStruct((b, SEQ, LANES), f32),
            jax.ShapeDtypeStruct((b, 2 * QW, window), f32),
        ],
        scratch_shapes=[pltpu.VMEM((D_MODEL // LANES, SEQ, LANES), f32),
                        pltpu.VMEM((SEQ + Q_BLOCK, QW), bf16), pltpu.VMEM((SEQ + Q_BLOCK, QW), bf16),
                        pltpu.VMEM((SEQ + Q_BLOCK, QW), bf16)],
        name=f"attn_d{dil}",
        compiler_params=pltpu.CompilerParams(
            dimension_semantics=("arbitrary", "arbitrary"), vmem_limit_bytes=VMEM_LIMIT),
    )(x, scale, shift, g_pre, w_in_b, w_in_b, w_in_b, w_kvt)


def _qkv_s_kernel(x_ref, scale_ref, shift_ref, gpre_ref, w_ref, o_ref):
    h = _norm_mod(x_ref[...], gpre_ref[...], scale_ref[...], shift_ref[...]).astype(bf16)
    o_ref[...] = _dot(h, w_ref[...])


def _qkv_sample(x, scale, shift, g_pre, w_in_b):
    n = x.shape[0]
    cols = 9 * QW
    tile = 3 * QW
    first = Q_COL0 // tile
    return pl.pallas_call(
        _qkv_s_kernel,
        grid=(cols // tile,),
        in_specs=[
            pl.BlockSpec((n, D_MODEL), lambda j: (0, 0)),
            pl.BlockSpec((n, D_MODEL), lambda j: (0, 0)),
            pl.BlockSpec((n, D_MODEL), lambda j: (0, 0)),
            pl.BlockSpec((1, D_MODEL), lambda j: (0, 0)),
            pl.BlockSpec((D_MODEL, tile), lambda j: (0, first + j)),
        ],
        out_specs=pl.BlockSpec((n, tile), lambda j: (0, j)),
        out_shape=jax.ShapeDtypeStruct((n, cols), f32),
        name="qkv_s",
        compiler_params=pltpu.CompilerParams(
            dimension_semantics=("arbitrary",), vmem_limit_bytes=VMEM_LIMIT),
    )(x, scale, shift, g_pre, w_in_b)


T_NEW = 8


def _sample_attention(qkv_ref, c0_ref, c1_ref, c2_ref, o_ref, new_sc):
    qkv = qkv_ref[0]
    new_sc[0:T_NEW, :] = qkv[:, 3 * QW:]
    q_all = qkv[:, :3 * QW] * (HEAD_DIM ** -0.5)

    lane8 = lax.broadcasted_iota(jnp.int32, (2 * T_NEW, LANES), 1)
    low = lane8 < HEAD_DIM
    t_idx = lax.broadcasted_iota(jnp.int32, (2 * T_NEW, LANES), 0) % T_NEW
    caches = (c0_ref, c1_ref, c2_ref)
    pairs = range(N_HEADS // 2)
    pats = range(len(DILATIONS))

    def pair_cols(g, hp, base=0):
        return slice(base + g * QW + hp * LANES, base + g * QW + (hp + 1) * LANES)

    s_cache, s_new = {}, {}
    for hp in pairs:
        for g in pats:
            q2 = q_all[:, pair_cols(g, hp)]
            q_st = jnp.concatenate([jnp.where(low[:T_NEW], q2, 0.0), jnp.where(low[:T_NEW], 0.0, q2)],
                                   axis=0).astype(bf16)
            kt = caches[g][0, hp * LANES:(hp + 1) * LANES, :].astype(bf16)
            k_new = new_sc[:, pair_cols(g, hp)].astype(bf16)
            s_cache[hp, g] = _dot(q_st, kt)
            s_new[hp, g] = _dot_nt(q_st, k_new)

    probs = {}
    for hp in pairs:
        m = jnp.full((2 * T_NEW, 1), NEG, f32)
        for g, dil in enumerate(DILATIONS):
            rows = WINDOWS[g]
            rho = lax.broadcasted_iota(jnp.int32, (2 * T_NEW, rows), 1)
            tq = lax.broadcasted_iota(jnp.int32, (2 * T_NEW, rows), 0) % T_NEW
            ok_c = jnp.logical_and(rho >= tq, ((rho - tq) & (dil - 1)) == 0)
            ok_n = jnp.logical_and(lane8 <= t_idx, ((t_idx - lane8) & (dil - 1)) == 0)
            s_cache[hp, g] = jnp.where(ok_c, s_cache[hp, g], NEG)
            s_new[hp, g] = jnp.where(ok_n, s_new[hp, g], NEG)
            m = jnp.maximum(m, jnp.maximum(jnp.max(s_cache[hp, g], axis=-1, keepdims=True),
                                           jnp.max(s_new[hp, g], axis=-1, keepdims=True)))
        l = jnp.zeros((2 * T_NEW, 1), f32)
        for g in pats:
            p_c = jnp.exp(s_cache[hp, g] - m)
            p_n = jnp.exp(s_new[hp, g] - m)
            l = l + jnp.sum(p_c, axis=-1, keepdims=True) + jnp.sum(p_n, axis=-1, keepdims=True)
            probs[hp, g] = (p_c.astype(bf16), p_n.astype(bf16))
        probs[hp] = l

    o_parts = []
    for hp in pairs:
        acc = jnp.zeros((2 * T_NEW, LANES), f32)
        for g in pats:
            vt = caches[g][0, QW + hp * LANES: QW + (hp + 1) * LANES, :].astype(bf16)
            v_new = new_sc[:, pair_cols(g, hp, 3 * QW)].astype(bf16)
            p_c, p_n = probs[hp, g]
            acc = acc + _dot_nt(p_c, vt) + _dot(p_n, v_new)
        out = acc / probs[hp]
        o_parts.append(jnp.where(low[:T_NEW], out[:T_NEW], out[T_NEW:]))
    o_ref[0] = jnp.concatenate(o_parts, axis=1)


TILE = 256


def _expand_heads(w):
    r = lax.broadcasted_iota(jnp.int32, (2 * LANES, QW), 0) % LANES
    head = lax.broadcasted_iota(jnp.int32, (2 * LANES, QW), 1) // HEAD_DIM
    expand = (r == head // 2 + HEAD_DIM * (1 - head % 2)).astype(bf16)
    hi = w.astype(bf16)
    lo = (w - hi.astype(f32)).astype(bf16)
    return _dot(jnp.concatenate([hi, lo], axis=1), expand)


def _rest_kernel(*refs, sample, sample_every=None):
    if sample:
        (x_ref, scale_ref, shift_ref, gate_ref, gpre_ref, gpost_ref, lng_ref, lnb_ref,
         wa_ref, wzb_ref, wg_ref, wsp_ref, bsp_ref, wpa_ref, wpb_ref, wout_ref,
         attn_ref, y_ref, vn_ref) = refs
        x = x_ref[...]
        scale, shift, gate = scale_ref[...], shift_ref[...], gate_ref[...]
    else:
        (x_ref, scale_ref, shift_ref, gate_ref, gpre_ref, gpost_ref, lng_ref, lnb_ref,
         wa_ref, wzb_ref, wg_ref, wsp_ref, bsp_ref, wpa_ref, wpb_ref, wout_ref,
         o1_ref, m1_ref, l1_ref, o4_ref, m4_ref, l4_ref, o16_ref, m16_ref, l16_ref,
         qkv_s_ref, c0_ref, c1_ref, c2_ref, y_ref, attn_s_ref,
         o4_sc, m4_sc, l4_sc, o16_sc, m16_sc, l16_sc, new_sc) = refs
        x = x_ref[0]
        scale, shift, gate = scale_ref[0], shift_ref[0], gate_ref[0]

        @pl.when(jnp.logical_and(pl.program_id(0) == 0, pl.program_id(1) == 0))
        def _():
            new_sc[...] = jnp.zeros_like(new_sc)

        @pl.when(pl.program_id(1) % sample_every == 0)
        def _():
            _sample_attention(qkv_s_ref, c0_ref, c1_ref, c2_ref, attn_s_ref, new_sc)

    rows = x.shape[0]
    h = _norm_mod(x, gpre_ref[...], scale, shift).astype(bf16)

    pa = _dot(h, wa_ref[...])
    u_a, v_a, z_a = pa[:, :D_MODEL], pa[:, D_MODEL:2 * D_MODEL], pa[:, 2 * D_MODEL:]
    mu = jnp.mean(v_a, axis=-1, keepdims=True)
    cen = v_a - mu
    var = jnp.mean(cen * cen, axis=-1, keepdims=True)
    v_n = cen * lax.rsqrt(var + EPS) * lng_ref[...] + lnb_ref[...]
    v_nb = v_n.astype(bf16)
    if sample:
        vn_ref[...] = v_n
        pick = (lax.broadcasted_iota(jnp.int32, (rows, CHUNK), 1)
                == lax.broadcasted_iota(jnp.int32, (rows, CHUNK), 0) % T_NEW).astype(bf16)
        same_seq = (lax.broadcasted_iota(jnp.int32, (rows, rows), 0) // T_NEW
                    == lax.broadcasted_iota(jnp.int32, (rows, rows), 1) // T_NEW)
        cols = []
        for g in range(A_GROUPS):
            tiled = _dot_nt(_dot(pick, wsp_ref[g]).astype(bf16), pick)
            w_blk = jnp.where(same_seq, tiled, 0.0).astype(bf16)
            cols.append(_dot(w_blk, v_nb[:, g * LANES:(g + 1) * LANES]) + bsp_ref[:, g:g + 1])
        zs = jnp.concatenate(cols, axis=1)
    else:
        n_ck = rows // CHUNK
        per_group = []
        for g in range(A_GROUPS):
            rhs = jnp.concatenate(
                [v_nb[ck * CHUNK:(ck + 1) * CHUNK, g * LANES:(g + 1) * LANES] for ck in range(n_ck)], axis=1)
            per_group.append(_dot(wsp_ref[g], rhs) + bsp_ref[:, g:g + 1])
        zs = jnp.concatenate(
            [jnp.concatenate([pg[:, ck * LANES:(ck + 1) * LANES] for pg in per_group], axis=1)
             for ck in range(n_ck)], axis=0)
    y_a = u_a * zs * _silu(z_a)

    if sample:
        attn = attn_ref[...]
    else:
        for osc, msc, lsc, oref, mref, lref, dil in (
                (o4_sc, m4_sc, l4_sc, o4_ref, m4_ref, l4_ref, DILATIONS[1]),
                (o16_sc, m16_sc, l16_sc, o16_ref, m16_ref, l16_ref, DILATIONS[2])):
            n = rows // dil
            for r in range(dil):
                o_r = oref[0, r].astype(f32)
                for k in range(QW // LANES):
                    osc[k, pl.ds(r, n, stride=dil), :] = o_r[:, k * LANES:(k + 1) * LANES]
                msc[pl.ds(r, n, stride=dil), :] = mref[0, r]
                lsc[pl.ds(r, n, stride=dil), :] = lref[0, r]
        ms = (m1_ref[0], m4_sc[...], m16_sc[...])
        ls = (l1_ref[0], l4_sc[...], l16_sc[...])
        unchunk = lambda sc: jnp.concatenate([sc[k] for k in range(QW // LANES)], axis=1)
        outs = (o1_ref[0].astype(f32), unchunk(o4_sc), unchunk(o16_sc))
        m_all = jnp.maximum(jnp.maximum(ms[0], ms[1]), ms[2])
        ws = [jnp.exp(m - m_all) for m in ms]
        den = ws[0] * ls[0] + ws[1] * ls[1] + ws[2] * ls[2]
        attn = sum(_expand_heads(w / den) * o for w, o in zip(ws, outs))

    z_b = _dot(h, wzb_ref[...])
    y_b = (attn * _silu(z_b)).astype(bf16)

    gl = _dot(h, wg_ref[...])
    p_a = _dot(y_a.astype(bf16), wpa_ref[...])
    p_b = _dot(y_b, wpb_ref[...])
    merged = jax.nn.sigmoid(gl[:, :D_MODEL]) * p_a + jax.nn.sigmoid(gl[:, D_MODEL:]) * p_b
    out = _dot(merged.astype(bf16), wout_ref[...])
    normed = out * lax.rsqrt(jnp.mean(out * out, axis=-1, keepdims=True) + EPS) * gpost_ref[...]
    y = x + gate * normed
    if sample:
        y_ref[...] = y
    else:
        y_ref[0] = y


def _full(shape):
    nd = len(shape)
    return pl.BlockSpec(shape, lambda *_: (0,) * nd)


def _weight_specs(weights):
    in_proj = [pl.BlockSpec((D_MODEL, 3 * D_MODEL), lambda *_: (0, 0)),
               pl.BlockSpec((D_MODEL, QW), lambda *_: (0, ZB_COL0 // QW)),
               pl.BlockSpec((D_MODEL, 2 * D_MODEL), lambda *_: (0, GATE_COL0 // (2 * D_MODEL)))]
    return in_proj + [_full(w.shape) for w in weights[3:]]


def _rest_prompt(x, scale, shift, gate, vecs, weights, attn_parts, qkv_s, caches):
    b = x.shape[0]
    tiles = SEQ // TILE
    n_seq = qkv_s.shape[0]
    sample_every, rem = divmod(b * tiles, n_seq)
    assert rem == 0 and tiles % sample_every == 0, "sample sequences must tile the grid evenly"
    seq_of = lambda i, c: (i * (tiles // sample_every) + c // sample_every, 0, 0)
    per_b = lambda i, c: (i, 0, 0)
    tile3 = lambda i, c: (i, c, 0)
    tile4 = lambda i, c: (i, 0, c, 0)
    part_args, part_specs, scratch = [], [], []
    for (o, m, l), dil in zip(attn_parts, DILATIONS):
        for a, width in ((o, QW), (m, LANES), (l, LANES)):
            if dil == 1:
                part_args.append(a)
                part_specs.append(pl.BlockSpec((1, TILE, width), tile3))
            else:
                part_args.append(a.reshape(b, dil, SEQ // dil, width))
                part_specs.append(pl.BlockSpec((1, dil, TILE // dil, width), tile4))
        if dil != 1:
            scratch += [pltpu.VMEM((QW // LANES, TILE, LANES), f32),
                        pltpu.VMEM((TILE, LANES), f32), pltpu.VMEM((TILE, LANES), f32)]
    in_specs = (
        [pl.BlockSpec((1, TILE, D_MODEL), tile3),
         pl.BlockSpec((1, 1, D_MODEL), per_b), pl.BlockSpec((1, 1, D_MODEL), per_b),
         pl.BlockSpec((1, 1, D_MODEL), per_b)]
        + [_full(v.shape) for v in vecs] + _weight_specs(weights) + part_specs
        + [pl.BlockSpec((1, T_NEW, 9 * QW), seq_of)]
        + [pl.BlockSpec((1, 2 * QW, win), seq_of) for win in WINDOWS])
    scratch.append(pltpu.VMEM((LANES, 6 * QW), f32))
    return pl.pallas_call(
        functools.partial(_rest_kernel, sample=False, sample_every=sample_every),
        grid=(b, tiles),
        in_specs=in_specs,
        out_specs=[pl.BlockSpec((1, TILE, D_MODEL), tile3), pl.BlockSpec((1, T_NEW, QW), seq_of)],
        out_shape=[jax.ShapeDtypeStruct((b, SEQ, D_MODEL), f32),
                   jax.ShapeDtypeStruct((n_seq, T_NEW, QW), f32)],
        scratch_shapes=scratch,
        name="rest_p",
        compiler_params=pltpu.CompilerParams(
            dimension_semantics=("arbitrary", "arbitrary"), vmem_limit_bytes=VMEM_LIMIT),
    )(x, scale, shift, gate, *vecs, *weights, *part_args, qkv_s, *caches)


def _rest_sample(x, scale, shift, gate, vecs, weights, attn):
    n = x.shape[0]
    args = (x, scale, shift, gate, *vecs, *weights, attn)
    in_specs = ([_full(a.shape) for a in (x, scale, shift, gate, *vecs)] + _weight_specs(weights)
                + [_full(attn.shape)])
    return pl.pallas_call(
        functools.partial(_rest_kernel, sample=True),
        in_specs=in_specs,
        out_specs=[_full((n, D_MODEL)), _full((n, D_MODEL))],
        out_shape=[jax.ShapeDtypeStruct((n, D_MODEL), f32), jax.ShapeDtypeStruct((n, D_MODEL), f32)],
        grid=(1,),
        name="rest_s",
        compiler_params=pltpu.CompilerParams(
            dimension_semantics=("arbitrary",), vmem_limit_bytes=VMEM_LIMIT),
    )(*args)


def _kv_rows(kv, batch, rows):
    return kv.reshape(1, batch, rows, 2, N_HEADS, HEAD_DIM)


def kernel(x_prompt, x_sample, cache_kv_w128, cache_kv_w512, cache_kv_w2048, c_prompt, c_sample, w_cond, b_cond, g_pre, w_in, ln_v_g, ln_v_b, w_spatial, b_spatial, w_proj_a, w_proj_b, w_out, g_post):
    assert w_in.shape[0] == 1, "single layer"
    bp, seq, _ = x_prompt.shape
    bs, t_new, _ = x_sample.shape
    assert seq == SEQ and t_new == T_NEW

    w_in_b = w_in[0].astype(bf16)
    w_kvt = jnp.transpose(w_in[0][:, K_COL0:ZB_COL0].reshape(D_MODEL, 2, 3, QW), (2, 1, 3, 0))
    w_kvt = w_kvt.reshape(3, 2 * QW, D_MODEL).astype(bf16)
    causal = jnp.tril(jnp.ones((CHUNK, CHUNK), bool))
    w_sp = jnp.where(causal[None], w_spatial[0], 0.0)
    w_sp_p = w_sp.astype(bf16)
    n_s = bs * T_NEW
    b_sp_p = b_spatial[0].T
    b_sp_s = jnp.tile(b_spatial[0][:, :T_NEW].T, (bs, 1))
    weights_tail = (w_proj_a[0].astype(bf16), w_proj_b[0].astype(bf16), w_out[0].astype(bf16))
    vecs = (g_pre, g_post, ln_v_g, ln_v_b)

    mod = _cond(jnp.concatenate([c_prompt, c_sample], axis=0), w_cond[0], b_cond)
    shift, scale, gate = mod[:, :D_MODEL], mod[:, D_MODEL:2 * D_MODEL], mod[:, 2 * D_MODEL:]
    mp = lambda a: a[:bp].reshape(bp, 1, D_MODEL)
    ms = lambda a: jnp.repeat(a[bp:], T_NEW, axis=0)

    xs = x_sample.reshape(n_s, D_MODEL)
    qkv_s = _qkv_sample(xs, ms(scale), ms(shift), g_pre, w_in_b)
    caches_t = []
    for cache, win in zip((cache_kv_w128, cache_kv_w512, cache_kv_w2048), WINDOWS):
        caches_t.append(jnp.transpose(cache[0], (0, 2, 3, 4, 1)).reshape(bs, 2 * QW, win))

    res = [_attn_prompt(x_prompt, mp(scale), mp(shift), g_pre, w_in_b, w_kvt, g) for g in range(3)]
    in_proj = (w_in_b, w_in_b, w_in_b)
    y_p, attn_s = _rest_prompt(x_prompt, mp(scale), mp(shift), mp(gate), vecs,
                               in_proj + (w_sp_p, b_sp_p) + weights_tail, [r[:3] for r in res],
                               qkv_s.reshape(bs, T_NEW, 9 * QW), caches_t)
    kv_p = [jnp.transpose(r[3].reshape(bp, 2, N_HEADS, HEAD_DIM, win), (0, 4, 1, 2, 3))[None]
            for r, win in zip(res, WINDOWS)]

    attn_s = attn_s.reshape(n_s, QW)
    y_s, v_n_s = _rest_sample(xs, ms(scale), ms(shift), ms(gate), vecs,
                              in_proj + (w_sp_p, b_sp_s) + weights_tail, attn_s)
    kv_s = []
    for g in range(3):
        kq = qkv_s[:, 3 * QW + g * QW:3 * QW + (g + 1) * QW]
        vq = qkv_s[:, 6 * QW + g * QW:6 * QW + (g + 1) * QW]
        kv_s.append(_kv_rows(jnp.concatenate([kq, vq], axis=1), bs, T_NEW))

    return (y_p, y_s.reshape(bs, T_NEW, D_MODEL), kv_p[0], kv_p[1], kv_p[2],
            kv_s[0], kv_s[1], kv_s[2], v_n_s.reshape(1, bs, T_NEW, D_MODEL))
```

```python
import functools

import jax
import jax.numpy as jnp
from jax import lax
from jax.experimental import pallas as pl
from jax.experimental.pallas import tpu as pltpu

D_MODEL = 1024
SEQ = 2048
HEAD_DIM = 64
N_HEADS = 8
QW = N_HEADS * HEAD_DIM
WINDOWS = (128, 512, 2048)
DILATIONS = (1, 4, 16)
N_BACK = 128
CHUNK = 128
A_GROUPS = 8
EPS = 1e-6
NEG = -1e30
LANES = 128
VMEM_LIMIT = 56 * 1024 * 1024
Q_COL0 = 3 * D_MODEL
K_COL0 = Q_COL0 + 3 * QW
V_COL0 = K_COL0 + 3 * QW
ZB_COL0 = V_COL0 + 3 * QW
GATE_COL0 = ZB_COL0 + QW

f32 = jnp.float32
bf16 = jnp.bfloat16


def _silu(x):
    return x * jax.nn.sigmoid(x)


def _norm_mod(x, g_pre, scale, shift):
    y = x * lax.rsqrt(jnp.mean(x * x, axis=-1, keepdims=True) + EPS) * g_pre
    return y * (1.0 + scale) + shift


def _dot(a, b):
    return jnp.dot(a, b, preferred_element_type=f32)


def _dot_nt(a, b):
    return lax.dot_general(a, b, (((1,), (1,)), ((), ())), preferred_element_type=f32)


def _cond_kernel(c_ref, w_ref, b_ref, o_ref):
    o_ref[...] = _dot(_silu(c_ref[...]).astype(bf16), w_ref[...].astype(bf16)) + b_ref[...]


def _cond(c_all, w_cond, b_cond):
    n = c_all.shape[0]
    return pl.pallas_call(
        _cond_kernel,
        out_shape=jax.ShapeDtypeStruct((n, 3 * D_MODEL), f32),
        name="cond",
        compiler_params=pltpu.CompilerParams(vmem_limit_bytes=VMEM_LIMIT),
    )(c_all, w_cond, b_cond)


ROWS_PER_STEP = 512
Q_BLOCK = 128
NORM_ROWS = 256


def _attn_kernel(x_ref, scale_ref, shift_ref, gpre_ref, wq_ref, wk_ref, wv_ref, wkvt_ref,
                 o_ref, m_ref, l_ref, kvt_ref, h_sc, k_sc, ve_sc, vo_sc, *, dil, window):
    c = pl.program_id(1)
    seg = SEQ // dil
    n_lc = D_MODEL // LANES

    @pl.when(c == 0)
    def _():
        zeros = jnp.zeros((Q_BLOCK, QW), bf16)
        k_sc[0:Q_BLOCK, :] = zeros
        ve_sc[0:Q_BLOCK, :] = zeros
        vo_sc[0:Q_BLOCK, :] = zeros
        for rb in range(SEQ // NORM_ROWS):
            rs = slice(rb * NORM_ROWS, (rb + 1) * NORM_ROWS)
            hb = _norm_mod(x_ref[0, rs, :], gpre_ref[...], scale_ref[0], shift_ref[0])
            for k in range(n_lc):
                h_sc[k, rs, :] = hb[:, k * LANES:(k + 1) * LANES]

    def rows_of(start, n):
        idx = pl.ds(pl.multiple_of(start, n), n) if dil == 1 else pl.ds(start, n, stride=dil)
        return jnp.concatenate([h_sc[k, idx, :] for k in range(n_lc)], axis=1)

    if seg >= ROWS_PER_STEP:
        per = seg // ROWS_PER_STEP
        h = rows_of((c // per) + (c % per) * ROWS_PER_STEP * dil, ROWS_PER_STEP)
    else:
        per = ROWS_PER_STEP // seg
        h = jnp.concatenate([rows_of(per * c + i, seg) for i in range(per)], axis=0)
    h = h.astype(bf16)
    base = pl.multiple_of(c * ROWS_PER_STEP, ROWS_PER_STEP)

    def tokens_t(start, n):
        return jnp.concatenate([h_sc[k, pl.ds(start, n), :] for k in range(n_lc)], axis=1).astype(bf16)

    if window == SEQ:
        kvt_ref[0] = _dot_nt(wkvt_ref[0], tokens_t(base, ROWS_PER_STEP))
    else:
        @pl.when(c == pl.num_programs(1) - 1)
        def _():
            kvt_ref[0] = _dot_nt(wkvt_ref[0], tokens_t(SEQ - window, window))

    low_w = (lax.broadcasted_iota(jnp.int32, (ROWS_PER_STEP, QW), 1) % LANES) < HEAD_DIM
    v = _dot(h, wv_ref[...])
    new_rows = pl.ds(Q_BLOCK + base, ROWS_PER_STEP)
    k_sc[new_rows, :] = _dot(h, wk_ref[...]).astype(bf16)
    ve_sc[new_rows, :] = jnp.where(low_w, v, 1.0).astype(bf16)
    vo_sc[new_rows, :] = jnp.where(low_w, 1.0, v).astype(bf16)
    q = _dot(h, wq_ref[...]) * (HEAD_DIM ** -0.5)
    q_even = jnp.where(low_w, q, 0.0).astype(bf16)
    q_odd = jnp.where(low_w, 0.0, q).astype(bf16)

    n_blocks = ROWS_PER_STEP // Q_BLOCK
    pairs = N_HEADS // 2
    qi = lax.broadcasted_iota(jnp.int32, (2 * Q_BLOCK, 2 * Q_BLOCK), 0) % Q_BLOCK
    kj = lax.broadcasted_iota(jnp.int32, (2 * Q_BLOCK, 2 * Q_BLOCK), 1)
    lane = lax.broadcasted_iota(jnp.int32, (Q_BLOCK, LANES), 1)
    low = lane < HEAD_DIM

    def has_prev(j):
        return seg > ROWS_PER_STEP or (j * Q_BLOCK) % seg != 0

    def key_rows(j):
        r0 = pl.multiple_of(base + j * Q_BLOCK, Q_BLOCK)
        return pl.ds(r0, 2 * Q_BLOCK) if has_prev(j) else pl.ds(r0 + Q_BLOCK, Q_BLOCK)

    def mask_of(j):
        if not has_prev(j):
            own = (2 * Q_BLOCK, Q_BLOCK)
            return (lax.broadcasted_iota(jnp.int32, own, 1)
                    <= lax.broadcasted_iota(jnp.int32, own, 0) % Q_BLOCK)
        off = jnp.where((base % seg) != 0, 0, Q_BLOCK) if (seg > ROWS_PER_STEP and j == 0) else 0
        in_prev = jnp.logical_and(kj < Q_BLOCK, kj >= qi + off)
        in_cur = jnp.logical_and(kj >= Q_BLOCK, kj - Q_BLOCK <= qi)
        return jnp.logical_or(in_prev, in_cur)

    def qk(j):
        rows = slice(j * Q_BLOCK, (j + 1) * Q_BLOCK)
        keys = key_rows(j)
        out = []
        for hp in range(pairs):
            cols = slice(hp * LANES, (hp + 1) * LANES)
            q_st = jnp.concatenate([q_even[rows, cols], q_odd[rows, cols]], axis=0)
            out.append(_dot_nt(q_st, k_sc[keys, cols]))
        return out

    def finish(j, scores):
        rows = slice(j * Q_BLOCK, (j + 1) * Q_BLOCK)
        keys = key_rows(j)
        mask = mask_of(j)
        ps, ms = [], []
        for hp in range(pairs):
            s = jnp.where(mask, scores[hp], NEG)
            for half in (s[:Q_BLOCK], s[Q_BLOCK:]):
                m = jnp.max(half, axis=-1, keepdims=True)
                ps.append(jnp.exp(half - m).astype(bf16))
                ms.append(m)
        o_parts = []
        m_tile = jnp.zeros((Q_BLOCK, LANES), f32)
        l_tile = jnp.ones((Q_BLOCK, LANES), f32)
        for hp in range(pairs):
            cols = slice(hp * LANES, (hp + 1) * LANES)
            acc_e = _dot(ps[2 * hp], ve_sc[keys, cols])
            acc_o = _dot(ps[2 * hp + 1], vo_sc[keys, cols])
            o_parts.append(jnp.where(low, acc_e, acc_o))
            l_tile = jnp.where(lane == HEAD_DIM + hp, acc_e, jnp.where(lane == hp, acc_o, l_tile))
            m_tile = jnp.where(lane == HEAD_DIM + hp, ms[2 * hp],
                               jnp.where(lane == hp, ms[2 * hp + 1], m_tile))
        o_ref[0, rows, :] = jnp.concatenate(o_parts, axis=1).astype(bf16)
        m_ref[0, rows, :] = m_tile
        l_ref[0, rows, :] = l_tile

    scores = qk(0)
    for j in range(n_blocks):
        nxt = qk(j + 1) if j + 1 < n_blocks else None
        finish(j, scores)
        scores = nxt


def _attn_prompt(x, scale, shift, g_pre, w_in_b, w_kvt, pattern):
    b = x.shape[0]
    steps = SEQ // ROWS_PER_STEP
    dil, window = DILATIONS[pattern], WINDOWS[pattern]
    q_blk, k_blk, v_blk = (Q_COL0 // QW + pattern, K_COL0 // QW + pattern, V_COL0 // QW + pattern)
    if window == SEQ:
        kvt_spec = pl.BlockSpec((1, 2 * QW, ROWS_PER_STEP), lambda i, c: (i, 0, c))
    else:
        kvt_spec = pl.BlockSpec((1, 2 * QW, window), lambda i, c: (i, 0, 0))
    return pl.pallas_call(
        functools.partial(_attn_kernel, dil=dil, window=window),
        grid=(b, steps),
        in_specs=[
            pl.BlockSpec((1, SEQ, D_MODEL), lambda i, c: (i, 0, 0)),
            pl.BlockSpec((1, 1, D_MODEL), lambda i, c: (i, 0, 0)),
            pl.BlockSpec((1, 1, D_MODEL), lambda i, c: (i, 0, 0)),
            pl.BlockSpec((1, D_MODEL), lambda i, c: (0, 0)),
            pl.BlockSpec((D_MODEL, QW), lambda i, c: (0, q_blk)),
            pl.BlockSpec((D_MODEL, QW), lambda i, c: (0, k_blk)),
            pl.BlockSpec((D_MODEL, QW), lambda i, c: (0, v_blk)),
            pl.BlockSpec((1, 2 * QW, D_MODEL), lambda i, c: (pattern, 0, 0)),
        ],
        out_specs=[
            pl.BlockSpec((1, ROWS_PER_STEP, QW), lambda i, c: (i, c, 0)),
            pl.BlockSpec((1, ROWS_PER_STEP, LANES), lambda i, c: (i, c, 0)),
            pl.BlockSpec((1, ROWS_PER_STEP, LANES), lambda i, c: (i, c, 0)),
            kvt_spec,
        ],
        out_shape=[
            jax.ShapeDtypeStruct((b, SEQ, QW), bf16),
            jax.ShapeDtypeStruct((b, SEQ, LANES), f32),
            jax.ShapeDtypeStruct((b, SEQ, LANES), f32),
            jax.ShapeDtypeStruct((b, 2 * QW, window), f32),
        ],
        scratch_shapes=[pltpu.VMEM((D_MODEL // LANES, SEQ, LANES), f32),
                        pltpu.VMEM((SEQ + Q_BLOCK, QW), bf16), pltpu.VMEM((SEQ + Q_BLOCK, QW), bf16),
                        pltpu.VMEM((SEQ + Q_BLOCK, QW), bf16)],
        name=f"attn_d{dil}",
        compiler_params=pltpu.CompilerParams(
            dimension_semantics=("arbitrary", "arbitrary"), vmem_limit_bytes=VMEM_LIMIT),
    )(x, scale, shift, g_pre, w_in_b, w_in_b, w_in_b, w_kvt)


def _qkv_s_kernel(x_ref, scale_ref, shift_ref, gpre_ref, w_ref, o_ref):
    h = _norm_mod(x_ref[...], gpre_ref[...], scale_ref[...], shift_ref[...]).astype(bf16)
    o_ref[...] = _dot(h, w_ref[...])


def _qkv_sample(x, scale, shift, g_pre, w_in_b):
    n = x.shape[0]
    cols = 9 * QW
    tile = 3 * QW
    first = Q_COL0 // tile
    return pl.pallas_call(
        _qkv_s_kernel,
        grid=(cols // tile,),
        in_specs=[
            pl.BlockSpec((n, D_MODEL), lambda j: (0, 0)),
            pl.BlockSpec((n, D_MODEL), lambda j: (0, 0)),
            pl.BlockSpec((n, D_MODEL), lambda j: (0, 0)),
            pl.BlockSpec((1, D_MODEL), lambda j: (0, 0)),
            pl.BlockSpec((D_MODEL, tile), lambda j: (0, first + j)),
        ],
        out_specs=pl.BlockSpec((n, tile), lambda j: (0, j)),
        out_shape=jax.ShapeDtypeStruct((n, cols), f32),
        name="qkv_s",
        compiler_params=pltpu.CompilerParams(
            dimension_semantics=("arbitrary",), vmem_limit_bytes=VMEM_LIMIT),
    )(x, scale, shift, g_pre, w_in_b)


T_NEW = 8


def _sample_attention(q_refs, kn_refs, vn_refs, ck_refs, cv_refs, o_ref, new_sc):
    n_pat = len(DILATIONS)
    width = q_refs[0].shape[-1]
    pairs = range(width // LANES)
    pats = range(n_pat)
    for g in pats:
        new_sc[0:T_NEW, g * width:(g + 1) * width] = kn_refs[g][0]
        new_sc[0:T_NEW, (n_pat + g) * width:(n_pat + g + 1) * width] = vn_refs[g][0]

    def new_cols(g, hp, base=0):
        return slice((base + g) * width + hp * LANES, (base + g) * width + (hp + 1) * LANES)

    lane8 = lax.broadcasted_iota(jnp.int32, (2 * T_NEW, LANES), 1)
    low = lane8 < HEAD_DIM
    t_idx = lax.broadcasted_iota(jnp.int32, (2 * T_NEW, LANES), 0) % T_NEW

    s_cache, s_new = {}, {}
    for hp in pairs:
        cols = slice(hp * LANES, (hp + 1) * LANES)
        for g in pats:
            q2 = q_refs[g][0, :, cols] * (HEAD_DIM ** -0.5)
            q_st = jnp.concatenate([jnp.where(low[:T_NEW], q2, 0.0), jnp.where(low[:T_NEW], 0.0, q2)],
                                   axis=0).astype(bf16)
            kt = ck_refs[g][0, cols, :].astype(bf16)
            k_new = new_sc[:, new_cols(g, hp)].astype(bf16)
            s_cache[hp, g] = _dot(q_st, kt)
            s_new[hp, g] = _dot_nt(q_st, k_new)

    probs = {}
    for hp in pairs:
        m = jnp.full((2 * T_NEW, 1), NEG, f32)
        for g, dil in enumerate(DILATIONS):
            rows = WINDOWS[g]
            rho = lax.broadcasted_iota(jnp.int32, (2 * T_NEW, rows), 1)
            tq = lax.broadcasted_iota(jnp.int32, (2 * T_NEW, rows), 0) % T_NEW
            ok_c = jnp.logical_and(rho >= tq, ((rho - tq) & (dil - 1)) == 0)
            ok_n = jnp.logical_and(lane8 <= t_idx, ((t_idx - lane8) & (dil - 1)) == 0)
            s_cache[hp, g] = jnp.where(ok_c, s_cache[hp, g], NEG)
            s_new[hp, g] = jnp.where(ok_n, s_new[hp, g], NEG)
            m = jnp.maximum(m, jnp.maximum(jnp.max(s_cache[hp, g], axis=-1, keepdims=True),
                                           jnp.max(s_new[hp, g], axis=-1, keepdims=True)))
        l = jnp.zeros((2 * T_NEW, 1), f32)
        for g in pats:
            p_c = jnp.exp(s_cache[hp, g] - m)
            p_n = jnp.exp(s_new[hp, g] - m)
            l = l + jnp.sum(p_c, axis=-1, keepdims=True) + jnp.sum(p_n, axis=-1, keepdims=True)
            probs[hp, g] = (p_c.astype(bf16), p_n.astype(bf16))
        probs[hp] = l

    def weighted_values():
        o_parts = []
        for hp in pairs:
            acc = jnp.zeros((2 * T_NEW, LANES), f32)
            for g in pats:
                vt = cv_refs[g][0, hp * LANES:(hp + 1) * LANES, :].astype(bf16)
                v_new = new_sc[:, new_cols(g, hp, n_pat)].astype(bf16)
                p_c, p_n = probs[hp, g]
                acc = acc + _dot_nt(p_c, vt) + _dot(p_n, v_new)
            out = acc / probs[hp]
            o_parts.append(jnp.where(low[:T_NEW], out[:T_NEW], out[T_NEW:]))
        o_ref[0] = jnp.concatenate(o_parts, axis=1)

    return weighted_values


TILE = 256


def _expand_heads(w):
    r = lax.broadcasted_iota(jnp.int32, (2 * LANES, QW), 0) % LANES
    head = lax.broadcasted_iota(jnp.int32, (2 * LANES, QW), 1) // HEAD_DIM
    expand = (r == head // 2 + HEAD_DIM * (1 - head % 2)).astype(bf16)
    hi = w.astype(bf16)
    lo = (w - hi.astype(f32)).astype(bf16)
    return _dot(jnp.concatenate([hi, lo], axis=1), expand)


def _rest_kernel(*refs, sample):
    if sample:
        (x_ref, scale_ref, shift_ref, gate_ref, gpre_ref, gpost_ref, lng_ref, lnb_ref,
         wa_ref, wzb_ref, wg_ref, wsp_ref, bsp_ref, wpa_ref, wpb_ref, wout_ref,
         attn_ref, y_ref, vn_ref) = refs
        x = x_ref[...]
        scale, shift, gate = scale_ref[...], shift_ref[...], gate_ref[...]
    else:
        (x_ref, scale_ref, shift_ref, gate_ref, gpre_ref, gpost_ref, lng_ref, lnb_ref,
         wa_ref, wzb_ref, wg_ref, wsp_ref, bsp_ref, wpa_ref, wpb_ref, wout_ref,
         o1_ref, m1_ref, l1_ref, o4_ref, m4_ref, l4_ref, o16_ref, m16_ref, l16_ref) = refs[:25]
        n_pat = len(DILATIONS)
        sample_refs = [refs[25 + n_pat * i:25 + n_pat * (i + 1)] for i in range(5)]
        (y_ref, attn_s_ref, o4_sc, m4_sc, l4_sc, o16_sc, m16_sc, l16_sc, new_sc) = refs[25 + 5 * n_pat:]
        x = x_ref[0]
        scale, shift, gate = scale_ref[0], shift_ref[0], gate_ref[0]

        @pl.when(jnp.logical_and(pl.program_id(0) == 0, pl.program_id(1) == 0))
        def _():
            new_sc[...] = jnp.zeros_like(new_sc)

        sample_values = _sample_attention(*sample_refs, attn_s_ref, new_sc)

    rows = x.shape[0]
    h = _norm_mod(x, gpre_ref[...], scale, shift).astype(bf16)

    pa = _dot(h, wa_ref[...])
    u_a, v_a, z_a = pa[:, :D_MODEL], pa[:, D_MODEL:2 * D_MODEL], pa[:, 2 * D_MODEL:]
    mu = jnp.mean(v_a, axis=-1, keepdims=True)
    cen = v_a - mu
    var = jnp.mean(cen * cen, axis=-1, keepdims=True)
    v_n = cen * lax.rsqrt(var + EPS) * lng_ref[...] + lnb_ref[...]
    v_nb = v_n.astype(bf16)
    if sample:
        vn_ref[...] = v_n
        pick = (lax.broadcasted_iota(jnp.int32, (rows, CHUNK), 1)
                == lax.broadcasted_iota(jnp.int32, (rows, CHUNK), 0) % T_NEW).astype(bf16)
        same_seq = (lax.broadcasted_iota(jnp.int32, (rows, rows), 0) // T_NEW
                    == lax.broadcasted_iota(jnp.int32, (rows, rows), 1) // T_NEW)
        cols = []
        for g in range(A_GROUPS):
            tiled = _dot_nt(_dot(pick, wsp_ref[g]).astype(bf16), pick)
            w_blk = jnp.where(same_seq, tiled, 0.0).astype(bf16)
            cols.append(_dot(w_blk, v_nb[:, g * LANES:(g + 1) * LANES]) + bsp_ref[:, g:g + 1])
        zs = jnp.concatenate(cols, axis=1)
    else:
        n_ck = rows // CHUNK
        per_group = []
        for g in range(A_GROUPS):
            rhs = jnp.concatenate(
                [v_nb[ck * CHUNK:(ck + 1) * CHUNK, g * LANES:(g + 1) * LANES] for ck in range(n_ck)], axis=1)
            per_group.append(_dot(wsp_ref[g], rhs) + bsp_ref[:, g:g + 1])
        zs = jnp.concatenate(
            [jnp.concatenate([pg[:, ck * LANES:(ck + 1) * LANES] for pg in per_group], axis=1)
             for ck in range(n_ck)], axis=0)
    y_a = u_a * zs * _silu(z_a)

    if sample:
        attn = attn_ref[...]
    else:
        for osc, msc, lsc, oref, mref, lref, dil in (
                (o4_sc, m4_sc, l4_sc, o4_ref, m4_ref, l4_ref, DILATIONS[1]),
                (o16_sc, m16_sc, l16_sc, o16_ref, m16_ref, l16_ref, DILATIONS[2])):
            n = rows // dil
            for r in range(dil):
                o_r = oref[0, r].astype(f32)
                for k in range(QW // LANES):
                    osc[k, pl.ds(r, n, stride=dil), :] = o_r[:, k * LANES:(k + 1) * LANES]
                msc[pl.ds(r, n, stride=dil), :] = mref[0, r]
                lsc[pl.ds(r, n, stride=dil), :] = lref[0, r]
        ms = (m1_ref[0], m4_sc[...], m16_sc[...])
        ls = (l1_ref[0], l4_sc[...], l16_sc[...])
        unchunk = lambda sc: jnp.concatenate([sc[k] for k in range(QW // LANES)], axis=1)
        outs = (o1_ref[0].astype(f32), unchunk(o4_sc), unchunk(o16_sc))
        m_all = jnp.maximum(jnp.maximum(ms[0], ms[1]), ms[2])
        ws = [jnp.exp(m - m_all) for m in ms]
        den = ws[0] * ls[0] + ws[1] * ls[1] + ws[2] * ls[2]
        attn = sum(_expand_heads(w / den) * o for w, o in zip(ws, outs))

    z_b = _dot(h, wzb_ref[...])
    y_b = (attn * _silu(z_b)).astype(bf16)

    gl = _dot(h, wg_ref[...])
    if not sample:
        sample_values()
    p_a = _dot(y_a.astype(bf16), wpa_ref[...])
    p_b = _dot(y_b, wpb_ref[...])
    merged = jax.nn.sigmoid(gl[:, :D_MODEL]) * p_a + jax.nn.sigmoid(gl[:, D_MODEL:]) * p_b
    out = _dot(merged.astype(bf16), wout_ref[...])
    normed = out * lax.rsqrt(jnp.mean(out * out, axis=-1, keepdims=True) + EPS) * gpost_ref[...]
    y = x + gate * normed
    if sample:
        y_ref[...] = y
    else:
        y_ref[0] = y


def _full(shape):
    nd = len(shape)
    return pl.BlockSpec(shape, lambda *_: (0,) * nd)


def _weight_specs(weights):
    in_proj = [pl.BlockSpec((D_MODEL, 3 * D_MODEL), lambda *_: (0, 0)),
               pl.BlockSpec((D_MODEL, QW), lambda *_: (0, ZB_COL0 // QW)),
               pl.BlockSpec((D_MODEL, 2 * D_MODEL), lambda *_: (0, GATE_COL0 // (2 * D_MODEL)))]
    return in_proj + [_full(w.shape) for w in weights[3:]]


def _rest_prompt(x, scale, shift, gate, vecs, weights, attn_parts, qkv_s, caches):
    b = x.shape[0]
    tiles = SEQ // TILE
    n_seq = qkv_s.shape[0]
    parts, rem = divmod(b * tiles, n_seq)
    n_pairs = N_HEADS // 2
    assert rem == 0 and tiles % parts == 0 and n_pairs % parts == 0, "sample work must tile the grid"
    width = (n_pairs // parts) * LANES
    seq = lambda i, c: i * (tiles // parts) + c // parts

    def lanes_at(first):
        return lambda i, c: (seq(i, c), 0, first // width + c % parts)

    def rows_at(first):
        return lambda i, c: (seq(i, c), first // width + c % parts, 0)
    per_b = lambda i, c: (i, 0, 0)
    tile3 = lambda i, c: (i, c, 0)
    tile4 = lambda i, c: (i, 0, c, 0)
    part_args, part_specs, scratch = [], [], []
    for (o, m, l), dil in zip(attn_parts, DILATIONS):
        for a, lanes in ((o, QW), (m, LANES), (l, LANES)):
            if dil == 1:
                part_args.append(a)
                part_specs.append(pl.BlockSpec((1, TILE, lanes), tile3))
            else:
                part_args.append(a.reshape(b, dil, SEQ // dil, lanes))
                part_specs.append(pl.BlockSpec((1, dil, TILE // dil, lanes), tile4))
        if dil != 1:
            scratch += [pltpu.VMEM((QW // LANES, TILE, LANES), f32),
                        pltpu.VMEM((TILE, LANES), f32), pltpu.VMEM((TILE, LANES), f32)]
    in_specs = (
        [pl.BlockSpec((1, TILE, D_MODEL), tile3),
         pl.BlockSpec((1, 1, D_MODEL), per_b), pl.BlockSpec((1, 1, D_MODEL), per_b),
         pl.BlockSpec((1, 1, D_MODEL), per_b)]
        + [_full(v.shape) for v in vecs] + _weight_specs(weights) + part_specs
        + [pl.BlockSpec((1, T_NEW, width), lanes_at(first + g * QW))
           for first in (0, 3 * QW, 6 * QW) for g in range(3)]
        + [pl.BlockSpec((1, width, win), rows_at(first)) for first in (0, QW) for win in WINDOWS])
    n_pat = len(DILATIONS)
    scratch.append(pltpu.VMEM((LANES, 2 * n_pat * width), f32))
    sample_args = [qkv_s] * (3 * n_pat) + list(caches) + list(caches)
    return pl.pallas_call(
        functools.partial(_rest_kernel, sample=False),
        grid=(b, tiles),
        in_specs=in_specs,
        out_specs=[pl.BlockSpec((1, TILE, D_MODEL), tile3),
                   pl.BlockSpec((1, T_NEW, width), lanes_at(0))],
        out_shape=[jax.ShapeDtypeStruct((b, SEQ, D_MODEL), f32),
                   jax.ShapeDtypeStruct((n_seq, T_NEW, QW), f32)],
        scratch_shapes=scratch,
        name="rest_p",
        compiler_params=pltpu.CompilerParams(
            dimension_semantics=("arbitrary", "arbitrary"), vmem_limit_bytes=VMEM_LIMIT),
    )(x, scale, shift, gate, *vecs, *weights, *part_args, *sample_args)


def _rest_sample(x, scale, shift, gate, vecs, weights, attn):
    n = x.shape[0]
    args = (x, scale, shift, gate, *vecs, *weights, attn)
    in_specs = ([_full(a.shape) for a in (x, scale, shift, gate, *vecs)] + _weight_specs(weights)
                + [_full(attn.shape)])
    return pl.pallas_call(
        functools.partial(_rest_kernel, sample=True),
        in_specs=in_specs,
        out_specs=[_full((n, D_MODEL)), _full((n, D_MODEL))],
        out_shape=[jax.ShapeDtypeStruct((n, D_MODEL), f32), jax.ShapeDtypeStruct((n, D_MODEL), f32)],
        grid=(1,),
        name="rest_s",
        compiler_params=pltpu.CompilerParams(
            dimension_semantics=("arbitrary",), vmem_limit_bytes=VMEM_LIMIT),
    )(*args)


def _kv_rows(kv, batch, rows):
    return kv.reshape(1, batch, rows, 2, N_HEADS, HEAD_DIM)


def kernel(x_prompt, x_sample, cache_kv_w128, cache_kv_w512, cache_kv_w2048, c_prompt, c_sample, w_cond, b_cond, g_pre, w_in, ln_v_g, ln_v_b, w_spatial, b_spatial, w_proj_a, w_proj_b, w_out, g_post):
    assert w_in.shape[0] == 1, "single layer"
    bp, seq, _ = x_prompt.shape
    bs, t_new, _ = x_sample.shape
    assert seq == SEQ and t_new == T_NEW

    w_in_b = w_in[0].astype(bf16)
    w_kvt = jnp.transpose(w_in[0][:, K_COL0:ZB_COL0].reshape(D_MODEL, 2, 3, QW), (2, 1, 3, 0))
    w_kvt = w_kvt.reshape(3, 2 * QW, D_MODEL).astype(bf16)
    causal = jnp.tril(jnp.ones((CHUNK, CHUNK), bool))
    w_sp = jnp.where(causal[None], w_spatial[0], 0.0)
    w_sp_p = w_sp.astype(bf16)
    n_s = bs * T_NEW
    b_sp_p = b_spatial[0].T
    b_sp_s = jnp.tile(b_spatial[0][:, :T_NEW].T, (bs, 1))
    weights_tail = (w_proj_a[0].astype(bf16), w_proj_b[0].astype(bf16), w_out[0].astype(bf16))
    vecs = (g_pre, g_post, ln_v_g, ln_v_b)

    mod = _cond(jnp.concatenate([c_prompt, c_sample], axis=0), w_cond[0], b_cond)
    shift, scale, gate = mod[:, :D_MODEL], mod[:, D_MODEL:2 * D_MODEL], mod[:, 2 * D_MODEL:]
    mp = lambda a: a[:bp].reshape(bp, 1, D_MODEL)
    ms = lambda a: jnp.repeat(a[bp:], T_NEW, axis=0)

    xs = x_sample.reshape(n_s, D_MODEL)
    qkv_s = _qkv_sample(xs, ms(scale), ms(shift), g_pre, w_in_b)
    caches_t = []
    for cache, win in zip((cache_kv_w128, cache_kv_w512, cache_kv_w2048), WINDOWS):
        caches_t.append(jnp.transpose(cache[0], (0, 2, 3, 4, 1)).reshape(bs, 2 * QW, win))

    res = [_attn_prompt(x_prompt, mp(scale), mp(shift), g_pre, w_in_b, w_kvt, g) for g in range(3)]
    in_proj = (w_in_b, w_in_b, w_in_b)
    y_p, attn_s = _rest_prompt(x_prompt, mp(scale), mp(shift), mp(gate), vecs,
                               in_proj + (w_sp_p, b_sp_p) + weights_tail, [r[:3] for r in res],
                               qkv_s.reshape(bs, T_NEW, 9 * QW), caches_t)
    kv_p = [jnp.transpose(r[3].reshape(bp, 2, N_HEADS, HEAD_DIM, win), (0, 4, 1, 2, 3))[None]
            for r, win in zip(res, WINDOWS)]

    attn_s = attn_s.reshape(n_s, QW)
    y_s, v_n_s = _rest_sample(xs, ms(scale), ms(shift), ms(gate), vecs,
                              in_proj + (w_sp_p, b_sp_s) + weights_tail, attn_s)
    kv_s = []
    for g in range(3):
        kq = qkv_s[:, 3 * QW + g * QW:3 * QW + (g + 1) * QW]
        vq = qkv_s[:, 6 * QW + g * QW:6 * QW + (g + 1) * QW]
        kv_s.append(_kv_rows(jnp.concatenate([kq, vq], axis=1), bs, T_NEW))

    return (y_p, y_s.reshape(bs, T_NEW, D_MODEL), kv_p[0], kv_p[1], kv_p[2],
            kv_s[0], kv_s[1], kv_s[2], v_n_s.reshape(1, bs, T_NEW, D_MODEL))
```

```python
import functools

import jax
import jax.numpy as jnp
from jax import lax
from jax.experimental import pallas as pl
from jax.experimental.pallas import tpu as pltpu

D_MODEL = 1024
SEQ = 2048
HEAD_DIM = 64
N_HEADS = 8
QW = N_HEADS * HEAD_DIM
WINDOWS = (128, 512, 2048)
DILATIONS = (1, 4, 16)
N_BACK = 128
CHUNK = 128
A_GROUPS = 8
EPS = 1e-6
NEG = -1e30
LANES = 128
VMEM_LIMIT = 56 * 1024 * 1024
Q_COL0 = 3 * D_MODEL
K_COL0 = Q_COL0 + 3 * QW
V_COL0 = K_COL0 + 3 * QW
ZB_COL0 = V_COL0 + 3 * QW
GATE_COL0 = ZB_COL0 + QW

f32 = jnp.float32
bf16 = jnp.bfloat16


def _silu(x):
    return x * jax.nn.sigmoid(x)


def _norm_mod(x, g_pre, scale, shift):
    y = x * lax.rsqrt(jnp.mean(x * x, axis=-1, keepdims=True) + EPS) * g_pre
    return y * (1.0 + scale) + shift


def _dot(a, b):
    return jnp.dot(a, b, preferred_element_type=f32)


def _dot_nt(a, b):
    return lax.dot_general(a, b, (((1,), (1,)), ((), ())), preferred_element_type=f32)


def _cond_kernel(c_ref, w_ref, b_ref, o_ref):
    o_ref[...] = _dot(_silu(c_ref[...]).astype(bf16), w_ref[...].astype(bf16)) + b_ref[...]


def _cond(c_all, w_cond, b_cond):
    n = c_all.shape[0]
    return pl.pallas_call(
        _cond_kernel,
        out_shape=jax.ShapeDtypeStruct((n, 3 * D_MODEL), f32),
        name="cond",
        compiler_params=pltpu.CompilerParams(vmem_limit_bytes=VMEM_LIMIT),
    )(c_all, w_cond, b_cond)


ROWS_PER_STEP = 512
Q_BLOCK = 128
NORM_ROWS = 256


def _attn_step(c, last_step, wq_ref, wk_ref, wv_ref, wkvt_ref, o_ref, m_ref, l_ref, kvt_ref,
               h_sc, k_sc, ve_sc, vo_sc, *, dil, window):
    seg = SEQ // dil
    n_lc = D_MODEL // LANES

    def rows_of(start, n):
        idx = pl.ds(pl.multiple_of(start, n), n) if dil == 1 else pl.ds(start, n, stride=dil)
        return jnp.concatenate([h_sc[k, idx, :] for k in range(n_lc)], axis=1)

    if seg >= ROWS_PER_STEP:
        per = seg // ROWS_PER_STEP
        h = rows_of((c // per) + (c % per) * ROWS_PER_STEP * dil, ROWS_PER_STEP)
    else:
        per = ROWS_PER_STEP // seg
        h = jnp.concatenate([rows_of(per * c + i, seg) for i in range(per)], axis=0)
    h = h.astype(bf16)
    base = pl.multiple_of(c * ROWS_PER_STEP, ROWS_PER_STEP)

    def tokens_t(start, n):
        return jnp.concatenate([h_sc[k, pl.ds(start, n), :] for k in range(n_lc)], axis=1).astype(bf16)

    if window == SEQ:
        kvt_ref[0] = _dot_nt(wkvt_ref[0], tokens_t(base, ROWS_PER_STEP))
    else:
        @pl.when(c == last_step)
        def _():
            kvt_ref[0] = _dot_nt(wkvt_ref[0], tokens_t(SEQ - window, window))

    low_w = (lax.broadcasted_iota(jnp.int32, (ROWS_PER_STEP, QW), 1) % LANES) < HEAD_DIM
    v = _dot(h, wv_ref[...])
    new_rows = pl.ds(Q_BLOCK + base, ROWS_PER_STEP)
    k_sc[new_rows, :] = _dot(h, wk_ref[...]).astype(bf16)
    ve_sc[new_rows, :] = jnp.where(low_w, v, 1.0).astype(bf16)
    vo_sc[new_rows, :] = jnp.where(low_w, 1.0, v).astype(bf16)
    q = _dot(h, wq_ref[...]) * (HEAD_DIM ** -0.5)
    q_even = jnp.where(low_w, q, 0.0).astype(bf16)
    q_odd = jnp.where(low_w, 0.0, q).astype(bf16)

    n_blocks = ROWS_PER_STEP // Q_BLOCK
    pairs = N_HEADS // 2
    qi = lax.broadcasted_iota(jnp.int32, (2 * Q_BLOCK, 2 * Q_BLOCK), 0) % Q_BLOCK
    kj = lax.broadcasted_iota(jnp.int32, (2 * Q_BLOCK, 2 * Q_BLOCK), 1)
    lane = lax.broadcasted_iota(jnp.int32, (Q_BLOCK, LANES), 1)
    low = lane < HEAD_DIM

    def has_prev(j):
        return seg > ROWS_PER_STEP or (j * Q_BLOCK) % seg != 0

    def key_rows(j):
        r0 = pl.multiple_of(base + j * Q_BLOCK, Q_BLOCK)
        return pl.ds(r0, 2 * Q_BLOCK) if has_prev(j) else pl.ds(r0 + Q_BLOCK, Q_BLOCK)

    def mask_of(j):
        if not has_prev(j):
            own = (2 * Q_BLOCK, Q_BLOCK)
            return (lax.broadcasted_iota(jnp.int32, own, 1)
                    <= lax.broadcasted_iota(jnp.int32, own, 0) % Q_BLOCK)
        off = jnp.where((base % seg) != 0, 0, Q_BLOCK) if (seg > ROWS_PER_STEP and j == 0) else 0
        in_prev = jnp.logical_and(kj < Q_BLOCK, kj >= qi + off)
        in_cur = jnp.logical_and(kj >= Q_BLOCK, kj - Q_BLOCK <= qi)
        return jnp.logical_or(in_prev, in_cur)

    def qk(j):
        rows = slice(j * Q_BLOCK, (j + 1) * Q_BLOCK)
        keys = key_rows(j)
        out = []
        for hp in range(pairs):
            cols = slice(hp * LANES, (hp + 1) * LANES)
            q_st = jnp.concatenate([q_even[rows, cols], q_odd[rows, cols]], axis=0)
            out.append(_dot_nt(q_st, k_sc[keys, cols]))
        return out

    def finish(j, scores):
        rows = slice(j * Q_BLOCK, (j + 1) * Q_BLOCK)
        keys = key_rows(j)
        mask = mask_of(j)
        ps, ms = [], []
        for hp in range(pairs):
            s = jnp.where(mask, scores[hp], NEG)
            for half in (s[:Q_BLOCK], s[Q_BLOCK:]):
                m = jnp.max(half, axis=-1, keepdims=True)
                ps.append(jnp.exp(half - m).astype(bf16))
                ms.append(m)
        o_parts = []
        m_tile = jnp.zeros((Q_BLOCK, LANES), f32)
        l_tile = jnp.ones((Q_BLOCK, LANES), f32)
        for hp in range(pairs):
            cols = slice(hp * LANES, (hp + 1) * LANES)
            acc_e = _dot(ps[2 * hp], ve_sc[keys, cols])
            acc_o = _dot(ps[2 * hp + 1], vo_sc[keys, cols])
            o_parts.append(jnp.where(low, acc_e, acc_o))
            l_tile = jnp.where(lane == HEAD_DIM + hp, acc_e, jnp.where(lane == hp, acc_o, l_tile))
            m_tile = jnp.where(lane == HEAD_DIM + hp, ms[2 * hp],
                               jnp.where(lane == hp, ms[2 * hp + 1], m_tile))
        o_ref[0, rows, :] = jnp.concatenate(o_parts, axis=1).astype(bf16)
        m_ref[0, rows, :] = m_tile
        l_ref[0, rows, :] = l_tile

    scores = qk(0)
    for j in range(n_blocks):
        nxt = qk(j + 1) if j + 1 < n_blocks else None
        finish(j, scores)
        scores = nxt


def _attn_kernel(x_ref, scale_ref, shift_ref, gpre_ref, wq_ref, wk_ref, wv_ref, wkvt_ref,
                 o_ref, m_ref, l_ref, kvt0_ref, kvt1_ref, kvt2_ref, h_sc, k_sc, ve_sc, vo_sc):
    g = pl.program_id(1)
    c = pl.program_id(2)
    last_step = pl.num_programs(2) - 1

    @pl.when(jnp.logical_and(g == 0, c == 0))
    def _():
        zeros = jnp.zeros((Q_BLOCK, QW), bf16)
        k_sc[0:Q_BLOCK, :] = zeros
        ve_sc[0:Q_BLOCK, :] = zeros
        vo_sc[0:Q_BLOCK, :] = zeros
        for rb in range(SEQ // NORM_ROWS):
            rs = slice(rb * NORM_ROWS, (rb + 1) * NORM_ROWS)
            hb = _norm_mod(x_ref[0, rs, :], gpre_ref[...], scale_ref[0], shift_ref[0])
            for k in range(D_MODEL // LANES):
                h_sc[k, rs, :] = hb[:, k * LANES:(k + 1) * LANES]

    for pattern, kvt_ref in enumerate((kvt0_ref, kvt1_ref, kvt2_ref)):
        @pl.when(g == pattern)
        def _(pattern=pattern, kvt_ref=kvt_ref):
            _attn_step(c, last_step, wq_ref, wk_ref, wv_ref, wkvt_ref, o_ref, m_ref, l_ref, kvt_ref,
                       h_sc, k_sc, ve_sc, vo_sc, dil=DILATIONS[pattern], window=WINDOWS[pattern])


def _attn_prompt(x, scale, shift, g_pre, w_in_b, w_kvt):
    b = x.shape[0]
    n_pat = len(DILATIONS)
    steps = SEQ // ROWS_PER_STEP
    full_window = WINDOWS.index(SEQ)
    per_b = lambda i, g, c: (i, 0, 0)
    step_rows = lambda i, g, c: (g, i, c, 0)

    def kvt_spec(pattern):
        if WINDOWS[pattern] == SEQ:
            return pl.BlockSpec((1, 2 * QW, ROWS_PER_STEP),
                                lambda i, g, c: (i, 0, jnp.where(g == full_window, c, 0)))
        return pl.BlockSpec((1, 2 * QW, WINDOWS[pattern]), per_b)

    return pl.pallas_call(
        _attn_kernel,
        grid=(b, n_pat, steps),
        in_specs=[
            pl.BlockSpec((1, SEQ, D_MODEL), per_b),
            pl.BlockSpec((1, 1, D_MODEL), per_b),
            pl.BlockSpec((1, 1, D_MODEL), per_b),
            pl.BlockSpec((1, D_MODEL), lambda i, g, c: (0, 0)),
            pl.BlockSpec((D_MODEL, QW), lambda i, g, c: (0, Q_COL0 // QW + g)),
            pl.BlockSpec((D_MODEL, QW), lambda i, g, c: (0, K_COL0 // QW + g)),
            pl.BlockSpec((D_MODEL, QW), lambda i, g, c: (0, V_COL0 // QW + g)),
            pl.BlockSpec((1, 2 * QW, D_MODEL), lambda i, g, c: (g, 0, 0)),
        ],
        out_specs=[
            pl.BlockSpec((None, 1, ROWS_PER_STEP, QW), step_rows),
            pl.BlockSpec((None, 1, ROWS_PER_STEP, LANES), step_rows),
            pl.BlockSpec((None, 1, ROWS_PER_STEP, LANES), step_rows),
        ] + [kvt_spec(p) for p in range(n_pat)],
        out_shape=[
            jax.ShapeDtypeStruct((n_pat, b, SEQ, QW), bf16),
            jax.ShapeDtypeStruct((n_pat, b, SEQ, LANES), f32),
            jax.ShapeDtypeStruct((n_pat, b, SEQ, LANES), f32),
        ] + [jax.ShapeDtypeStruct((b, 2 * QW, win), f32) for win in WINDOWS],
        scratch_shapes=[pltpu.VMEM((D_MODEL // LANES, SEQ, LANES), f32),
                        pltpu.VMEM((SEQ + Q_BLOCK, QW), bf16), pltpu.VMEM((SEQ + Q_BLOCK, QW), bf16),
                        pltpu.VMEM((SEQ + Q_BLOCK, QW), bf16)],
        name="attn_p",
        compiler_params=pltpu.CompilerParams(
            dimension_semantics=("arbitrary", "arbitrary", "arbitrary"), vmem_limit_bytes=VMEM_LIMIT),
    )(x, scale, shift, g_pre, w_in_b, w_in_b, w_in_b, w_kvt)


def _qkv_s_kernel(x_ref, scale_ref, shift_ref, gpre_ref, w_ref, o_ref):
    h = _norm_mod(x_ref[...], gpre_ref[...], scale_ref[...], shift_ref[...]).astype(bf16)
    o_ref[...] = _dot(h, w_ref[...])


def _qkv_sample(x, scale, shift, g_pre, w_in_b):
    n = x.shape[0]
    cols = 9 * QW
    tile = 3 * QW
    first = Q_COL0 // tile
    return pl.pallas_call(
        _qkv_s_kernel,
        grid=(cols // tile,),
        in_specs=[
            pl.BlockSpec((n, D_MODEL), lambda j: (0, 0)),
            pl.BlockSpec((n, D_MODEL), lambda j: (0, 0)),
            pl.BlockSpec((n, D_MODEL), lambda j: (0, 0)),
            pl.BlockSpec((1, D_MODEL), lambda j: (0, 0)),
            pl.BlockSpec((D_MODEL, tile), lambda j: (0, first + j)),
        ],
        out_specs=pl.BlockSpec((n, tile), lambda j: (0, j)),
        out_shape=jax.ShapeDtypeStruct((n, cols), f32),
        name="qkv_s",
        compiler_params=pltpu.CompilerParams(
            dimension_semantics=("arbitrary",), vmem_limit_bytes=VMEM_LIMIT),
    )(x, scale, shift, g_pre, w_in_b)


T_NEW = 8


def _sample_attention(qkv_ref, c0_ref, c1_ref, c2_ref, o_ref, new_sc):
    qkv = qkv_ref[0]
    new_sc[0:T_NEW, :] = qkv[:, 3 * QW:]
    q_all = qkv[:, :3 * QW] * (HEAD_DIM ** -0.5)

    lane8 = lax.broadcasted_iota(jnp.int32, (2 * T_NEW, LANES), 1)
    low = lane8 < HEAD_DIM
    t_idx = lax.broadcasted_iota(jnp.int32, (2 * T_NEW, LANES), 0) % T_NEW
    caches = (c0_ref, c1_ref, c2_ref)
    pairs = range(N_HEADS // 2)
    pats = range(len(DILATIONS))

    def pair_cols(g, hp, base=0):
        return slice(base + g * QW + hp * LANES, base + g * QW + (hp + 1) * LANES)

    s_cache, s_new = {}, {}
    for hp in pairs:
        for g in pats:
            q2 = q_all[:, pair_cols(g, hp)]
            q_st = jnp.concatenate([jnp.where(low[:T_NEW], q2, 0.0), jnp.where(low[:T_NEW], 0.0, q2)],
                                   axis=0).astype(bf16)
            kt = caches[g][0, hp * LANES:(hp + 1) * LANES, :].astype(bf16)
            k_new = new_sc[:, pair_cols(g, hp)].astype(bf16)
            s_cache[hp, g] = _dot(q_st, kt)
            s_new[hp, g] = _dot_nt(q_st, k_new)

    probs = {}
    for hp in pairs:
        m = jnp.full((2 * T_NEW, 1), NEG, f32)
        for g, dil in enumerate(DILATIONS):
            rows = WINDOWS[g]
            rho = lax.broadcasted_iota(jnp.int32, (2 * T_NEW, rows), 1)
            tq = lax.broadcasted_iota(jnp.int32, (2 * T_NEW, rows), 0) % T_NEW
            ok_c = jnp.logical_and(rho >= tq, ((rho - tq) & (dil - 1)) == 0)
            ok_n = jnp.logical_and(lane8 <= t_idx, ((t_idx - lane8) & (dil - 1)) == 0)
            s_cache[hp, g] = jnp.where(ok_c, s_cache[hp, g], NEG)
            s_new[hp, g] = jnp.where(ok_n, s_new[hp, g], NEG)
            m = jnp.maximum(m, jnp.maximum(jnp.max(s_cache[hp, g], axis=-1, keepdims=True),
                                           jnp.max(s_new[hp, g], axis=-1, keepdims=True)))
        l = jnp.zeros((2 * T_NEW, 1), f32)
        for g in pats:
            p_c = jnp.exp(s_cache[hp, g] - m)
            p_n = jnp.exp(s_new[hp, g] - m)
            l = l + jnp.sum(p_c, axis=-1, keepdims=True) + jnp.sum(p_n, axis=-1, keepdims=True)
            probs[hp, g] = (p_c.astype(bf16), p_n.astype(bf16))
        probs[hp] = l

    o_parts = []
    for hp in pairs:
        acc = jnp.zeros((2 * T_NEW, LANES), f32)
        for g in pats:
            vt = caches[g][0, QW + hp * LANES: QW + (hp + 1) * LANES, :].astype(bf16)
            v_new = new_sc[:, pair_cols(g, hp, 3 * QW)].astype(bf16)
            p_c, p_n = probs[hp, g]
            acc = acc + _dot_nt(p_c, vt) + _dot(p_n, v_new)
        out = acc / probs[hp]
        o_parts.append(jnp.where(low[:T_NEW], out[:T_NEW], out[T_NEW:]))
    o_ref[0] = jnp.concatenate(o_parts, axis=1)


TILE = 256


def _expand_heads(w):
    r = lax.broadcasted_iota(jnp.int32, (2 * LANES, QW), 0) % LANES
    head = lax.broadcasted_iota(jnp.int32, (2 * LANES, QW), 1) // HEAD_DIM
    expand = (r == head // 2 + HEAD_DIM * (1 - head % 2)).astype(bf16)
    hi = w.astype(bf16)
    lo = (w - hi.astype(f32)).astype(bf16)
    return _dot(jnp.concatenate([hi, lo], axis=1), expand)


def _rest_kernel(*refs, sample, sample_every=None):
    if sample:
        (x_ref, scale_ref, shift_ref, gate_ref, gpre_ref, gpost_ref, lng_ref, lnb_ref,
         wa_ref, wzb_ref, wg_ref, wsp_ref, bsp_ref, wpa_ref, wpb_ref, wout_ref,
         attn_ref, y_ref, vn_ref) = refs
        x = x_ref[...]
        scale, shift, gate = scale_ref[...], shift_ref[...], gate_ref[...]
    else:
        (x_ref, scale_ref, shift_ref, gate_ref, gpre_ref, gpost_ref, lng_ref, lnb_ref,
         wa_ref, wzb_ref, wg_ref, wsp_ref, bsp_ref, wpa_ref, wpb_ref, wout_ref,
         o1_ref, m1_ref, l1_ref, o4_ref, m4_ref, l4_ref, o16_ref, m16_ref, l16_ref,
         qkv_s_ref, c0_ref, c1_ref, c2_ref, y_ref, attn_s_ref,
         o4_sc, m4_sc, l4_sc, o16_sc, m16_sc, l16_sc, new_sc) = refs
        x = x_ref[0]
        scale, shift, gate = scale_ref[0], shift_ref[0], gate_ref[0]

        @pl.when(jnp.logical_and(pl.program_id(0) == 0, pl.program_id(1) == 0))
        def _():
            new_sc[...] = jnp.zeros_like(new_sc)

        @pl.when(pl.program_id(1) % sample_every == 0)
        def _():
            _sample_attention(qkv_s_ref, c0_ref, c1_ref, c2_ref, attn_s_ref, new_sc)

    rows = x.shape[0]
    h = _norm_mod(x, gpre_ref[...], scale, shift).astype(bf16)

    pa = _dot(h, wa_ref[...])
    u_a, v_a, z_a = pa[:, :D_MODEL], pa[:, D_MODEL:2 * D_MODEL], pa[:, 2 * D_MODEL:]
    mu = jnp.mean(v_a, axis=-1, keepdims=True)
    cen = v_a - mu
    var = jnp.mean(cen * cen, axis=-1, keepdims=True)
    v_n = cen * lax.rsqrt(var + EPS) * lng_ref[...] + lnb_ref[...]
    v_nb = v_n.astype(bf16)
    if sample:
        vn_ref[...] = v_n
        pick = (lax.broadcasted_iota(jnp.int32, (rows, CHUNK), 1)
                == lax.broadcasted_iota(jnp.int32, (rows, CHUNK), 0) % T_NEW).astype(bf16)
        same_seq = (lax.broadcasted_iota(jnp.int32, (rows, rows), 0) // T_NEW
                    == lax.broadcasted_iota(jnp.int32, (rows, rows), 1) // T_NEW)
        cols = []
        for g in range(A_GROUPS):
            tiled = _dot_nt(_dot(pick, wsp_ref[g]).astype(bf16), pick)
            w_blk = jnp.where(same_seq, tiled, 0.0).astype(bf16)
            cols.append(_dot(w_blk, v_nb[:, g * LANES:(g + 1) * LANES]) + bsp_ref[:, g:g + 1])
        zs = jnp.concatenate(cols, axis=1)
    else:
        n_ck = rows // CHUNK
        per_group = []
        for g in range(A_GROUPS):
            rhs = jnp.concatenate(
                [v_nb[ck * CHUNK:(ck + 1) * CHUNK, g * LANES:(g + 1) * LANES] for ck in range(n_ck)], axis=1)
            per_group.append(_dot(wsp_ref[g], rhs) + bsp_ref[:, g:g + 1])
        zs = jnp.concatenate(
            [jnp.concatenate([pg[:, ck * LANES:(ck + 1) * LANES] for pg in per_group], axis=1)
             for ck in range(n_ck)], axis=0)
    y_a = u_a * zs * _silu(z_a)

    if sample:
        attn = attn_ref[...]
    else:
        for osc, msc, lsc, oref, mref, lref, dil in (
                (o4_sc, m4_sc, l4_sc, o4_ref, m4_ref, l4_ref, DILATIONS[1]),
                (o16_sc, m16_sc, l16_sc, o16_ref, m16_ref, l16_ref, DILATIONS[2])):
            n = rows // dil
            for r in range(dil):
                o_r = oref[0, r].astype(f32)
                for k in range(QW // LANES):
                    osc[k, pl.ds(r, n, stride=dil), :] = o_r[:, k * LANES:(k + 1) * LANES]
                msc[pl.ds(r, n, stride=dil), :] = mref[0, r]
                lsc[pl.ds(r, n, stride=dil), :] = lref[0, r]
        ms = (m1_ref[0], m4_sc[...], m16_sc[...])
        ls = (l1_ref[0], l4_sc[...], l16_sc[...])
        unchunk = lambda sc: jnp.concatenate([sc[k] for k in range(QW // LANES)], axis=1)
        outs = (o1_ref[0].astype(f32), unchunk(o4_sc), unchunk(o16_sc))
        m_all = jnp.maximum(jnp.maximum(ms[0], ms[1]), ms[2])
        ws = [jnp.exp(m - m_all) for m in ms]
        den = ws[0] * ls[0] + ws[1] * ls[1] + ws[2] * ls[2]
        attn = sum(_expand_heads(w / den) * o for w, o in zip(ws, outs))

    z_b = _dot(h, wzb_ref[...])
    y_b = (attn * _silu(z_b)).astype(bf16)

    gl = _dot(h, wg_ref[...])
    p_a = _dot(y_a.astype(bf16), wpa_ref[...])
    p_b = _dot(y_b, wpb_ref[...])
    merged = jax.nn.sigmoid(gl[:, :D_MODEL]) * p_a + jax.nn.sigmoid(gl[:, D_MODEL:]) * p_b
    out = _dot(merged.astype(bf16), wout_ref[...])
    normed = out * lax.rsqrt(jnp.mean(out * out, axis=-1, keepdims=True) + EPS) * gpost_ref[...]
    y = x + gate * normed
    if sample:
        y_ref[...] = y
    else:
        y_ref[0] = y


def _full(shape):
    nd = len(shape)
    return pl.BlockSpec(shape, lambda *_: (0,) * nd)


def _weight_specs(weights):
    in_proj = [pl.BlockSpec((D_MODEL, 3 * D_MODEL), lambda *_: (0, 0)),
               pl.BlockSpec((D_MODEL, QW), lambda *_: (0, ZB_COL0 // QW)),
               pl.BlockSpec((D_MODEL, 2 * D_MODEL), lambda *_: (0, GATE_COL0 // (2 * D_MODEL)))]
    return in_proj + [_full(w.shape) for w in weights[3:]]


def _rest_prompt(x, scale, shift, gate, vecs, weights, attn_parts, qkv_s, caches):
    b = x.shape[0]
    n_pat = len(DILATIONS)
    tiles = SEQ // TILE
    n_seq = qkv_s.shape[0]
    sample_every, rem = divmod(b * tiles, n_seq)
    assert rem == 0 and tiles % sample_every == 0, "sample sequences must tile the grid evenly"
    seq_of = lambda i, c: (i * (tiles // sample_every) + c // sample_every, 0, 0)
    per_b = lambda i, c: (i, 0, 0)
    tile3 = lambda i, c: (i, c, 0)
    part_args, part_specs, scratch = [], [], []
    for pattern, dil in enumerate(DILATIONS):
        for a in attn_parts:
            lanes = a.shape[-1]
            if dil == 1:
                part_args.append(a)
                part_specs.append(pl.BlockSpec((None, 1, TILE, lanes),
                                               lambda i, c, p=pattern: (p, i, c, 0)))
            else:
                part_args.append(a.reshape(n_pat, b, dil, SEQ // dil, lanes))
                part_specs.append(pl.BlockSpec((None, 1, dil, TILE // dil, lanes),
                                               lambda i, c, p=pattern: (p, i, 0, c, 0)))
        if dil != 1:
            scratch += [pltpu.VMEM((QW // LANES, TILE, LANES), f32),
                        pltpu.VMEM((TILE, LANES), f32), pltpu.VMEM((TILE, LANES), f32)]
    in_specs = (
        [pl.BlockSpec((1, TILE, D_MODEL), tile3),
         pl.BlockSpec((1, 1, D_MODEL), per_b), pl.BlockSpec((1, 1, D_MODEL), per_b),
         pl.BlockSpec((1, 1, D_MODEL), per_b)]
        + [_full(v.shape) for v in vecs] + _weight_specs(weights) + part_specs
        + [pl.BlockSpec((1, T_NEW, 9 * QW), seq_of)]
        + [pl.BlockSpec((1, 2 * QW, win), seq_of) for win in WINDOWS])
    scratch.append(pltpu.VMEM((LANES, 6 * QW), f32))
    return pl.pallas_call(
        functools.partial(_rest_kernel, sample=False, sample_every=sample_every),
        grid=(b, tiles),
        in_specs=in_specs,
        out_specs=[pl.BlockSpec((1, TILE, D_MODEL), tile3), pl.BlockSpec((1, T_NEW, QW), seq_of)],
        out_shape=[jax.ShapeDtypeStruct((b, SEQ, D_MODEL), f32),
                   jax.ShapeDtypeStruct((n_seq, T_NEW, QW), f32)],
        scratch_shapes=scratch,
        name="rest_p",
        compiler_params=pltpu.CompilerParams(
            dimension_semantics=("arbitrary", "arbitrary"), vmem_limit_bytes=VMEM_LIMIT),
    )(x, scale, shift, gate, *vecs, *weights, *part_args, qkv_s, *caches)


def _rest_sample(x, scale, shift, gate, vecs, weights, attn):
    n = x.shape[0]
    args = (x, scale, shift, gate, *vecs, *weights, attn)
    in_specs = ([_full(a.shape) for a in (x, scale, shift, gate, *vecs)] + _weight_specs(weights)
                + [_full(attn.shape)])
    return pl.pallas_call(
        functools.partial(_rest_kernel, sample=True),
        in_specs=in_specs,
        out_specs=[_full((n, D_MODEL)), _full((n, D_MODEL))],
        out_shape=[jax.ShapeDtypeStruct((n, D_MODEL), f32), jax.ShapeDtypeStruct((n, D_MODEL), f32)],
        grid=(1,),
        name="rest_s",
        compiler_params=pltpu.CompilerParams(
            dimension_semantics=("arbitrary",), vmem_limit_bytes=VMEM_LIMIT),
    )(*args)


def _kv_rows(kv, batch, rows):
    return kv.reshape(1, batch, rows, 2, N_HEADS, HEAD_DIM)


def kernel(x_prompt, x_sample, cache_kv_w128, cache_kv_w512, cache_kv_w2048, c_prompt, c_sample, w_cond, b_cond, g_pre, w_in, ln_v_g, ln_v_b, w_spatial, b_spatial, w_proj_a, w_proj_b, w_out, g_post):
    assert w_in.shape[0] == 1, "single layer"
    bp, seq, _ = x_prompt.shape
    bs, t_new, _ = x_sample.shape
    assert seq == SEQ and t_new == T_NEW

    w_in_b = w_in[0].astype(bf16)
    w_kvt = jnp.transpose(w_in[0][:, K_COL0:ZB_COL0].reshape(D_MODEL, 2, 3, QW), (2, 1, 3, 0))
    w_kvt = w_kvt.reshape(3, 2 * QW, D_MODEL).astype(bf16)
    causal = jnp.tril(jnp.ones((CHUNK, CHUNK), bool))
    w_sp = jnp.where(causal[None], w_spatial[0], 0.0)
    w_sp_p = w_sp.astype(bf16)
    n_s = bs * T_NEW
    b_sp_p = b_spatial[0].T
    b_sp_s = jnp.tile(b_spatial[0][:, :T_NEW].T, (bs, 1))
    weights_tail = (w_proj_a[0].astype(bf16), w_proj_b[0].astype(bf16), w_out[0].astype(bf16))
    vecs = (g_pre, g_post, ln_v_g, ln_v_b)

    mod = _cond(jnp.concatenate([c_prompt, c_sample], axis=0), w_cond[0], b_cond)
    shift, scale, gate = mod[:, :D_MODEL], mod[:, D_MODEL:2 * D_MODEL], mod[:, 2 * D_MODEL:]
    mp = lambda a: a[:bp].reshape(bp, 1, D_MODEL)
    ms = lambda a: jnp.repeat(a[bp:], T_NEW, axis=0)

    xs = x_sample.reshape(n_s, D_MODEL)
    qkv_s = _qkv_sample(xs, ms(scale), ms(shift), g_pre, w_in_b)
    caches_t = []
    for cache, win in zip((cache_kv_w128, cache_kv_w512, cache_kv_w2048), WINDOWS):
        caches_t.append(jnp.transpose(cache[0], (0, 2, 3, 4, 1)).reshape(bs, 2 * QW, win))

    o_all, m_all, l_all, *kvt = _attn_prompt(x_prompt, mp(scale), mp(shift), g_pre, w_in_b, w_kvt)
    in_proj = (w_in_b, w_in_b, w_in_b)
    y_p, attn_s = _rest_prompt(x_prompt, mp(scale), mp(shift), mp(gate), vecs,
                               in_proj + (w_sp_p, b_sp_p) + weights_tail, (o_all, m_all, l_all),
                               qkv_s.reshape(bs, T_NEW, 9 * QW), caches_t)
    kv_p = [jnp.transpose(a.reshape(bp, 2, N_HEADS, HEAD_DIM, win), (0, 4, 1, 2, 3))[None]
            for a, win in zip(kvt, WINDOWS)]

    attn_s = attn_s.reshape(n_s, QW)
    y_s, v_n_s = _rest_sample(xs, ms(scale), ms(shift), ms(gate), vecs,
                              in_proj + (w_sp_p, b_sp_s) + weights_tail, attn_s)
    kv_s = []
    for g in range(3):
        kq = qkv_s[:, 3 * QW + g * QW:3 * QW + (g + 1) * QW]
        vq = qkv_s[:, 6 * QW + g * QW:6 * QW + (g + 1) * QW]
        kv_s.append(_kv_rows(jnp.concatenate([kq, vq], axis=1), bs, T_NEW))

    return (y_p, y_s.reshape(bs, T_NEW, D_MODEL), kv_p[0], kv_p[1], kv_p[2],
            kv_s[0], kv_s[1], kv_s[2], v_n_s.reshape(1, bs, T_NEW, D_MODEL))
```

```python
import functools

import jax
import jax.numpy as jnp
from jax import lax
from jax.experimental import pallas as pl
from jax.experimental.pallas import tpu as pltpu

D_MODEL = 1024
SEQ = 2048
HEAD_DIM = 64
N_HEADS = 8
QW = N_HEADS * HEAD_DIM
WINDOWS = (128, 512, 2048)
DILATIONS = (1, 4, 16)
CHUNK = 128
A_GROUPS = 8
EPS = 1e-6
NEG = -1e30
LANES = 128
VMEM_LIMIT = 56 * 1024 * 1024
Q_COL0 = 3 * D_MODEL
K_COL0 = Q_COL0 + 3 * QW
V_COL0 = K_COL0 + 3 * QW
ZB_COL0 = V_COL0 + 3 * QW
GATE_COL0 = ZB_COL0 + QW
MOD_SHIFT, MOD_SCALE, MOD_GATE = 0, 1, 2

f32 = jnp.float32
bf16 = jnp.bfloat16


def _silu(x):
    return x * jax.nn.sigmoid(x)


def _norm_mod(x, g_pre, scale, shift):
    y = x * lax.rsqrt(jnp.mean(x * x, axis=-1, keepdims=True) + EPS) * g_pre
    return y * (1.0 + scale) + shift


def _dot(a, b):
    return jnp.dot(a, b, preferred_element_type=f32)


def _dot_nt(a, b):
    return lax.dot_general(a, b, (((1,), (1,)), ((), ())), preferred_element_type=f32)


def _cond_kernel(c_ref, w_ref, b_ref, o_ref):
    o_ref[...] = _dot(_silu(c_ref[...]).astype(bf16), w_ref[...].astype(bf16)) + b_ref[...]


def _cond(c_all, w_cond, b_cond):
    n = c_all.shape[0]
    return pl.pallas_call(
        _cond_kernel,
        grid=(3,),
        in_specs=[pl.BlockSpec((n, D_MODEL), lambda j: (0, 0)),
                  pl.BlockSpec((D_MODEL, D_MODEL), lambda j: (0, j)),
                  pl.BlockSpec((1, D_MODEL), lambda j: (0, j))],
        out_specs=pl.BlockSpec((n, D_MODEL), lambda j: (0, j)),
        out_shape=jax.ShapeDtypeStruct((n, 3 * D_MODEL), f32),
        name="cond",
        compiler_params=pltpu.CompilerParams(
            dimension_semantics=("arbitrary",), vmem_limit_bytes=VMEM_LIMIT),
    )(c_all, w_cond, b_cond)


ROWS_PER_STEP = 512
Q_BLOCK = 128
NORM_ROWS = 256


def _attn_step(c, last_step, wq_ref, wk_ref, wv_ref, wkvt_ref, o_ref, m_ref, l_ref, kvt_ref,
               h_sc, k_sc, ve_sc, vo_sc, *, dil, window):
    seg = SEQ // dil
    n_lc = D_MODEL // LANES

    def rows_of(start, n):
        idx = pl.ds(pl.multiple_of(start, n), n) if dil == 1 else pl.ds(start, n, stride=dil)
        return jnp.concatenate([h_sc[k, idx, :] for k in range(n_lc)], axis=1)

    if seg >= ROWS_PER_STEP:
        per = seg // ROWS_PER_STEP
        h = rows_of((c // per) + (c % per) * ROWS_PER_STEP * dil, ROWS_PER_STEP)
    else:
        per = ROWS_PER_STEP // seg
        h = jnp.concatenate([rows_of(per * c + i, seg) for i in range(per)], axis=0)
    h = h.astype(bf16)
    base = pl.multiple_of(c * ROWS_PER_STEP, ROWS_PER_STEP)

    def tokens_t(start, n):
        return jnp.concatenate([h_sc[k, pl.ds(start, n), :] for k in range(n_lc)], axis=1).astype(bf16)

    if window == SEQ:
        kvt_ref[0] = _dot_nt(wkvt_ref[0], tokens_t(base, ROWS_PER_STEP))
    else:
        @pl.when(c == last_step)
        def _():
            kvt_ref[0] = _dot_nt(wkvt_ref[0], tokens_t(SEQ - window, window))

    low_w = (lax.broadcasted_iota(jnp.int32, (ROWS_PER_STEP, QW), 1) % LANES) < HEAD_DIM
    v = _dot(h, wv_ref[...])
    new_rows = pl.ds(Q_BLOCK + base, ROWS_PER_STEP)
    k_sc[new_rows, :] = _dot(h, wk_ref[...]).astype(bf16)
    ve_sc[new_rows, :] = jnp.where(low_w, v, 1.0).astype(bf16)
    vo_sc[new_rows, :] = jnp.where(low_w, 1.0, v).astype(bf16)
    q = _dot(h, wq_ref[...]) * (HEAD_DIM ** -0.5)
    q_even = jnp.where(low_w, q, 0.0).astype(bf16)
    q_odd = jnp.where(low_w, 0.0, q).astype(bf16)

    n_blocks = ROWS_PER_STEP // Q_BLOCK
    pairs = N_HEADS // 2
    qi = lax.broadcasted_iota(jnp.int32, (2 * Q_BLOCK, 2 * Q_BLOCK), 0) % Q_BLOCK
    kj = lax.broadcasted_iota(jnp.int32, (2 * Q_BLOCK, 2 * Q_BLOCK), 1)
    lane = lax.broadcasted_iota(jnp.int32, (Q_BLOCK, LANES), 1)
    low = lane < HEAD_DIM

    def has_prev(j):
        return seg > ROWS_PER_STEP or (j * Q_BLOCK) % seg != 0

    def key_rows(j):
        r0 = pl.multiple_of(base + j * Q_BLOCK, Q_BLOCK)
        return pl.ds(r0, 2 * Q_BLOCK) if has_prev(j) else pl.ds(r0 + Q_BLOCK, Q_BLOCK)

    def mask_of(j):
        if not has_prev(j):
            own = (2 * Q_BLOCK, Q_BLOCK)
            return (lax.broadcasted_iota(jnp.int32, own, 1)
                    <= lax.broadcasted_iota(jnp.int32, own, 0) % Q_BLOCK)
        off = jnp.where((base % seg) != 0, 0, Q_BLOCK) if (seg > ROWS_PER_STEP and j == 0) else 0
        in_prev = jnp.logical_and(kj < Q_BLOCK, kj >= qi + off)
        in_cur = jnp.logical_and(kj >= Q_BLOCK, kj - Q_BLOCK <= qi)
        return jnp.logical_or(in_prev, in_cur)

    def qk(j):
        rows = slice(j * Q_BLOCK, (j + 1) * Q_BLOCK)
        keys = key_rows(j)
        out = []
        for hp in range(pairs):
            cols = slice(hp * LANES, (hp + 1) * LANES)
            q_st = jnp.concatenate([q_even[rows, cols], q_odd[rows, cols]], axis=0)
            out.append(_dot_nt(q_st, k_sc[keys, cols]))
        return out

    def finish(j, scores):
        rows = slice(j * Q_BLOCK, (j + 1) * Q_BLOCK)
        keys = key_rows(j)
        mask = mask_of(j)
        ps, ms = [], []
        for hp in range(pairs):
            s = jnp.where(mask, scores[hp], NEG)
            for half in (s[:Q_BLOCK], s[Q_BLOCK:]):
                m = jnp.max(half, axis=-1, keepdims=True)
                ps.append(jnp.exp(half - m).astype(bf16))
                ms.append(m)
        o_parts = []
        m_tile = jnp.zeros((Q_BLOCK, LANES), f32)
        l_tile = jnp.ones((Q_BLOCK, LANES), f32)
        for hp in range(pairs):
            cols = slice(hp * LANES, (hp + 1) * LANES)
            acc_e = _dot(ps[2 * hp], ve_sc[keys, cols])
            acc_o = _dot(ps[2 * hp + 1], vo_sc[keys, cols])
            o_parts.append(jnp.where(low, acc_e, acc_o))
            l_tile = jnp.where(lane == HEAD_DIM + hp, acc_e, jnp.where(lane == hp, acc_o, l_tile))
            m_tile = jnp.where(lane == HEAD_DIM + hp, ms[2 * hp],
                               jnp.where(lane == hp, ms[2 * hp + 1], m_tile))
        o_ref[0, rows, :] = jnp.concatenate(o_parts, axis=1).astype(bf16)
        m_ref[0, rows, :] = m_tile
        l_ref[0, rows, :] = l_tile

    scores = qk(0)
    for j in range(n_blocks):
        nxt = qk(j + 1) if j + 1 < n_blocks else None
        finish(j, scores)
        scores = nxt


def _attn_kernel(x_ref, scale_ref, shift_ref, gpre_ref, wq_ref, wk_ref, wv_ref, wkvt_ref,
                 o_ref, m_ref, l_ref, kvt0_ref, kvt1_ref, kvt2_ref, h_sc, k_sc, ve_sc, vo_sc):
    g = pl.program_id(1)
    c = pl.program_id(2)
    last_step = pl.num_programs(2) - 1

    @pl.when(jnp.logical_and(g == 0, c == 0))
    def _():
        zeros = jnp.zeros((Q_BLOCK, QW), bf16)
        k_sc[0:Q_BLOCK, :] = zeros
        ve_sc[0:Q_BLOCK, :] = zeros
        vo_sc[0:Q_BLOCK, :] = zeros
        for rb in range(SEQ // NORM_ROWS):
            rs = slice(rb * NORM_ROWS, (rb + 1) * NORM_ROWS)
            hb = _norm_mod(x_ref[0, rs, :], gpre_ref[...], scale_ref[0], shift_ref[0])
            for k in range(D_MODEL // LANES):
                h_sc[k, rs, :] = hb[:, k * LANES:(k + 1) * LANES]

    for pattern, kvt_ref in enumerate((kvt0_ref, kvt1_ref, kvt2_ref)):
        @pl.when(g == pattern)
        def _(pattern=pattern, kvt_ref=kvt_ref):
            _attn_step(c, last_step, wq_ref, wk_ref, wv_ref, wkvt_ref, o_ref, m_ref, l_ref, kvt_ref,
                       h_sc, k_sc, ve_sc, vo_sc, dil=DILATIONS[pattern], window=WINDOWS[pattern])


def _attn_prompt(x, mod, g_pre, w_in_b, w_kvt):
    b = x.shape[0]
    n_pat = len(DILATIONS)
    steps = SEQ // ROWS_PER_STEP
    full_window = WINDOWS.index(SEQ)
    per_b = lambda i, g, c: (i, 0, 0)
    step_rows = lambda i, g, c: (g, i, c, 0)

    def kvt_spec(pattern):
        if WINDOWS[pattern] == SEQ:
            return pl.BlockSpec((1, 2 * QW, ROWS_PER_STEP),
                                lambda i, g, c: (i, 0, jnp.where(g == full_window, c, 0)))
        return pl.BlockSpec((1, 2 * QW, WINDOWS[pattern]), per_b)

    return pl.pallas_call(
        _attn_kernel,
        grid=(b, n_pat, steps),
        in_specs=[
            pl.BlockSpec((1, SEQ, D_MODEL), per_b),
            pl.BlockSpec((1, 1, D_MODEL), lambda i, g, c: (i, 0, MOD_SCALE)),
            pl.BlockSpec((1, 1, D_MODEL), lambda i, g, c: (i, 0, MOD_SHIFT)),
            pl.BlockSpec((1, D_MODEL), lambda i, g, c: (0, 0)),
            pl.BlockSpec((D_MODEL, QW), lambda i, g, c: (0, Q_COL0 // QW + g)),
            pl.BlockSpec((D_MODEL, QW), lambda i, g, c: (0, K_COL0 // QW + g)),
            pl.BlockSpec((D_MODEL, QW), lambda i, g, c: (0, V_COL0 // QW + g)),
            pl.BlockSpec((1, 2 * QW, D_MODEL), lambda i, g, c: (g, 0, 0)),
        ],
        out_specs=[
            pl.BlockSpec((None, 1, ROWS_PER_STEP, QW), step_rows),
            pl.BlockSpec((None, 1, ROWS_PER_STEP, LANES), step_rows),
            pl.BlockSpec((None, 1, ROWS_PER_STEP, LANES), step_rows),
        ] + [kvt_spec(p) for p in range(n_pat)],
        out_shape=[
            jax.ShapeDtypeStruct((n_pat, b, SEQ, QW), bf16),
            jax.ShapeDtypeStruct((n_pat, b, SEQ, LANES), f32),
            jax.ShapeDtypeStruct((n_pat, b, SEQ, LANES), f32),
        ] + [jax.ShapeDtypeStruct((b, 2 * QW, win), f32) for win in WINDOWS],
        scratch_shapes=[pltpu.VMEM((D_MODEL // LANES, SEQ, LANES), f32),
                        pltpu.VMEM((SEQ + Q_BLOCK, QW), bf16), pltpu.VMEM((SEQ + Q_BLOCK, QW), bf16),
                        pltpu.VMEM((SEQ + Q_BLOCK, QW), bf16)],
        name="attn_p",
        compiler_params=pltpu.CompilerParams(
            dimension_semantics=("arbitrary", "arbitrary", "arbitrary"), vmem_limit_bytes=VMEM_LIMIT),
    )(x, mod, mod, g_pre, w_in_b, w_in_b, w_in_b, w_kvt)


def _qkv_s_kernel(x_ref, scale_ref, shift_ref, gpre_ref, w_ref, o_ref, *kv_refs):
    h = _norm_mod(x_ref[...], gpre_ref[...], scale_ref[...], shift_ref[...]).astype(bf16)
    tile = _dot(h, w_ref[...])
    o_ref[...] = tile
    for half in range(2):
        @pl.when(pl.program_id(0) == 1 + half)
        def _(half=half):
            for g, kv_ref in enumerate(kv_refs):
                kv_ref[:, half * QW:(half + 1) * QW] = tile[:, g * QW:(g + 1) * QW]


def _qkv_sample(x, scale, shift, g_pre, w_in_b):
    n = x.shape[0]
    n_pat = len(DILATIONS)
    cols = 3 * n_pat * QW
    tile = n_pat * QW
    first = Q_COL0 // tile
    return pl.pallas_call(
        _qkv_s_kernel,
        grid=(cols // tile,),
        in_specs=[
            pl.BlockSpec((n, D_MODEL), lambda j: (0, 0)),
            pl.BlockSpec((n, D_MODEL), lambda j: (0, 0)),
            pl.BlockSpec((n, D_MODEL), lambda j: (0, 0)),
            pl.BlockSpec((1, D_MODEL), lambda j: (0, 0)),
            pl.BlockSpec((D_MODEL, tile), lambda j: (0, first + j)),
        ],
        out_specs=[pl.BlockSpec((n, tile), lambda j: (0, j))]
        + [pl.BlockSpec((n, 2 * QW), lambda j: (0, 0))] * n_pat,
        out_shape=[jax.ShapeDtypeStruct((n, cols), f32)]
        + [jax.ShapeDtypeStruct((n, 2 * QW), f32)] * n_pat,
        name="qkv_s",
        compiler_params=pltpu.CompilerParams(
            dimension_semantics=("arbitrary",), vmem_limit_bytes=VMEM_LIMIT),
    )(x, scale, shift, g_pre, w_in_b)


T_NEW = 8


def _sample_attention(qkv_ref, c0_ref, c1_ref, c2_ref, o_ref, new_sc):
    qkv = qkv_ref[0]
    new_sc[0:T_NEW, :] = qkv[:, 3 * QW:]
    q_all = qkv[:, :3 * QW] * (HEAD_DIM ** -0.5)

    lane8 = lax.broadcasted_iota(jnp.int32, (2 * T_NEW, LANES), 1)
    low = lane8 < HEAD_DIM
    t_idx = lax.broadcasted_iota(jnp.int32, (2 * T_NEW, LANES), 0) % T_NEW
    caches = (c0_ref, c1_ref, c2_ref)
    pairs = range(N_HEADS // 2)
    pats = range(len(DILATIONS))

    def pair_cols(g, hp, base=0):
        return slice(base + g * QW + hp * LANES, base + g * QW + (hp + 1) * LANES)

    s_cache, s_new = {}, {}
    for hp in pairs:
        for g in pats:
            q2 = q_all[:, pair_cols(g, hp)]
            q_st = jnp.concatenate([jnp.where(low[:T_NEW], q2, 0.0), jnp.where(low[:T_NEW], 0.0, q2)],
                                   axis=0).astype(bf16)
            kt = caches[g][0, hp * LANES:(hp + 1) * LANES, :].astype(bf16)
            k_new = new_sc[:, pair_cols(g, hp)].astype(bf16)
            s_cache[hp, g] = _dot(q_st, kt)
            s_new[hp, g] = _dot_nt(q_st, k_new)

    probs = {}
    for hp in pairs:
        m = jnp.full((2 * T_NEW, 1), NEG, f32)
        for g, dil in enumerate(DILATIONS):
            rows = WINDOWS[g]
            rho = lax.broadcasted_iota(jnp.int32, (2 * T_NEW, rows), 1)
            tq = lax.broadcasted_iota(jnp.int32, (2 * T_NEW, rows), 0) % T_NEW
            ok_c = jnp.logical_and(rho >= tq, ((rho - tq) & (dil - 1)) == 0)
            ok_n = jnp.logical_and(lane8 <= t_idx, ((t_idx - lane8) & (dil - 1)) == 0)
            s_cache[hp, g] = jnp.where(ok_c, s_cache[hp, g], NEG)
            s_new[hp, g] = jnp.where(ok_n, s_new[hp, g], NEG)
            m = jnp.maximum(m, jnp.maximum(jnp.max(s_cache[hp, g], axis=-1, keepdims=True),
                                           jnp.max(s_new[hp, g], axis=-1, keepdims=True)))
        l = jnp.zeros((2 * T_NEW, 1), f32)
        for g in pats:
            p_c = jnp.exp(s_cache[hp, g] - m)
            p_n = jnp.exp(s_new[hp, g] - m)
            l = l + jnp.sum(p_c, axis=-1, keepdims=True) + jnp.sum(p_n, axis=-1, keepdims=True)
            probs[hp, g] = (p_c.astype(bf16), p_n.astype(bf16))
        probs[hp] = l

    o_parts = []
    for hp in pairs:
        acc = jnp.zeros((2 * T_NEW, LANES), f32)
        for g in pats:
            vt = caches[g][0, QW + hp * LANES: QW + (hp + 1) * LANES, :].astype(bf16)
            v_new = new_sc[:, pair_cols(g, hp, 3 * QW)].astype(bf16)
            p_c, p_n = probs[hp, g]
            acc = acc + _dot_nt(p_c, vt) + _dot(p_n, v_new)
        out = acc / probs[hp]
        o_parts.append(jnp.where(low[:T_NEW], out[:T_NEW], out[T_NEW:]))
    o_ref[0] = jnp.concatenate(o_parts, axis=1)


TILE = 256


def _expand_heads(w):
    r = lax.broadcasted_iota(jnp.int32, (2 * LANES, QW), 0) % LANES
    head = lax.broadcasted_iota(jnp.int32, (2 * LANES, QW), 1) // HEAD_DIM
    expand = (r == head // 2 + HEAD_DIM * (1 - head % 2)).astype(bf16)
    hi = w.astype(bf16)
    lo = (w - hi.astype(f32)).astype(bf16)
    return _dot(jnp.concatenate([hi, lo], axis=1), expand)


def _rest_kernel(*refs, sample, sample_every=None):
    if sample:
        (x_ref, scale_ref, shift_ref, gate_ref, gpre_ref, gpost_ref, lng_ref, lnb_ref,
         wa_ref, wzb_ref, wg_ref, wsp_ref, bsp_ref, wpa_ref, wpb_ref, wout_ref,
         attn_ref, y_ref, vn_ref) = refs
        x = x_ref[...]
        scale, shift, gate = scale_ref[...], shift_ref[...], gate_ref[...]
    else:
        (x_ref, scale_ref, shift_ref, gate_ref, gpre_ref, gpost_ref, lng_ref, lnb_ref,
         wa_ref, wzb_ref, wg_ref, wsp_ref, bsp_ref, wpa_ref, wpb_ref, wout_ref,
         o1_ref, m1_ref, l1_ref, o4_ref, m4_ref, l4_ref, o16_ref, m16_ref, l16_ref,
         qkv_s_ref, c0_ref, c1_ref, c2_ref, y_ref, attn_s_ref,
         o4_sc, m4_sc, l4_sc, o16_sc, m16_sc, l16_sc, new_sc) = refs
        x = x_ref[0]
        scale, shift, gate = scale_ref[0], shift_ref[0], gate_ref[0]

        @pl.when(jnp.logical_and(pl.program_id(0) == 0, pl.program_id(1) == 0))
        def _():
            new_sc[...] = jnp.zeros_like(new_sc)

        @pl.when(pl.program_id(1) % sample_every == 0)
        def _():
            _sample_attention(qkv_s_ref, c0_ref, c1_ref, c2_ref, attn_s_ref, new_sc)

    rows = x.shape[0]
    h = _norm_mod(x, gpre_ref[...], scale, shift).astype(bf16)

    pa = _dot(h, wa_ref[...])
    u_a, v_a, z_a = pa[:, :D_MODEL], pa[:, D_MODEL:2 * D_MODEL], pa[:, 2 * D_MODEL:]
    mu = jnp.mean(v_a, axis=-1, keepdims=True)
    cen = v_a - mu
    var = jnp.mean(cen * cen, axis=-1, keepdims=True)
    v_n = cen * lax.rsqrt(var + EPS) * lng_ref[...] + lnb_ref[...]
    v_nb = v_n.astype(bf16)
    if sample:
        vn_ref[...] = v_n
        pick = (lax.broadcasted_iota(jnp.int32, (rows, CHUNK), 1)
                == lax.broadcasted_iota(jnp.int32, (rows, CHUNK), 0) % T_NEW).astype(bf16)
        same_seq = (lax.broadcasted_iota(jnp.int32, (rows, rows), 0) // T_NEW
                    == lax.broadcasted_iota(jnp.int32, (rows, rows), 1) // T_NEW)
        cols = []
        for g in range(A_GROUPS):
            tiled = _dot_nt(_dot(pick, wsp_ref[g]).astype(bf16), pick)
            w_blk = jnp.where(same_seq, tiled, 0.0).astype(bf16)
            cols.append(_dot(w_blk, v_nb[:, g * LANES:(g + 1) * LANES]) + bsp_ref[:, g:g + 1])
        zs = jnp.concatenate(cols, axis=1)
    else:
        n_ck = rows // CHUNK
        per_group = []
        for g in range(A_GROUPS):
            rhs = jnp.concatenate(
                [v_nb[ck * CHUNK:(ck + 1) * CHUNK, g * LANES:(g + 1) * LANES] for ck in range(n_ck)], axis=1)
            per_group.append(_dot(wsp_ref[g], rhs) + bsp_ref[:, g:g + 1])
        zs = jnp.concatenate(
            [jnp.concatenate([pg[:, ck * LANES:(ck + 1) * LANES] for pg in per_group], axis=1)
             for ck in range(n_ck)], axis=0)
    y_a = u_a * zs * _silu(z_a)

    if sample:
        attn = attn_ref[...]
    else:
        for osc, msc, lsc, oref, mref, lref, dil in (
                (o4_sc, m4_sc, l4_sc, o4_ref, m4_ref, l4_ref, DILATIONS[1]),
                (o16_sc, m16_sc, l16_sc, o16_ref, m16_ref, l16_ref, DILATIONS[2])):
            n = rows // dil
            for r in range(dil):
                o_r = oref[0, r].astype(f32)
                for k in range(QW // LANES):
                    osc[k, pl.ds(r, n, stride=dil), :] = o_r[:, k * LANES:(k + 1) * LANES]
                msc[pl.ds(r, n, stride=dil), :] = mref[0, r]
                lsc[pl.ds(r, n, stride=dil), :] = lref[0, r]
        ms = (m1_ref[0], m4_sc[...], m16_sc[...])
        ls = (l1_ref[0], l4_sc[...], l16_sc[...])
        unchunk = lambda sc: jnp.concatenate([sc[k] for k in range(QW // LANES)], axis=1)
        outs = (o1_ref[0].astype(f32), unchunk(o4_sc), unchunk(o16_sc))
        m_all = jnp.maximum(jnp.maximum(ms[0], ms[1]), ms[2])
        ws = [jnp.exp(m - m_all) for m in ms]
        den = ws[0] * ls[0] + ws[1] * ls[1] + ws[2] * ls[2]
        attn = sum(_expand_heads(w / den) * o for w, o in zip(ws, outs))

    z_b = _dot(h, wzb_ref[...])
    y_b = (attn * _silu(z_b)).astype(bf16)

    gl = _dot(h, wg_ref[...])
    p_a = _dot(y_a.astype(bf16), wpa_ref[...])
    p_b = _dot(y_b, wpb_ref[...])
    merged = jax.nn.sigmoid(gl[:, :D_MODEL]) * p_a + jax.nn.sigmoid(gl[:, D_MODEL:]) * p_b
    out = _dot(merged.astype(bf16), wout_ref[...])
    normed = out * lax.rsqrt(jnp.mean(out * out, axis=-1, keepdims=True) + EPS) * gpost_ref[...]
    y = x + gate * normed
    if sample:
        y_ref[...] = y
    else:
        y_ref[0] = y


def _full(shape):
    nd = len(shape)
    return pl.BlockSpec(shape, lambda *_: (0,) * nd)


def _weight_specs(weights):
    in_proj = [pl.BlockSpec((D_MODEL, 3 * D_MODEL), lambda *_: (0, 0)),
               pl.BlockSpec((D_MODEL, QW), lambda *_: (0, ZB_COL0 // QW)),
               pl.BlockSpec((D_MODEL, 2 * D_MODEL), lambda *_: (0, GATE_COL0 // (2 * D_MODEL)))]
    return in_proj + [_full(w.shape) for w in weights[3:]]


def _rest_prompt(x, mod, vecs, weights, attn_parts, qkv_s, caches):
    b = x.shape[0]
    n_pat = len(DILATIONS)
    tiles = SEQ // TILE
    n_seq = qkv_s.shape[0]
    sample_every, rem = divmod(b * tiles, n_seq)
    assert rem == 0 and tiles % sample_every == 0, "sample sequences must tile the grid evenly"
    seq_of = lambda i, c: (i * (tiles // sample_every) + c // sample_every, 0, 0)
    tile3 = lambda i, c: (i, c, 0)
    part_args, part_specs, scratch = [], [], []
    for pattern, dil in enumerate(DILATIONS):
        for a in attn_parts:
            lanes = a.shape[-1]
            if dil == 1:
                part_args.append(a)
                part_specs.append(pl.BlockSpec((None, 1, TILE, lanes),
                                               lambda i, c, p=pattern: (p, i, c, 0)))
            else:
                part_args.append(a.reshape(n_pat, b, dil, SEQ // dil, lanes))
                part_specs.append(pl.BlockSpec((None, 1, dil, TILE // dil, lanes),
                                               lambda i, c, p=pattern: (p, i, 0, c, 0)))
        if dil != 1:
            scratch += [pltpu.VMEM((QW // LANES, TILE, LANES), f32),
                        pltpu.VMEM((TILE, LANES), f32), pltpu.VMEM((TILE, LANES), f32)]
    in_specs = (
        [pl.BlockSpec((1, TILE, D_MODEL), tile3)]
        + [pl.BlockSpec((1, 1, D_MODEL), lambda i, c, col=col: (i, 0, col))
           for col in (MOD_SCALE, MOD_SHIFT, MOD_GATE)]
        + [_full(v.shape) for v in vecs] + _weight_specs(weights) + part_specs
        + [pl.BlockSpec((1, T_NEW, 9 * QW), seq_of)]
        + [pl.BlockSpec((1, 2 * QW, win), seq_of) for win in WINDOWS])
    scratch.append(pltpu.VMEM((LANES, 6 * QW), f32))
    return pl.pallas_call(
        functools.partial(_rest_kernel, sample=False, sample_every=sample_every),
        grid=(b, tiles),
        in_specs=in_specs,
        out_specs=[pl.BlockSpec((1, TILE, D_MODEL), tile3), pl.BlockSpec((1, T_NEW, QW), seq_of)],
        out_shape=[jax.ShapeDtypeStruct((b, SEQ, D_MODEL), f32),
                   jax.ShapeDtypeStruct((n_seq, T_NEW, QW), f32)],
        scratch_shapes=scratch,
        name="rest_p",
        compiler_params=pltpu.CompilerParams(
            dimension_semantics=("arbitrary", "arbitrary"), vmem_limit_bytes=VMEM_LIMIT),
    )(x, mod, mod, mod, *vecs, *weights, *part_args, qkv_s, *caches)


def _rest_sample(x, scale, shift, gate, vecs, weights, attn):
    n = x.shape[0]
    args = (x, scale, shift, gate, *vecs, *weights, attn)
    in_specs = ([_full(a.shape) for a in (x, scale, shift, gate, *vecs)] + _weight_specs(weights)
                + [_full(attn.shape)])
    return pl.pallas_call(
        functools.partial(_rest_kernel, sample=True),
        in_specs=in_specs,
        out_specs=[_full((n, D_MODEL)), _full((n, D_MODEL))],
        out_shape=[jax.ShapeDtypeStruct((n, D_MODEL), f32), jax.ShapeDtypeStruct((n, D_MODEL), f32)],
        grid=(1,),
        name="rest_s",
        compiler_params=pltpu.CompilerParams(
            dimension_semantics=("arbitrary",), vmem_limit_bytes=VMEM_LIMIT),
    )(*args)


def kernel(x_prompt, x_sample, cache_kv_w128, cache_kv_w512, cache_kv_w2048, c_prompt, c_sample, w_cond, b_cond, g_pre, w_in, ln_v_g, ln_v_b, w_spatial, b_spatial, w_proj_a, w_proj_b, w_out, g_post):
    assert w_in.shape[0] == 1, "single layer"
    bp, seq, _ = x_prompt.shape
    bs, t_new, _ = x_sample.shape
    assert seq == SEQ and t_new == T_NEW

    w_in_b = w_in[0].astype(bf16)
    w_kvt = jnp.transpose(w_in[0][:, K_COL0:ZB_COL0].reshape(D_MODEL, 2, 3, QW), (2, 1, 3, 0))
    w_kvt = w_kvt.reshape(3, 2 * QW, D_MODEL).astype(bf16)
    causal = jnp.tril(jnp.ones((CHUNK, CHUNK), bool))
    w_sp = jnp.where(causal[None], w_spatial[0], 0.0)
    w_sp_p = w_sp.astype(bf16)
    n_s = bs * T_NEW
    b_sp_p = b_spatial[0].T
    b_sp_s = jnp.tile(b_spatial[0][:, :T_NEW].T, (bs, 1))
    weights_tail = (w_proj_a[0].astype(bf16), w_proj_b[0].astype(bf16), w_out[0].astype(bf16))
    vecs = (g_pre, g_post, ln_v_g, ln_v_b)

    mod = _cond(jnp.concatenate([c_prompt, c_sample], axis=0), w_cond[0], b_cond)
    shift, scale, gate = mod[:, :D_MODEL], mod[:, D_MODEL:2 * D_MODEL], mod[:, 2 * D_MODEL:]
    mod_p = mod.reshape(bp + bs, 1, 3 * D_MODEL)
    ms = lambda a: jnp.repeat(a[bp:], T_NEW, axis=0)

    xs = x_sample.reshape(n_s, D_MODEL)
    qkv_s, *kv_s = _qkv_sample(xs, ms(scale), ms(shift), g_pre, w_in_b)
    caches_t = []
    for cache, win in zip((cache_kv_w128, cache_kv_w512, cache_kv_w2048), WINDOWS):
        caches_t.append(jnp.transpose(cache[0], (0, 2, 3, 4, 1)).reshape(bs, 2 * QW, win))

    o_all, m_all, l_all, *kvt = _attn_prompt(x_prompt, mod_p, g_pre, w_in_b, w_kvt)
    in_proj = (w_in_b, w_in_b, w_in_b)
    y_p, attn_s = _rest_prompt(x_prompt, mod_p, vecs,
                               in_proj + (w_sp_p, b_sp_p) + weights_tail, (o_all, m_all, l_all),
                               qkv_s.reshape(bs, T_NEW, 9 * QW), caches_t)
    kv_p = [jnp.transpose(a.reshape(bp, 2, N_HEADS, HEAD_DIM, win), (0, 4, 1, 2, 3))[None]
            for a, win in zip(kvt, WINDOWS)]

    attn_s = attn_s.reshape(n_s, QW)
    y_s, v_n_s = _rest_sample(xs, ms(scale), ms(shift), ms(gate), vecs,
                              in_proj + (w_sp_p, b_sp_s) + weights_tail, attn_s)
    kv_s = [a.reshape(1, bs, T_NEW, 2, N_HEADS, HEAD_DIM) for a in kv_s]

    return (y_p, y_s.reshape(bs, T_NEW, D_MODEL), kv_p[0], kv_p[1], kv_p[2],
            kv_s[0], kv_s[1], kv_s[2], v_n_s.reshape(1, bs, T_NEW, D_MODEL))
```

```python
import functools

import jax
import jax.numpy as jnp
from jax import lax
from jax.experimental import pallas as pl
from jax.experimental.pallas import tpu as pltpu

D_MODEL = 1024
SEQ = 2048
HEAD_DIM = 64
N_HEADS = 8
QW = N_HEADS * HEAD_DIM
WINDOWS = (128, 512, 2048)
DILATIONS = (1, 4, 16)
CHUNK = 128
A_GROUPS = 8
EPS = 1e-6
NEG = -1e30
LANES = 128
VMEM_LIMIT = 56 * 1024 * 1024
Q_COL0 = 3 * D_MODEL
K_COL0 = Q_COL0 + 3 * QW
V_COL0 = K_COL0 + 3 * QW
ZB_COL0 = V_COL0 + 3 * QW
GATE_COL0 = ZB_COL0 + QW
MOD_SHIFT, MOD_SCALE, MOD_GATE = 0, 1, 2

f32 = jnp.float32
bf16 = jnp.bfloat16


def _silu(x):
    return x * jax.nn.sigmoid(x)


def _norm_mod(x, g_pre, scale, shift):
    y = x * lax.rsqrt(jnp.mean(x * x, axis=-1, keepdims=True) + EPS) * g_pre
    return y * (1.0 + scale) + shift


def _dot(a, b):
    return jnp.dot(a, b, preferred_element_type=f32)


def _dot_nt(a, b):
    return lax.dot_general(a, b, (((1,), (1,)), ((), ())), preferred_element_type=f32)


def _cond_kernel(c_ref, w_ref, b_ref, o_ref):
    o_ref[...] = _dot(_silu(c_ref[...]).astype(bf16), w_ref[...].astype(bf16)) + b_ref[...]


def _cond(c_all, w_cond, b_cond):
    n = c_all.shape[0]
    return pl.pallas_call(
        _cond_kernel,
        out_shape=jax.ShapeDtypeStruct((n, 3 * D_MODEL), f32),
        name="cond",
        compiler_params=pltpu.CompilerParams(vmem_limit_bytes=VMEM_LIMIT),
    )(c_all, w_cond, b_cond)


ROWS_PER_STEP = 512
Q_BLOCK = 128
NORM_ROWS = 256


def _attn_step(c, last_step, wq_ref, wk_ref, wv_ref, wkvt_ref, o_ref, m_ref, l_ref, kvt_ref,
               h_sc, k_sc, ve_sc, vo_sc, *, dil, window):
    seg = SEQ // dil
    n_lc = D_MODEL // LANES

    def rows_of(start, n):
        idx = pl.ds(pl.multiple_of(start, n), n) if dil == 1 else pl.ds(start, n, stride=dil)
        return jnp.concatenate([h_sc[k, idx, :] for k in range(n_lc)], axis=1)

    if seg >= ROWS_PER_STEP:
        per = seg // ROWS_PER_STEP
        h = rows_of((c // per) + (c % per) * ROWS_PER_STEP * dil, ROWS_PER_STEP)
    else:
        per = ROWS_PER_STEP // seg
        h = jnp.concatenate([rows_of(per * c + i, seg) for i in range(per)], axis=0)
    h = h.astype(bf16)
    base = pl.multiple_of(c * ROWS_PER_STEP, ROWS_PER_STEP)

    def tokens_t(start, n):
        return jnp.concatenate([h_sc[k, pl.ds(start, n), :] for k in range(n_lc)], axis=1).astype(bf16)

    if window == SEQ:
        kvt_ref[0] = _dot_nt(wkvt_ref[0], tokens_t(base, ROWS_PER_STEP))
    else:
        @pl.when(c == last_step)
        def _():
            kvt_ref[0] = _dot_nt(wkvt_ref[0], tokens_t(SEQ - window, window))

    low_w = (lax.broadcasted_iota(jnp.int32, (ROWS_PER_STEP, QW), 1) % LANES) < HEAD_DIM
    v = _dot(h, wv_ref[...])
    new_rows = pl.ds(Q_BLOCK + base, ROWS_PER_STEP)
    k_sc[new_rows, :] = _dot(h, wk_ref[...]).astype(bf16)
    ve_sc[new_rows, :] = jnp.where(low_w, v, 1.0).astype(bf16)
    vo_sc[new_rows, :] = jnp.where(low_w, 1.0, v).astype(bf16)
    low_q = (lax.broadcasted_iota(jnp.int32, (Q_BLOCK, QW), 1) % LANES) < HEAD_DIM

    def project_q(j):
        q = _dot(h[j * Q_BLOCK:(j + 1) * Q_BLOCK], wq_ref[...]) * (HEAD_DIM ** -0.5)
        return jnp.where(low_q, q, 0.0).astype(bf16), jnp.where(low_q, 0.0, q).astype(bf16)

    n_blocks = ROWS_PER_STEP // Q_BLOCK
    pairs = N_HEADS // 2
    qi = lax.broadcasted_iota(jnp.int32, (2 * Q_BLOCK, 2 * Q_BLOCK), 0) % Q_BLOCK
    kj = lax.broadcasted_iota(jnp.int32, (2 * Q_BLOCK, 2 * Q_BLOCK), 1)
    lane = lax.broadcasted_iota(jnp.int32, (Q_BLOCK, LANES), 1)
    low = lane < HEAD_DIM

    def has_prev(j):
        return seg > ROWS_PER_STEP or (j * Q_BLOCK) % seg != 0

    def key_rows(j):
        r0 = pl.multiple_of(base + j * Q_BLOCK, Q_BLOCK)
        return pl.ds(r0, 2 * Q_BLOCK) if has_prev(j) else pl.ds(r0 + Q_BLOCK, Q_BLOCK)

    def mask_of(j):
        if not has_prev(j):
            own = (2 * Q_BLOCK, Q_BLOCK)
            return (lax.broadcasted_iota(jnp.int32, own, 1)
                    <= lax.broadcasted_iota(jnp.int32, own, 0) % Q_BLOCK)
        off = jnp.where((base % seg) != 0, 0, Q_BLOCK) if (seg > ROWS_PER_STEP and j == 0) else 0
        in_prev = jnp.logical_and(kj < Q_BLOCK, kj >= qi + off)
        in_cur = jnp.logical_and(kj >= Q_BLOCK, kj - Q_BLOCK <= qi)
        return jnp.logical_or(in_prev, in_cur)

    def qk(j, q_even, q_odd):
        keys = key_rows(j)
        out = []
        for hp in range(pairs):
            cols = slice(hp * LANES, (hp + 1) * LANES)
            q_st = jnp.concatenate([q_even[:, cols], q_odd[:, cols]], axis=0)
            out.append(_dot_nt(q_st, k_sc[keys, cols]))
        return out

    def finish(j, scores):
        rows = slice(j * Q_BLOCK, (j + 1) * Q_BLOCK)
        keys = key_rows(j)
        mask = mask_of(j)
        ps, ms = [], []
        for hp in range(pairs):
            s = jnp.where(mask, scores[hp], NEG)
            for half in (s[:Q_BLOCK], s[Q_BLOCK:]):
                m = jnp.max(half, axis=-1, keepdims=True)
                ps.append(jnp.exp(half - m).astype(bf16))
                ms.append(m)
        o_parts = []
        m_tile = jnp.zeros((Q_BLOCK, LANES), f32)
        l_tile = jnp.ones((Q_BLOCK, LANES), f32)
        for hp in range(pairs):
            cols = slice(hp * LANES, (hp + 1) * LANES)
            acc_e = _dot(ps[2 * hp], ve_sc[keys, cols])
            acc_o = _dot(ps[2 * hp + 1], vo_sc[keys, cols])
            o_parts.append(jnp.where(low, acc_e, acc_o))
            l_tile = jnp.where(lane == HEAD_DIM + hp, acc_e, jnp.where(lane == hp, acc_o, l_tile))
            m_tile = jnp.where(lane == HEAD_DIM + hp, ms[2 * hp],
                               jnp.where(lane == hp, ms[2 * hp + 1], m_tile))
        o_ref[0, rows, :] = jnp.concatenate(o_parts, axis=1).astype(bf16)
        m_ref[0, rows, :] = m_tile
        l_ref[0, rows, :] = l_tile

    qs = {j: project_q(j) for j in range(min(2, n_blocks))}
    scores = qk(0, *qs[0])
    for j in range(n_blocks):
        nxt = qk(j + 1, *qs[j + 1]) if j + 1 < n_blocks else None
        if j + 2 < n_blocks:
            qs[j + 2] = project_q(j + 2)
        finish(j, scores)
        scores = nxt


def _attn_kernel(x_ref, scale_ref, shift_ref, gpre_ref, wq_ref, wk_ref, wv_ref, wkvt_ref,
                 o_ref, m_ref, l_ref, kvt0_ref, kvt1_ref, kvt2_ref, h_sc, k_sc, ve_sc, vo_sc):
    g = pl.program_id(1)
    c = pl.program_id(2)
    last_step = pl.num_programs(2) - 1

    @pl.when(jnp.logical_and(g == 0, c == 0))
    def _():
        zeros = jnp.zeros((Q_BLOCK, QW), bf16)
        k_sc[0:Q_BLOCK, :] = zeros
        ve_sc[0:Q_BLOCK, :] = zeros
        vo_sc[0:Q_BLOCK, :] = zeros
        for rb in range(SEQ // NORM_ROWS):
            rs = slice(rb * NORM_ROWS, (rb + 1) * NORM_ROWS)
            hb = _norm_mod(x_ref[0, rs, :], gpre_ref[...], scale_ref[0], shift_ref[0])
            for k in range(D_MODEL // LANES):
                h_sc[k, rs, :] = hb[:, k * LANES:(k + 1) * LANES]

    for pattern, kvt_ref in enumerate((kvt0_ref, kvt1_ref, kvt2_ref)):
        @pl.when(g == pattern)
        def _(pattern=pattern, kvt_ref=kvt_ref):
            _attn_step(c, last_step, wq_ref, wk_ref, wv_ref, wkvt_ref, o_ref, m_ref, l_ref, kvt_ref,
                       h_sc, k_sc, ve_sc, vo_sc, dil=DILATIONS[pattern], window=WINDOWS[pattern])


def _attn_prompt(x, mod, g_pre, w_in_b, w_kvt):
    b = x.shape[0]
    n_pat = len(DILATIONS)
    steps = SEQ // ROWS_PER_STEP
    full_window = WINDOWS.index(SEQ)
    per_b = lambda i, g, c: (i, 0, 0)
    step_rows = lambda i, g, c: (g, i, c, 0)

    def kvt_spec(pattern):
        if WINDOWS[pattern] == SEQ:
            return pl.BlockSpec((1, 2 * QW, ROWS_PER_STEP),
                                lambda i, g, c: (i, 0, jnp.where(g == full_window, c, 0)))
        return pl.BlockSpec((1, 2 * QW, WINDOWS[pattern]), per_b)

    return pl.pallas_call(
        _attn_kernel,
        grid=(b, n_pat, steps),
        in_specs=[
            pl.BlockSpec((1, SEQ, D_MODEL), per_b),
            pl.BlockSpec((1, 1, D_MODEL), lambda i, g, c: (i, 0, MOD_SCALE)),
            pl.BlockSpec((1, 1, D_MODEL), lambda i, g, c: (i, 0, MOD_SHIFT)),
            pl.BlockSpec((1, D_MODEL), lambda i, g, c: (0, 0)),
            pl.BlockSpec((D_MODEL, QW), lambda i, g, c: (0, Q_COL0 // QW + g)),
            pl.BlockSpec((D_MODEL, QW), lambda i, g, c: (0, K_COL0 // QW + g)),
            pl.BlockSpec((D_MODEL, QW), lambda i, g, c: (0, V_COL0 // QW + g)),
            pl.BlockSpec((1, 2 * QW, D_MODEL), lambda i, g, c: (g, 0, 0)),
        ],
        out_specs=[
            pl.BlockSpec((None, 1, ROWS_PER_STEP, QW), step_rows),
            pl.BlockSpec((None, 1, ROWS_PER_STEP, LANES), step_rows),
            pl.BlockSpec((None, 1, ROWS_PER_STEP, LANES), step_rows),
        ] + [kvt_spec(p) for p in range(n_pat)],
        out_shape=[
            jax.ShapeDtypeStruct((n_pat, b, SEQ, QW), bf16),
            jax.ShapeDtypeStruct((n_pat, b, SEQ, LANES), f32),
            jax.ShapeDtypeStruct((n_pat, b, SEQ, LANES), f32),
        ] + [jax.ShapeDtypeStruct((b, 2 * QW, win), f32) for win in WINDOWS],
        scratch_shapes=[pltpu.VMEM((D_MODEL // LANES, SEQ, LANES), f32),
                        pltpu.VMEM((SEQ + Q_BLOCK, QW), bf16), pltpu.VMEM((SEQ + Q_BLOCK, QW), bf16),
                        pltpu.VMEM((SEQ + Q_BLOCK, QW), bf16)],
        name="attn_p",
        compiler_params=pltpu.CompilerParams(
            dimension_semantics=("arbitrary", "arbitrary", "arbitrary"), vmem_limit_bytes=VMEM_LIMIT),
    )(x, mod, mod, g_pre, w_in_b, w_in_b, w_in_b, w_kvt)


def _qkv_s_kernel(x_ref, scale_ref, shift_ref, gpre_ref, w_ref, o_ref, *kv_refs):
    h = _norm_mod(x_ref[...], gpre_ref[...], scale_ref[...], shift_ref[...]).astype(bf16)
    tile = _dot(h, w_ref[...])
    o_ref[...] = tile
    for half in range(2):
        @pl.when(pl.program_id(0) == 1 + half)
        def _(half=half):
            for g, kv_ref in enumerate(kv_refs):
                kv_ref[:, half * QW:(half + 1) * QW] = tile[:, g * QW:(g + 1) * QW]


def _qkv_sample(x, scale, shift, g_pre, w_in_b):
    n = x.shape[0]
    n_pat = len(DILATIONS)
    cols = 3 * n_pat * QW
    tile = n_pat * QW
    first = Q_COL0 // tile
    return pl.pallas_call(
        _qkv_s_kernel,
        grid=(cols // tile,),
        in_specs=[
            pl.BlockSpec((n, D_MODEL), lambda j: (0, 0)),
            pl.BlockSpec((n, D_MODEL), lambda j: (0, 0)),
            pl.BlockSpec((n, D_MODEL), lambda j: (0, 0)),
            pl.BlockSpec((1, D_MODEL), lambda j: (0, 0)),
            pl.BlockSpec((D_MODEL, tile), lambda j: (0, first + j)),
        ],
        out_specs=[pl.BlockSpec((n, tile), lambda j: (0, j))]
        + [pl.BlockSpec((n, 2 * QW), lambda j: (0, 0))] * n_pat,
        out_shape=[jax.ShapeDtypeStruct((n, cols), f32)]
        + [jax.ShapeDtypeStruct((n, 2 * QW), f32)] * n_pat,
        name="qkv_s",
        compiler_params=pltpu.CompilerParams(
            dimension_semantics=("arbitrary",), vmem_limit_bytes=VMEM_LIMIT),
    )(x, scale, shift, g_pre, w_in_b)


T_NEW = 8


def _sample_attention(qkv_ref, c0_ref, c1_ref, c2_ref, o_ref, new_sc):
    qkv = qkv_ref[0]
    new_sc[0:T_NEW, :] = qkv[:, 3 * QW:]
    q_all = qkv[:, :3 * QW] * (HEAD_DIM ** -0.5)

    lane8 = lax.broadcasted_iota(jnp.int32, (2 * T_NEW, LANES), 1)
    low = lane8 < HEAD_DIM
    t_idx = lax.broadcasted_iota(jnp.int32, (2 * T_NEW, LANES), 0) % T_NEW
    caches = (c0_ref, c1_ref, c2_ref)
    pairs = range(N_HEADS // 2)
    pats = range(len(DILATIONS))

    def pair_cols(g, hp, base=0):
        return slice(base + g * QW + hp * LANES, base + g * QW + (hp + 1) * LANES)

    s_cache, s_new = {}, {}
    for hp in pairs:
        for g in pats:
            q2 = q_all[:, pair_cols(g, hp)]
            q_st = jnp.concatenate([jnp.where(low[:T_NEW], q2, 0.0), jnp.where(low[:T_NEW], 0.0, q2)],
                                   axis=0).astype(bf16)
            kt = caches[g][0, hp * LANES:(hp + 1) * LANES, :].astype(bf16)
            k_new = new_sc[:, pair_cols(g, hp)].astype(bf16)
            s_cache[hp, g] = _dot(q_st, kt)
            s_new[hp, g] = _dot_nt(q_st, k_new)

    probs = {}
    for hp in pairs:
        m = jnp.full((2 * T_NEW, 1), NEG, f32)
        for g, dil in enumerate(DILATIONS):
            rows = WINDOWS[g]
            rho = lax.broadcasted_iota(jnp.int32, (2 * T_NEW, rows), 1)
            tq = lax.broadcasted_iota(jnp.int32, (2 * T_NEW, rows), 0) % T_NEW
            ok_c = jnp.logical_and(rho >= tq, ((rho - tq) & (dil - 1)) == 0)
            ok_n = jnp.logical_and(lane8 <= t_idx, ((t_idx - lane8) & (dil - 1)) == 0)
            s_cache[hp, g] = jnp.where(ok_c, s_cache[hp, g], NEG)
            s_new[hp, g] = jnp.where(ok_n, s_new[hp, g], NEG)
            m = jnp.maximum(m, jnp.maximum(jnp.max(s_cache[hp, g], axis=-1, keepdims=True),
                                           jnp.max(s_new[hp, g], axis=-1, keepdims=True)))
        l = jnp.zeros((2 * T_NEW, 1), f32)
        for g in pats:
            p_c = jnp.exp(s_cache[hp, g] - m)
            p_n = jnp.exp(s_new[hp, g] - m)
            l = l + jnp.sum(p_c, axis=-1, keepdims=True) + jnp.sum(p_n, axis=-1, keepdims=True)
            probs[hp, g] = (p_c.astype(bf16), p_n.astype(bf16))
        probs[hp] = l

    o_parts = []
    for hp in pairs:
        acc = jnp.zeros((2 * T_NEW, LANES), f32)
        for g in pats:
            vt = caches[g][0, QW + hp * LANES: QW + (hp + 1) * LANES, :].astype(bf16)
            v_new = new_sc[:, pair_cols(g, hp, 3 * QW)].astype(bf16)
            p_c, p_n = probs[hp, g]
            acc = acc + _dot_nt(p_c, vt) + _dot(p_n, v_new)
        out = acc / probs[hp]
        o_parts.append(jnp.where(low[:T_NEW], out[:T_NEW], out[T_NEW:]))
    o_ref[0] = jnp.concatenate(o_parts, axis=1)


TILE = 256


def _expand_heads(w):
    r = lax.broadcasted_iota(jnp.int32, (2 * LANES, QW), 0) % LANES
    head = lax.broadcasted_iota(jnp.int32, (2 * LANES, QW), 1) // HEAD_DIM
    expand = (r == head // 2 + HEAD_DIM * (1 - head % 2)).astype(bf16)
    hi = w.astype(bf16)
    lo = (w - hi.astype(f32)).astype(bf16)
    return _dot(jnp.concatenate([hi, lo], axis=1), expand)


def _rest_kernel(*refs, sample, sample_every=None):
    if sample:
        (x_ref, scale_ref, shift_ref, gate_ref, gpre_ref, gpost_ref, lng_ref, lnb_ref,
         wa_ref, wzb_ref, wg_ref, wsp_ref, bsp_ref, wpa_ref, wpb_ref, wout_ref,
         attn_ref, y_ref, vn_ref) = refs
        x = x_ref[...]
        scale, shift, gate = scale_ref[...], shift_ref[...], gate_ref[...]
    else:
        (x_ref, scale_ref, shift_ref, gate_ref, gpre_ref, gpost_ref, lng_ref, lnb_ref,
         wa_ref, wzb_ref, wg_ref, wsp_ref, bsp_ref, wpa_ref, wpb_ref, wout_ref,
         o1_ref, m1_ref, l1_ref, o4_ref, m4_ref, l4_ref, o16_ref, m16_ref, l16_ref,
         qkv_s_ref, c0_ref, c1_ref, c2_ref, y_ref, attn_s_ref,
         o4_sc, m4_sc, l4_sc, o16_sc, m16_sc, l16_sc, new_sc) = refs
        x = x_ref[0]
        scale, shift, gate = scale_ref[0], shift_ref[0], gate_ref[0]

        @pl.when(jnp.logical_and(pl.program_id(0) == 0, pl.program_id(1) == 0))
        def _():
            new_sc[...] = jnp.zeros_like(new_sc)

        @pl.when(pl.program_id(1) % sample_every == 0)
        def _():
            _sample_attention(qkv_s_ref, c0_ref, c1_ref, c2_ref, attn_s_ref, new_sc)

    rows = x.shape[0]
    h = _norm_mod(x, gpre_ref[...], scale, shift).astype(bf16)

    pa = _dot(h, wa_ref[...])
    u_a, v_a, z_a = pa[:, :D_MODEL], pa[:, D_MODEL:2 * D_MODEL], pa[:, 2 * D_MODEL:]
    mu = jnp.mean(v_a, axis=-1, keepdims=True)
    cen = v_a - mu
    var = jnp.mean(cen * cen, axis=-1, keepdims=True)
    v_n = cen * lax.rsqrt(var + EPS) * lng_ref[...] + lnb_ref[...]
    v_nb = v_n.astype(bf16)
    if sample:
        vn_ref[...] = v_n
        pick = (lax.broadcasted_iota(jnp.int32, (rows, CHUNK), 1)
                == lax.broadcasted_iota(jnp.int32, (rows, CHUNK), 0) % T_NEW).astype(bf16)
        same_seq = (lax.broadcasted_iota(jnp.int32, (rows, rows), 0) // T_NEW
                    == lax.broadcasted_iota(jnp.int32, (rows, rows), 1) // T_NEW)
        cols = []
        for g in range(A_GROUPS):
            tiled = _dot_nt(_dot(pick, wsp_ref[g]).astype(bf16), pick)
            w_blk = jnp.where(same_seq, tiled, 0.0).astype(bf16)
            cols.append(_dot(w_blk, v_nb[:, g * LANES:(g + 1) * LANES]) + bsp_ref[:, g:g + 1])
        zs = jnp.concatenate(cols, axis=1)
    else:
        n_ck = rows // CHUNK
        per_group = []
        for g in range(A_GROUPS):
            rhs = jnp.concatenate(
                [v_nb[ck * CHUNK:(ck + 1) * CHUNK, g * LANES:(g + 1) * LANES] for ck in range(n_ck)], axis=1)
            per_group.append(_dot(wsp_ref[g], rhs) + bsp_ref[:, g:g + 1])
        zs = jnp.concatenate(
            [jnp.concatenate([pg[:, ck * LANES:(ck + 1) * LANES] for pg in per_group], axis=1)
             for ck in range(n_ck)], axis=0)
    y_a = u_a * zs * _silu(z_a)

    if sample:
        attn = attn_ref[...]
    else:
        for osc, msc, lsc, oref, mref, lref, dil in (
                (o4_sc, m4_sc, l4_sc, o4_ref, m4_ref, l4_ref, DILATIONS[1]),
                (o16_sc, m16_sc, l16_sc, o16_ref, m16_ref, l16_ref, DILATIONS[2])):
            n = rows // dil
            for r in range(dil):
                o_r = oref[0, r].astype(f32)
                for k in range(QW // LANES):
                    osc[k, pl.ds(r, n, stride=dil), :] = o_r[:, k * LANES:(k + 1) * LANES]
                msc[pl.ds(r, n, stride=dil), :] = mref[0, r]
                lsc[pl.ds(r, n, stride=dil), :] = lref[0, r]
        ms = (m1_ref[0], m4_sc[...], m16_sc[...])
        ls = (l1_ref[0], l4_sc[...], l16_sc[...])
        unchunk = lambda sc: jnp.concatenate([sc[k] for k in range(QW // LANES)], axis=1)
        outs = (o1_ref[0].astype(f32), unchunk(o4_sc), unchunk(o16_sc))
        m_all = jnp.maximum(jnp.maximum(ms[0], ms[1]), ms[2])
        ws = [jnp.exp(m - m_all) for m in ms]
        den = ws[0] * ls[0] + ws[1] * ls[1] + ws[2] * ls[2]
        attn = sum(_expand_heads(w / den) * o for w, o in zip(ws, outs))

    z_b = _dot(h, wzb_ref[...])
    y_b = (attn * _silu(z_b)).astype(bf16)

    gl = _dot(h, wg_ref[...])
    p_a = _dot(y_a.astype(bf16), wpa_ref[...])
    p_b = _dot(y_b, wpb_ref[...])
    merged = jax.nn.sigmoid(gl[:, :D_MODEL]) * p_a + jax.nn.sigmoid(gl[:, D_MODEL:]) * p_b
    out = _dot(merged.astype(bf16), wout_ref[...])
    normed = out * lax.rsqrt(jnp.mean(out * out, axis=-1, keepdims=True) + EPS) * gpost_ref[...]
    y = x + gate * normed
    if sample:
        y_ref[...] = y
    else:
        y_ref[0] = y


def _full(shape):
    nd = len(shape)
    return pl.BlockSpec(shape, lambda *_: (0,) * nd)


def _weight_specs(weights):
    in_proj = [pl.BlockSpec((D_MODEL, 3 * D_MODEL), lambda *_: (0, 0)),
               pl.BlockSpec((D_MODEL, QW), lambda *_: (0, ZB_COL0 // QW)),
               pl.BlockSpec((D_MODEL, 2 * D_MODEL), lambda *_: (0, GATE_COL0 // (2 * D_MODEL)))]
    return in_proj + [_full(w.shape) for w in weights[3:]]


def _rest_prompt(x, mod, vecs, weights, attn_parts, qkv_s, caches):
    b = x.shape[0]
    n_pat = len(DILATIONS)
    tiles = SEQ // TILE
    n_seq = qkv_s.shape[0]
    sample_every, rem = divmod(b * tiles, n_seq)
    assert rem == 0 and tiles % sample_every == 0, "sample sequences must tile the grid evenly"
    seq_of = lambda i, c: (i * (tiles // sample_every) + c // sample_every, 0, 0)
    tile3 = lambda i, c: (i, c, 0)
    part_args, part_specs, scratch = [], [], []
    for pattern, dil in enumerate(DILATIONS):
        for a in attn_parts:
            lanes = a.shape[-1]
            if dil == 1:
                part_args.append(a)
                part_specs.append(pl.BlockSpec((None, 1, TILE, lanes),
                                               lambda i, c, p=pattern: (p, i, c, 0)))
            else:
                part_args.append(a.reshape(n_pat, b, dil, SEQ // dil, lanes))
                part_specs.append(pl.BlockSpec((None, 1, dil, TILE // dil, lanes),
                                               lambda i, c, p=pattern: (p, i, 0, c, 0)))
        if dil != 1:
            scratch += [pltpu.VMEM((QW // LANES, TILE, LANES), f32),
                        pltpu.VMEM((TILE, LANES), f32), pltpu.VMEM((TILE, LANES), f32)]
    in_specs = (
        [pl.BlockSpec((1, TILE, D_MODEL), tile3)]
        + [pl.BlockSpec((1, 1, D_MODEL), lambda i, c, col=col: (i, 0, col))
           for col in (MOD_SCALE, MOD_SHIFT, MOD_GATE)]
        + [_full(v.shape) for v in vecs] + _weight_specs(weights) + part_specs
        + [pl.BlockSpec((1, T_NEW, 9 * QW), seq_of)]
        + [pl.BlockSpec((1, 2 * QW, win), seq_of) for win in WINDOWS])
    scratch.append(pltpu.VMEM((LANES, 6 * QW), f32))
    return pl.pallas_call(
        functools.partial(_rest_kernel, sample=False, sample_every=sample_every),
        grid=(b, tiles),
        in_specs=in_specs,
        out_specs=[pl.BlockSpec((1, TILE, D_MODEL), tile3), pl.BlockSpec((1, T_NEW, QW), seq_of)],
        out_shape=[jax.ShapeDtypeStruct((b, SEQ, D_MODEL), f32),
                   jax.ShapeDtypeStruct((n_seq, T_NEW, QW), f32)],
        scratch_shapes=scratch,
        name="rest_p",
        compiler_params=pltpu.CompilerParams(
            dimension_semantics=("arbitrary", "arbitrary"), vmem_limit_bytes=VMEM_LIMIT),
    )(x, mod, mod, mod, *vecs, *weights, *part_args, qkv_s, *caches)


def _rest_sample(x, scale, shift, gate, vecs, weights, attn):
    n = x.shape[0]
    args = (x, scale, shift, gate, *vecs, *weights, attn)
    in_specs = ([_full(a.shape) for a in (x, scale, shift, gate, *vecs)] + _weight_specs(weights)
                + [_full(attn.shape)])
    return pl.pallas_call(
        functools.partial(_rest_kernel, sample=True),
        in_specs=in_specs,
        out_specs=[_full((n, D_MODEL)), _full((n, D_MODEL))],
        out_shape=[jax.ShapeDtypeStruct((n, D_MODEL), f32), jax.ShapeDtypeStruct((n, D_MODEL), f32)],
        grid=(1,),
        name="rest_s",
        compiler_params=pltpu.CompilerParams(
            dimension_semantics=("arbitrary",), vmem_limit_bytes=VMEM_LIMIT),
    )(*args)


def kernel(x_prompt, x_sample, cache_kv_w128, cache_kv_w512, cache_kv_w2048, c_prompt, c_sample, w_cond, b_cond, g_pre, w_in, ln_v_g, ln_v_b, w_spatial, b_spatial, w_proj_a, w_proj_b, w_out, g_post):
    assert w_in.shape[0] == 1, "single layer"
    bp, seq, _ = x_prompt.shape
    bs, t_new, _ = x_sample.shape
    assert seq == SEQ and t_new == T_NEW

    w_in_b = w_in[0].astype(bf16)
    w_kvt = jnp.transpose(w_in[0][:, K_COL0:ZB_COL0].reshape(D_MODEL, 2, 3, QW), (2, 1, 3, 0))
    w_kvt = w_kvt.reshape(3, 2 * QW, D_MODEL).astype(bf16)
    causal = jnp.tril(jnp.ones((CHUNK, CHUNK), bool))
    w_sp = jnp.where(causal[None], w_spatial[0], 0.0)
    w_sp_p = w_sp.astype(bf16)
    n_s = bs * T_NEW
    b_sp_p = b_spatial[0].T
    b_sp_s = jnp.tile(b_spatial[0][:, :T_NEW].T, (bs, 1))
    weights_tail = (w_proj_a[0].astype(bf16), w_proj_b[0].astype(bf16), w_out[0].astype(bf16))
    vecs = (g_pre, g_post, ln_v_g, ln_v_b)

    mod = _cond(jnp.concatenate([c_prompt, c_sample], axis=0), w_cond[0], b_cond)
    shift, scale, gate = mod[:, :D_MODEL], mod[:, D_MODEL:2 * D_MODEL], mod[:, 2 * D_MODEL:]
    mod_p = mod.reshape(bp + bs, 1, 3 * D_MODEL)
    ms = lambda a: jnp.repeat(a[bp:], T_NEW, axis=0)

    xs = x_sample.reshape(n_s, D_MODEL)
    qkv_s, *kv_s = _qkv_sample(xs, ms(scale), ms(shift), g_pre, w_in_b)
    caches_t = []
    for cache, win in zip((cache_kv_w128, cache_kv_w512, cache_kv_w2048), WINDOWS):
        caches_t.append(jnp.transpose(cache[0], (0, 2, 3, 4, 1)).reshape(bs, 2 * QW, win))

    o_all, m_all, l_all, *kvt = _attn_prompt(x_prompt, mod_p, g_pre, w_in_b, w_kvt)
    in_proj = (w_in_b, w_in_b, w_in_b)
    y_p, attn_s = _rest_prompt(x_prompt, mod_p, vecs,
                               in_proj + (w_sp_p, b_sp_p) + weights_tail, (o_all, m_all, l_all),
                               qkv_s.reshape(bs, T_NEW, 9 * QW), caches_t)
    kv_p = [jnp.transpose(a.reshape(bp, 2, N_HEADS, HEAD_DIM, win), (0, 4, 1, 2, 3))[None]
            for a, win in zip(kvt, WINDOWS)]

    attn_s = attn_s.reshape(n_s, QW)
    y_s, v_n_s = _rest_sample(xs, ms(scale), ms(shift), ms(gate), vecs,
                              in_proj + (w_sp_p, b_sp_s) + weights_tail, attn_s)
    kv_s = [a.reshape(1, bs, T_NEW, 2, N_HEADS, HEAD_DIM) for a in kv_s]

    return (y_p, y_s.reshape(bs, T_NEW, D_MODEL), kv_p[0], kv_p[1], kv_p[2],
            kv_s[0], kv_s[1], kv_s[2], v_n_s.reshape(1, bs, T_NEW, D_MODEL))
```

```python
import functools

import jax
import jax.numpy as jnp
from jax import lax
from jax.experimental import pallas as pl
from jax.experimental.pallas import tpu as pltpu

D_MODEL = 1024
SEQ = 2048
HEAD_DIM = 64
N_HEADS = 8
QW = N_HEADS * HEAD_DIM
WINDOWS = (128, 512, 2048)
DILATIONS = (1, 4, 16)
CHUNK = 128
A_GROUPS = 8
EPS = 1e-6
NEG = -1e30
LANES = 128
VMEM_LIMIT = 60 * 1024 * 1024
Q_COL0 = 3 * D_MODEL
K_COL0 = Q_COL0 + 3 * QW
V_COL0 = K_COL0 + 3 * QW
ZB_COL0 = V_COL0 + 3 * QW
GATE_COL0 = ZB_COL0 + QW
MOD_SHIFT, MOD_SCALE, MOD_GATE = 0, 1, 2

f32 = jnp.float32
bf16 = jnp.bfloat16


def _silu(x):
    return x * jax.nn.sigmoid(x)


def _norm_mod(x, g_pre, scale, shift):
    y = x * lax.rsqrt(jnp.mean(x * x, axis=-1, keepdims=True) + EPS) * g_pre
    return y * (1.0 + scale) + shift


def _dot(a, b):
    return jnp.dot(a, b, preferred_element_type=f32)


def _dot_nt(a, b):
    return lax.dot_general(a, b, (((1,), (1,)), ((), ())), preferred_element_type=f32)


def _cond_kernel(c_ref, w_ref, b_ref, o_ref):
    o_ref[...] = _dot(_silu(c_ref[...]).astype(bf16), w_ref[...].astype(bf16)) + b_ref[...]


def _cond(c_all, w_cond, b_cond):
    n = c_all.shape[0]
    return pl.pallas_call(
        _cond_kernel,
        out_shape=jax.ShapeDtypeStruct((n, 3 * D_MODEL), f32),
        name="cond",
        compiler_params=pltpu.CompilerParams(vmem_limit_bytes=VMEM_LIMIT),
    )(c_all, w_cond, b_cond)


ROWS_PER_STEP = 512
Q_BLOCK = 128
PACK_ROWS = 16
PERM_DIL = DILATIONS[-1]
PERM_TILE = PERM_DIL * PACK_ROWS
NORM_ROWS = PERM_TILE


def _attn_step(c, last_step, wq_ref, wk_ref, wv_ref, wkvt_ref, o_ref, m_ref, l_ref, kvt_ref,
               h_sc, hp_sc, k_sc, ve_sc, vo_sc, *, dil, window):
    seg = SEQ // dil
    n_lc = D_MODEL // LANES

    def rows_of(start, n):
        idx = pl.ds(pl.multiple_of(start, n), n) if dil == 1 else pl.ds(start, n, stride=dil)
        return jnp.concatenate([h_sc[k, idx, :] for k in range(n_lc)], axis=1)

    if dil == PERM_DIL:
        per = ROWS_PER_STEP // seg
        h = jnp.concatenate(
            [hp_sc[pl.ds(pl.multiple_of(PERM_TILE * t + PACK_ROWS * (per * c + i), PACK_ROWS), PACK_ROWS), :]
             for i in range(per) for t in range(SEQ // PERM_TILE)], axis=0)
    elif seg >= ROWS_PER_STEP:
        per = seg // ROWS_PER_STEP
        h = rows_of((c // per) + (c % per) * ROWS_PER_STEP * dil, ROWS_PER_STEP).astype(bf16)
    else:
        per = ROWS_PER_STEP // seg
        h = jnp.concatenate([rows_of(per * c + i, seg) for i in range(per)], axis=0).astype(bf16)
    base = pl.multiple_of(c * ROWS_PER_STEP, ROWS_PER_STEP)

    def tokens_t(start, n):
        return jnp.concatenate([h_sc[k, pl.ds(start, n), :] for k in range(n_lc)], axis=1).astype(bf16)

    if window == SEQ:
        kvt_ref[0] = _dot_nt(wkvt_ref[0], tokens_t(base, ROWS_PER_STEP))
    else:
        @pl.when(c == last_step)
        def _():
            kvt_ref[0] = _dot_nt(wkvt_ref[0], tokens_t(SEQ - window, window))

    low_w = (lax.broadcasted_iota(jnp.int32, (ROWS_PER_STEP, QW), 1) % LANES) < HEAD_DIM
    v = _dot(h, wv_ref[...])
    new_rows = pl.ds(Q_BLOCK + base, ROWS_PER_STEP)
    k_sc[new_rows, :] = _dot(h, wk_ref[...]).astype(bf16)
    ve_sc[new_rows, :] = jnp.where(low_w, v, 1.0).astype(bf16)
    vo_sc[new_rows, :] = jnp.where(low_w, 1.0, v).astype(bf16)
    low_q = (lax.broadcasted_iota(jnp.int32, (Q_BLOCK, QW), 1) % LANES) < HEAD_DIM

    def project_q(j):
        q = _dot(h[j * Q_BLOCK:(j + 1) * Q_BLOCK], wq_ref[...]) * (HEAD_DIM ** -0.5)
        return jnp.where(low_q, q, 0.0).astype(bf16), jnp.where(low_q, 0.0, q).astype(bf16)

    n_blocks = ROWS_PER_STEP // Q_BLOCK
    pairs = N_HEADS // 2
    qi = lax.broadcasted_iota(jnp.int32, (2 * Q_BLOCK, 2 * Q_BLOCK), 0) % Q_BLOCK
    kj = lax.broadcasted_iota(jnp.int32, (2 * Q_BLOCK, 2 * Q_BLOCK), 1)
    lane = lax.broadcasted_iota(jnp.int32, (Q_BLOCK, LANES), 1)
    low = lane < HEAD_DIM

    def has_prev(j):
        return seg > ROWS_PER_STEP or (j * Q_BLOCK) % seg != 0

    def key_rows(j):
        r0 = pl.multiple_of(base + j * Q_BLOCK, Q_BLOCK)
        return pl.ds(r0, 2 * Q_BLOCK) if has_prev(j) else pl.ds(r0 + Q_BLOCK, Q_BLOCK)

    def mask_of(j):
        if not has_prev(j):
            own = (2 * Q_BLOCK, Q_BLOCK)
            return (lax.broadcasted_iota(jnp.int32, own, 1)
                    <= lax.broadcasted_iota(jnp.int32, own, 0) % Q_BLOCK)
        off = jnp.where((base % seg) != 0, 0, Q_BLOCK) if (seg > ROWS_PER_STEP and j == 0) else 0
        in_prev = jnp.logical_and(kj < Q_BLOCK, kj >= qi + off)
        in_cur = jnp.logical_and(kj >= Q_BLOCK, kj - Q_BLOCK <= qi)
        return jnp.logical_or(in_prev, in_cur)

    def qk(j, q_even, q_odd):
        keys = key_rows(j)
        out = []
        for hp in range(pairs):
            cols = slice(hp * LANES, (hp + 1) * LANES)
            q_st = jnp.concatenate([q_even[:, cols], q_odd[:, cols]], axis=0)
            out.append(_dot_nt(q_st, k_sc[keys, cols]))
        return out

    def finish(j, scores):
        rows = slice(j * Q_BLOCK, (j + 1) * Q_BLOCK)
        keys = key_rows(j)
        mask = mask_of(j)
        ps, ms = [], []
        for hp in range(pairs):
            s = jnp.where(mask, scores[hp], NEG)
            for half in (s[:Q_BLOCK], s[Q_BLOCK:]):
                m = jnp.max(half, axis=-1, keepdims=True)
                ps.append(jnp.exp(half - m).astype(bf16))
                ms.append(m)
        o_parts = []
        m_tile = jnp.zeros((Q_BLOCK, LANES), f32)
        l_tile = jnp.ones((Q_BLOCK, LANES), f32)
        for hp in range(pairs):
            cols = slice(hp * LANES, (hp + 1) * LANES)
            acc_e = _dot(ps[2 * hp], ve_sc[keys, cols])
            acc_o = _dot(ps[2 * hp + 1], vo_sc[keys, cols])
            o_parts.append(jnp.where(low, acc_e, acc_o))
            l_tile = jnp.where(lane == HEAD_DIM + hp, acc_e, jnp.where(lane == hp, acc_o, l_tile))
            m_tile = jnp.where(lane == HEAD_DIM + hp, ms[2 * hp],
                               jnp.where(lane == hp, ms[2 * hp + 1], m_tile))
        o_ref[0, rows, :] = jnp.concatenate(o_parts, axis=1).astype(bf16)
        m_ref[0, rows, :] = m_tile
        l_ref[0, rows, :] = l_tile

    qs = {j: project_q(j) for j in range(min(2, n_blocks))}
    scores = qk(0, *qs[0])
    for j in range(n_blocks):
        nxt = qk(j + 1, *qs[j + 1]) if j + 1 < n_blocks else None
        if j + 2 < n_blocks:
            qs[j + 2] = project_q(j + 2)
        finish(j, scores)
        scores = nxt


def _attn_kernel(x_ref, scale_ref, shift_ref, gpre_ref, wq_ref, wk_ref, wv_ref, wkvt_ref,
                 o_ref, m_ref, l_ref, kvt0_ref, kvt1_ref, kvt2_ref, h_sc, hp_sc, k_sc, ve_sc, vo_sc):
    g = pl.program_id(1)
    c = pl.program_id(2)
    last_step = pl.num_programs(2) - 1

    @pl.when(jnp.logical_and(g == 0, c == 0))
    def _():
        zeros = jnp.zeros((Q_BLOCK, QW), bf16)
        k_sc[0:Q_BLOCK, :] = zeros
        ve_sc[0:Q_BLOCK, :] = zeros
        vo_sc[0:Q_BLOCK, :] = zeros
        r_out = lax.broadcasted_iota(jnp.int32, (PERM_TILE, PERM_TILE), 0)
        r_in = lax.broadcasted_iota(jnp.int32, (PERM_TILE, PERM_TILE), 1)
        perm = (r_in == PERM_DIL * (r_out % PACK_ROWS) + r_out // PACK_ROWS).astype(bf16)
        for rb in range(SEQ // NORM_ROWS):
            rs = slice(rb * NORM_ROWS, (rb + 1) * NORM_ROWS)
            hb = _norm_mod(x_ref[0, rs, :], gpre_ref[...], scale_ref[0], shift_ref[0])
            for k in range(D_MODEL // LANES):
                h_sc[k, rs, :] = hb[:, k * LANES:(k + 1) * LANES]
            hp_sc[rs, :] = _dot(perm, hb.astype(bf16)).astype(bf16)

    for pattern, kvt_ref in enumerate((kvt0_ref, kvt1_ref, kvt2_ref)):
        @pl.when(g == pattern)
        def _(pattern=pattern, kvt_ref=kvt_ref):
            _attn_step(c, last_step, wq_ref, wk_ref, wv_ref, wkvt_ref, o_ref, m_ref, l_ref, kvt_ref,
                       h_sc, hp_sc, k_sc, ve_sc, vo_sc, dil=DILATIONS[pattern], window=WINDOWS[pattern])


def _attn_prompt(x, mod, g_pre, w_in_b, w_kvt):
    b = x.shape[0]
    n_pat = len(DILATIONS)
    steps = SEQ // ROWS_PER_STEP
    full_window = WINDOWS.index(SEQ)
    per_b = lambda i, g, c: (i, 0, 0)
    step_rows = lambda i, g, c: (g, i, c, 0)

    def kvt_spec(pattern):
        if WINDOWS[pattern] == SEQ:
            return pl.BlockSpec((1, 2 * QW, ROWS_PER_STEP),
                                lambda i, g, c: (i, 0, jnp.where(g == full_window, c, 0)))
        return pl.BlockSpec((1, 2 * QW, WINDOWS[pattern]), per_b)

    return pl.pallas_call(
        _attn_kernel,
        grid=(b, n_pat, steps),
        in_specs=[
            pl.BlockSpec((1, SEQ, D_MODEL), per_b),
            pl.BlockSpec((1, 1, D_MODEL), lambda i, g, c: (i, 0, MOD_SCALE)),
            pl.BlockSpec((1, 1, D_MODEL), lambda i, g, c: (i, 0, MOD_SHIFT)),
            pl.BlockSpec((1, D_MODEL), lambda i, g, c: (0, 0)),
            pl.BlockSpec((D_MODEL, QW), lambda i, g, c: (0, Q_COL0 // QW + g)),
            pl.BlockSpec((D_MODEL, QW), lambda i, g, c: (0, K_COL0 // QW + g)),
            pl.BlockSpec((D_MODEL, QW), lambda i, g, c: (0, V_COL0 // QW + g)),
            pl.BlockSpec((1, 2 * QW, D_MODEL), lambda i, g, c: (g, 0, 0)),
        ],
        out_specs=[
            pl.BlockSpec((None, 1, ROWS_PER_STEP, QW), step_rows),
            pl.BlockSpec((None, 1, ROWS_PER_STEP, LANES), step_rows),
            pl.BlockSpec((None, 1, ROWS_PER_STEP, LANES), step_rows),
        ] + [kvt_spec(p) for p in range(n_pat)],
        out_shape=[
            jax.ShapeDtypeStruct((n_pat, b, SEQ, QW), bf16),
            jax.ShapeDtypeStruct((n_pat, b, SEQ, LANES), f32),
            jax.ShapeDtypeStruct((n_pat, b, SEQ, LANES), f32),
        ] + [jax.ShapeDtypeStruct((b, 2 * QW, win), f32) for win in WINDOWS],
        scratch_shapes=[pltpu.VMEM((D_MODEL // LANES, SEQ, LANES), f32),
                        pltpu.VMEM((SEQ, D_MODEL), bf16),
                        pltpu.VMEM((SEQ + Q_BLOCK, QW), bf16), pltpu.VMEM((SEQ + Q_BLOCK, QW), bf16),
                        pltpu.VMEM((SEQ + Q_BLOCK, QW), bf16)],
        name="attn_p",
        compiler_params=pltpu.CompilerParams(
            dimension_semantics=("arbitrary", "arbitrary", "arbitrary"), vmem_limit_bytes=VMEM_LIMIT),
    )(x, mod, mod, g_pre, w_in_b, w_in_b, w_in_b, w_kvt)


def _qkv_s_kernel(x_ref, scale_ref, shift_ref, gpre_ref, w_ref, o_ref, *kv_refs):
    h = _norm_mod(x_ref[...], gpre_ref[...], scale_ref[...], shift_ref[...]).astype(bf16)
    tile = _dot(h, w_ref[...])
    o_ref[...] = tile
    for half in range(2):
        @pl.when(pl.program_id(0) == 1 + half)
        def _(half=half):
            for g, kv_ref in enumerate(kv_refs):
                kv_ref[:, half * QW:(half + 1) * QW] = tile[:, g * QW:(g + 1) * QW]


def _qkv_sample(x, scale, shift, g_pre, w_in_b):
    n = x.shape[0]
    n_pat = len(DILATIONS)
    cols = 3 * n_pat * QW
    tile = n_pat * QW
    first = Q_COL0 // tile
    return pl.pallas_call(
        _qkv_s_kernel,
        grid=(cols // tile,),
        in_specs=[
            pl.BlockSpec((n, D_MODEL), lambda j: (0, 0)),
            pl.BlockSpec((n, D_MODEL), lambda j: (0, 0)),
            pl.BlockSpec((n, D_MODEL), lambda j: (0, 0)),
            pl.BlockSpec((1, D_MODEL), lambda j: (0, 0)),
            pl.BlockSpec((D_MODEL, tile), lambda j: (0, first + j)),
        ],
        out_specs=[pl.BlockSpec((n, tile), lambda j: (0, j))]
        + [pl.BlockSpec((n, 2 * QW), lambda j: (0, 0))] * n_pat,
        out_shape=[jax.ShapeDtypeStruct((n, cols), f32)]
        + [jax.ShapeDtypeStruct((n, 2 * QW), f32)] * n_pat,
        name="qkv_s",
        compiler_params=pltpu.CompilerParams(
            dimension_semantics=("arbitrary",), vmem_limit_bytes=VMEM_LIMIT),
    )(x, scale, shift, g_pre, w_in_b)


T_NEW = 8


def _sample_attention(qkv_ref, c0_ref, c1_ref, c2_ref, o_ref, new_sc):
    qkv = qkv_ref[0]
    new_sc[0:T_NEW, :] = qkv[:, 3 * QW:]
    q_all = qkv[:, :3 * QW] * (HEAD_DIM ** -0.5)

    lane8 = lax.broadcasted_iota(jnp.int32, (2 * T_NEW, LANES), 1)
    low = lane8 < HEAD_DIM
    t_idx = lax.broadcasted_iota(jnp.int32, (2 * T_NEW, LANES), 0) % T_NEW
    caches = (c0_ref, c1_ref, c2_ref)
    pairs = range(N_HEADS // 2)
    pats = range(len(DILATIONS))

    def pair_cols(g, hp, base=0):
        return slice(base + g * QW + hp * LANES, base + g * QW + (hp + 1) * LANES)

    s_cache, s_new = {}, {}
    for hp in pairs:
        for g in pats:
            q2 = q_all[:, pair_cols(g, hp)]
            q_st = jnp.concatenate([jnp.where(low[:T_NEW], q2, 0.0), jnp.where(low[:T_NEW], 0.0, q2)],
                                   axis=0).astype(bf16)
            kt = caches[g][0, hp * LANES:(hp + 1) * LANES, :].astype(bf16)
            k_new = new_sc[:, pair_cols(g, hp)].astype(bf16)
            s_cache[hp, g] = _dot(q_st, kt)
            s_new[hp, g] = _dot_nt(q_st, k_new)

    probs = {}
    for hp in pairs:
        m = jnp.full((2 * T_NEW, 1), NEG, f32)
        for g, dil in enumerate(DILATIONS):
            rows = WINDOWS[g]
            rho = lax.broadcasted_iota(jnp.int32, (2 * T_NEW, rows), 1)
            tq = lax.broadcasted_iota(jnp.int32, (2 * T_NEW, rows), 0) % T_NEW
            ok_c = jnp.logical_and(rho >= tq, ((rho - tq) & (dil - 1)) == 0)
            ok_n = jnp.logical_and(lane8 <= t_idx, ((t_idx - lane8) & (dil - 1)) == 0)
            s_cache[hp, g] = jnp.where(ok_c, s_cache[hp, g], NEG)
            s_new[hp, g] = jnp.where(ok_n, s_new[hp, g], NEG)
            m = jnp.maximum(m, jnp.maximum(jnp.max(s_cache[hp, g], axis=-1, keepdims=True),
                                           jnp.max(s_new[hp, g], axis=-1, keepdims=True)))
        l = jnp.zeros((2 * T_NEW, 1), f32)
        for g in pats:
            p_c = jnp.exp(s_cache[hp, g] - m)
            p_n = jnp.exp(s_new[hp, g] - m)
            l = l + jnp.sum(p_c, axis=-1, keepdims=True) + jnp.sum(p_n, axis=-1, keepdims=True)
            probs[hp, g] = (p_c.astype(bf16), p_n.astype(bf16))
        probs[hp] = l

    o_parts = []
    for hp in pairs:
        acc = jnp.zeros((2 * T_NEW, LANES), f32)
        for g in pats:
            vt = caches[g][0, QW + hp * LANES: QW + (hp + 1) * LANES, :].astype(bf16)
            v_new = new_sc[:, pair_cols(g, hp, 3 * QW)].astype(bf16)
            p_c, p_n = probs[hp, g]
            acc = acc + _dot_nt(p_c, vt) + _dot(p_n, v_new)
        out = acc / probs[hp]
        o_parts.append(jnp.where(low[:T_NEW], out[:T_NEW], out[T_NEW:]))
    o_ref[0] = jnp.concatenate(o_parts, axis=1)


TILE = 256


def _expand_heads(w):
    r = lax.broadcasted_iota(jnp.int32, (2 * LANES, QW), 0) % LANES
    head = lax.broadcasted_iota(jnp.int32, (2 * LANES, QW), 1) // HEAD_DIM
    expand = (r == head // 2 + HEAD_DIM * (1 - head % 2)).astype(bf16)
    hi = w.astype(bf16)
    lo = (w - hi.astype(f32)).astype(bf16)
    return _dot(jnp.concatenate([hi, lo], axis=1), expand)


def _rest_kernel(*refs, sample, sample_every=None):
    if sample:
        (x_ref, scale_ref, shift_ref, gate_ref, gpre_ref, gpost_ref, lng_ref, lnb_ref,
         wa_ref, wzb_ref, wg_ref, wsp_ref, bsp_ref, wpa_ref, wpb_ref, wout_ref,
         attn_ref, y_ref, vn_ref) = refs
        x = x_ref[...]
        scale, shift, gate = scale_ref[...], shift_ref[...], gate_ref[...]
    else:
        (x_ref, scale_ref, shift_ref, gate_ref, gpre_ref, gpost_ref, lng_ref, lnb_ref,
         wa_ref, wzb_ref, wg_ref, wsp_ref, bsp_ref, wpa_ref, wpb_ref, wout_ref,
         o1_ref, m1_ref, l1_ref, o4_ref, m4_ref, l4_ref, o16_ref, m16_ref, l16_ref,
         qkv_s_ref, c0_ref, c1_ref, c2_ref, y_ref, attn_s_ref,
         o4_sc, m4_sc, l4_sc, o16_sc, m16_sc, l16_sc, new_sc) = refs
        x = x_ref[0]
        scale, shift, gate = scale_ref[0], shift_ref[0], gate_ref[0]

        @pl.when(jnp.logical_and(pl.program_id(0) == 0, pl.program_id(1) == 0))
        def _():
            new_sc[...] = jnp.zeros_like(new_sc)

        @pl.when(pl.program_id(1) % sample_every == 0)
        def _():
            _sample_attention(qkv_s_ref, c0_ref, c1_ref, c2_ref, attn_s_ref, new_sc)

    rows = x.shape[0]
    h = _norm_mod(x, gpre_ref[...], scale, shift).astype(bf16)

    pa = _dot(h, wa_ref[...])
    u_a, v_a, z_a = pa[:, :D_MODEL], pa[:, D_MODEL:2 * D_MODEL], pa[:, 2 * D_MODEL:]
    mu = jnp.mean(v_a, axis=-1, keepdims=True)
    cen = v_a - mu
    var = jnp.mean(cen * cen, axis=-1, keepdims=True)
    v_n = cen * lax.rsqrt(var + EPS) * lng_ref[...] + lnb_ref[...]
    v_nb = v_n.astype(bf16)
    if sample:
        vn_ref[...] = v_n
        pick = (lax.broadcasted_iota(jnp.int32, (rows, CHUNK), 1)
                == lax.broadcasted_iota(jnp.int32, (rows, CHUNK), 0) % T_NEW).astype(bf16)
        same_seq = (lax.broadcasted_iota(jnp.int32, (rows, rows), 0) // T_NEW
                    == lax.broadcasted_iota(jnp.int32, (rows, rows), 1) // T_NEW)
        cols = []
        for g in range(A_GROUPS):
            tiled = _dot_nt(_dot(pick, wsp_ref[g]).astype(bf16), pick)
            w_blk = jnp.where(same_seq, tiled, 0.0).astype(bf16)
            cols.append(_dot(w_blk, v_nb[:, g * LANES:(g + 1) * LANES]) + bsp_ref[:, g:g + 1])
        zs = jnp.concatenate(cols, axis=1)
    else:
        n_ck = rows // CHUNK
        per_group = []
        for g in range(A_GROUPS):
            rhs = jnp.concatenate(
                [v_nb[ck * CHUNK:(ck + 1) * CHUNK, g * LANES:(g + 1) * LANES] for ck in range(n_ck)], axis=1)
            per_group.append(_dot(wsp_ref[g], rhs) + bsp_ref[:, g:g + 1])
        zs = jnp.concatenate(
            [jnp.concatenate([pg[:, ck * LANES:(ck + 1) * LANES] for pg in per_group], axis=1)
             for ck in range(n_ck)], axis=0)
    y_a = u_a * zs * _silu(z_a)

    if sample:
        attn = attn_ref[...]
    else:
        for osc, msc, lsc, oref, mref, lref, dil in (
                (o4_sc, m4_sc, l4_sc, o4_ref, m4_ref, l4_ref, DILATIONS[1]),
                (o16_sc, m16_sc, l16_sc, o16_ref, m16_ref, l16_ref, DILATIONS[2])):
            n = rows // dil
            for r in range(dil):
                o_r = oref[0, r].astype(f32)
                for k in range(QW // LANES):
                    osc[k, pl.ds(r, n, stride=dil), :] = o_r[:, k * LANES:(k + 1) * LANES]
                msc[pl.ds(r, n, stride=dil), :] = mref[0, r]
                lsc[pl.ds(r, n, stride=dil), :] = lref[0, r]
        ms = (m1_ref[0], m4_sc[...], m16_sc[...])
        ls = (l1_ref[0], l4_sc[...], l16_sc[...])
        unchunk = lambda sc: jnp.concatenate([sc[k] for k in range(QW // LANES)], axis=1)
        outs = (o1_ref[0].astype(f32), unchunk(o4_sc), unchunk(o16_sc))
        m_all = jnp.maximum(jnp.maximum(ms[0], ms[1]), ms[2])
        ws = [jnp.exp(m - m_all) for m in ms]
        den = ws[0] * ls[0] + ws[1] * ls[1] + ws[2] * ls[2]
        attn = sum(_expand_heads(w / den) * o for w, o in zip(ws, outs))

    z_b = _dot(h, wzb_ref[...])
    y_b = (attn * _silu(z_b)).astype(bf16)

    gl = _dot(h, wg_ref[...])
    p_a = _dot(y_a.astype(bf16), wpa_ref[...])
    p_b = _dot(y_b, wpb_ref[...])
    merged = jax.nn.sigmoid(gl[:, :D_MODEL]) * p_a + jax.nn.sigmoid(gl[:, D_MODEL:]) * p_b
    out = _dot(merged.astype(bf16), wout_ref[...])
    normed = out * lax.rsqrt(jnp.mean(out * out, axis=-1, keepdims=True) + EPS) * gpost_ref[...]
    y = x + gate * normed
    if sample:
        y_ref[...] = y
    else:
        y_ref[0] = y


def _full(shape):
    nd = len(shape)
    return pl.BlockSpec(shape, lambda *_: (0,) * nd)


def _weight_specs(weights):
    in_proj = [pl.BlockSpec((D_MODEL, 3 * D_MODEL), lambda *_: (0, 0)),
               pl.BlockSpec((D_MODEL, QW), lambda *_: (0, ZB_COL0 // QW)),
               pl.BlockSpec((D_MODEL, 2 * D_MODEL), lambda *_: (0, GATE_COL0 // (2 * D_MODEL)))]
    return in_proj + [_full(w.shape) for w in weights[3:]]


def _rest_prompt(x, mod, vecs, weights, attn_parts, qkv_s, caches):
    b = x.shape[0]
    n_pat = len(DILATIONS)
    tiles = SEQ // TILE
    n_seq = qkv_s.shape[0]
    sample_every, rem = divmod(b * tiles, n_seq)
    assert rem == 0 and tiles % sample_every == 0, "sample sequences must tile the grid evenly"
    seq_of = lambda i, c: (i * (tiles // sample_every) + c // sample_every, 0, 0)
    tile3 = lambda i, c: (i, c, 0)
    part_args, part_specs, scratch = [], [], []
    for pattern, dil in enumerate(DILATIONS):
        for a in attn_parts:
            lanes = a.shape[-1]
            if dil == 1:
                part_args.append(a)
                part_specs.append(pl.BlockSpec((None, 1, TILE, lanes),
                                               lambda i, c, p=pattern: (p, i, c, 0)))
            else:
                part_args.append(a.reshape(n_pat, b, dil, SEQ // dil, lanes))
                part_specs.append(pl.BlockSpec((None, 1, dil, TILE // dil, lanes),
                                               lambda i, c, p=pattern: (p, i, 0, c, 0)))
        if dil != 1:
            scratch += [pltpu.VMEM((QW // LANES, TILE, LANES), f32),
                        pltpu.VMEM((TILE, LANES), f32), pltpu.VMEM((TILE, LANES), f32)]
    in_specs = (
        [pl.BlockSpec((1, TILE, D_MODEL), tile3)]
        + [pl.BlockSpec((1, 1, D_MODEL), lambda i, c, col=col: (i, 0, col))
           for col in (MOD_SCALE, MOD_SHIFT, MOD_GATE)]
        + [_full(v.shape) for v in vecs] + _weight_specs(weights) + part_specs
        + [pl.BlockSpec((1, T_NEW, 9 * QW), seq_of)]
        + [pl.BlockSpec((1, 2 * QW, win), seq_of) for win in WINDOWS])
    scratch.append(pltpu.VMEM((LANES, 6 * QW), f32))
    return pl.pallas_call(
        functools.partial(_rest_kernel, sample=False, sample_every=sample_every),
        grid=(b, tiles),
        in_specs=in_specs,
        out_specs=[pl.BlockSpec((1, TILE, D_MODEL), tile3), pl.BlockSpec((1, T_NEW, QW), seq_of)],
        out_shape=[jax.ShapeDtypeStruct((b, SEQ, D_MODEL), f32),
                   jax.ShapeDtypeStruct((n_seq, T_NEW, QW), f32)],
        scratch_shapes=scratch,
        name="rest_p",
        compiler_params=pltpu.CompilerParams(
            dimension_semantics=("arbitrary", "arbitrary"), vmem_limit_bytes=VMEM_LIMIT),
    )(x, mod, mod, mod, *vecs, *weights, *part_args, qkv_s, *caches)


def _rest_sample(x, scale, shift, gate, vecs, weights, attn):
    n = x.shape[0]
    args = (x, scale, shift, gate, *vecs, *weights, attn)
    in_specs = ([_full(a.shape) for a in (x, scale, shift, gate, *vecs)] + _weight_specs(weights)
                + [_full(attn.shape)])
    return pl.pallas_call(
        functools.partial(_rest_kernel, sample=True),
        in_specs=in_specs,
        out_specs=[_full((n, D_MODEL)), _full((n, D_MODEL))],
        out_shape=[jax.ShapeDtypeStruct((n, D_MODEL), f32), jax.ShapeDtypeStruct((n, D_MODEL), f32)],
        grid=(1,),
        name="rest_s",
        compiler_params=pltpu.CompilerParams(
            dimension_semantics=("arbitrary",), vmem_limit_bytes=VMEM_LIMIT),
    )(*args)


def kernel(x_prompt, x_sample, cache_kv_w128, cache_kv_w512, cache_kv_w2048, c_prompt, c_sample, w_cond, b_cond, g_pre, w_in, ln_v_g, ln_v_b, w_spatial, b_spatial, w_proj_a, w_proj_b, w_out, g_post):
    assert w_in.shape[0] == 1, "single layer"
    bp, seq, _ = x_prompt.shape
    bs, t_new, _ = x_sample.shape
    assert seq == SEQ and t_new == T_NEW

    w_in_b = w_in[0].astype(bf16)
    w_kvt = jnp.transpose(w_in[0][:, K_COL0:ZB_COL0].reshape(D_MODEL, 2, 3, QW), (2, 1, 3, 0))
    w_kvt = w_kvt.reshape(3, 2 * QW, D_MODEL).astype(bf16)
    causal = jnp.tril(jnp.ones((CHUNK, CHUNK), bool))
    w_sp = jnp.where(causal[None], w_spatial[0], 0.0)
    w_sp_p = w_sp.astype(bf16)
    n_s = bs * T_NEW
    b_sp_p = b_spatial[0].T
    b_sp_s = jnp.tile(b_spatial[0][:, :T_NEW].T, (bs, 1))
    weights_tail = (w_proj_a[0].astype(bf16), w_proj_b[0].astype(bf16), w_out[0].astype(bf16))
    vecs = (g_pre, g_post, ln_v_g, ln_v_b)

    mod = _cond(jnp.concatenate([c_prompt, c_sample], axis=0), w_cond[0], b_cond)
    shift, scale, gate = mod[:, :D_MODEL], mod[:, D_MODEL:2 * D_MODEL], mod[:, 2 * D_MODEL:]
    mod_p = mod.reshape(bp + bs, 1, 3 * D_MODEL)
    ms = lambda a: jnp.repeat(a[bp:], T_NEW, axis=0)

    xs = x_sample.reshape(n_s, D_MODEL)
    qkv_s, *kv_s = _qkv_sample(xs, ms(scale), ms(shift), g_pre, w_in_b)
    caches_t = []
    for cache, win in zip((cache_kv_w128, cache_kv_w512, cache_kv_w2048), WINDOWS):
        caches_t.append(jnp.transpose(cache[0], (0, 2, 3, 4, 1)).reshape(bs, 2 * QW, win))

    o_all, m_all, l_all, *kvt = _attn_prompt(x_prompt, mod_p, g_pre, w_in_b, w_kvt)
    in_proj = (w_in_b, w_in_b, w_in_b)
    y_p, attn_s = _rest_prompt(x_prompt, mod_p, vecs,
                               in_proj + (w_sp_p, b_sp_p) + weights_tail, (o_all, m_all, l_all),
                               qkv_s.reshape(bs, T_NEW, 9 * QW), caches_t)
    kv_p = [jnp.transpose(a.reshape(bp, 2, N_HEADS, HEAD_DIM, win), (0, 4, 1, 2, 3))[None]
            for a, win in zip(kvt, WINDOWS)]

    attn_s = attn_s.reshape(n_s, QW)
    y_s, v_n_s = _rest_sample(xs, ms(scale), ms(shift), ms(gate), vecs,
                              in_proj + (w_sp_p, b_sp_s) + weights_tail, attn_s)
    kv_s = [a.reshape(1, bs, T_NEW, 2, N_HEADS, HEAD_DIM) for a in kv_s]

    return (y_p, y_s.reshape(bs, T_NEW, D_MODEL), kv_p[0], kv_p[1], kv_p[2],
            kv_s[0], kv_s[1], kv_s[2], v_n_s.reshape(1, bs, T_NEW, D_MODEL))
```

```python
import functools

import jax
import jax.numpy as jnp
from jax import lax
from jax.experimental import pallas as pl
from jax.experimental.pallas import tpu as pltpu

D_MODEL = 1024
SEQ = 2048
HEAD_DIM = 64
N_HEADS = 8
QW = N_HEADS * HEAD_DIM
WINDOWS = (128, 512, 2048)
DILATIONS = (1, 4, 16)
CHUNK = 128
A_GROUPS = 8
EPS = 1e-6
NEG = -1e30
LANES = 128
VMEM_LIMIT = 56 * 1024 * 1024
Q_COL0 = 3 * D_MODEL
K_COL0 = Q_COL0 + 3 * QW
V_COL0 = K_COL0 + 3 * QW
ZB_COL0 = V_COL0 + 3 * QW
GATE_COL0 = ZB_COL0 + QW
MOD_SHIFT, MOD_SCALE, MOD_GATE = 0, 1, 2

f32 = jnp.float32
bf16 = jnp.bfloat16


def _silu(x):
    return x * jax.nn.sigmoid(x)


def _norm_mod(x, g_pre, scale, shift):
    y = x * lax.rsqrt(jnp.mean(x * x, axis=-1, keepdims=True) + EPS) * g_pre
    return y * (1.0 + scale) + shift


def _dot(a, b):
    return jnp.dot(a, b, preferred_element_type=f32)


def _dot_nt(a, b):
    return lax.dot_general(a, b, (((1,), (1,)), ((), ())), preferred_element_type=f32)


def _cond_kernel(c_ref, w_ref, b_ref, o_ref):
    o_ref[...] = _dot(_silu(c_ref[...]).astype(bf16), w_ref[...].astype(bf16)) + b_ref[...]


def _cond(c_all, w_cond, b_cond):
    n = c_all.shape[0]
    return pl.pallas_call(
        _cond_kernel,
        out_shape=jax.ShapeDtypeStruct((n, 3 * D_MODEL), f32),
        name="cond",
        compiler_params=pltpu.CompilerParams(vmem_limit_bytes=VMEM_LIMIT),
    )(c_all, w_cond, b_cond)


ROWS_PER_STEP = 512
Q_BLOCK = 128
NORM_ROWS = 256


def _attn_step(c, last_step, wq_ref, wk_ref, wv_ref, wkvt_ref, o_ref, m_ref, l_ref, kvt_ref,
               h_sc, k_sc, ve_sc, vo_sc, *, dil, window):
    seg = SEQ // dil
    n_lc = D_MODEL // LANES

    def rows_of(start, n):
        idx = pl.ds(pl.multiple_of(start, n), n) if dil == 1 else pl.ds(start, n, stride=dil)
        return jnp.concatenate([h_sc[k, idx, :] for k in range(n_lc)], axis=1)

    if seg >= ROWS_PER_STEP:
        per = seg // ROWS_PER_STEP
        h = rows_of((c // per) + (c % per) * ROWS_PER_STEP * dil, ROWS_PER_STEP)
    else:
        per = ROWS_PER_STEP // seg
        h = jnp.concatenate([rows_of(per * c + i, seg) for i in range(per)], axis=0)
    h = h.astype(bf16)
    base = pl.multiple_of(c * ROWS_PER_STEP, ROWS_PER_STEP)

    def tokens_t(start, n):
        return jnp.concatenate([h_sc[k, pl.ds(start, n), :] for k in range(n_lc)], axis=1).astype(bf16)

    if window == SEQ:
        kvt_ref[0] = _dot_nt(wkvt_ref[0], tokens_t(base, ROWS_PER_STEP))
    else:
        @pl.when(c == last_step)
        def _():
            kvt_ref[0] = _dot_nt(wkvt_ref[0], tokens_t(SEQ - window, window))

    low_w = (lax.broadcasted_iota(jnp.int32, (ROWS_PER_STEP, QW), 1) % LANES) < HEAD_DIM
    new_rows = pl.ds(Q_BLOCK + base, ROWS_PER_STEP)
    k_sc[new_rows, :] = _dot(h, wk_ref[...]).astype(bf16)

    def project_v():
        v = _dot(h, wv_ref[...])
        ve_sc[new_rows, :] = jnp.where(low_w, v, 1.0).astype(bf16)
        vo_sc[new_rows, :] = jnp.where(low_w, 1.0, v).astype(bf16)

    low_q = (lax.broadcasted_iota(jnp.int32, (Q_BLOCK, QW), 1) % LANES) < HEAD_DIM

    def project_q(j):
        q = _dot(h[j * Q_BLOCK:(j + 1) * Q_BLOCK], wq_ref[...]) * (HEAD_DIM ** -0.5)
        return jnp.where(low_q, q, 0.0).astype(bf16), jnp.where(low_q, 0.0, q).astype(bf16)

    n_blocks = ROWS_PER_STEP // Q_BLOCK
    pairs = N_HEADS // 2
    qi = lax.broadcasted_iota(jnp.int32, (2 * Q_BLOCK, 2 * Q_BLOCK), 0) % Q_BLOCK
    kj = lax.broadcasted_iota(jnp.int32, (2 * Q_BLOCK, 2 * Q_BLOCK), 1)
    lane = lax.broadcasted_iota(jnp.int32, (Q_BLOCK, LANES), 1)
    low = lane < HEAD_DIM

    def has_prev(j):
        return seg > ROWS_PER_STEP or (j * Q_BLOCK) % seg != 0

    def key_rows(j):
        r0 = pl.multiple_of(base + j * Q_BLOCK, Q_BLOCK)
        return pl.ds(r0, 2 * Q_BLOCK) if has_prev(j) else pl.ds(r0 + Q_BLOCK, Q_BLOCK)

    def mask_of(j):
        if not has_prev(j):
            own = (2 * Q_BLOCK, Q_BLOCK)
            return (lax.broadcasted_iota(jnp.int32, own, 1)
                    <= lax.broadcasted_iota(jnp.int32, own, 0) % Q_BLOCK)
        off = jnp.where((base % seg) != 0, 0, Q_BLOCK) if (seg > ROWS_PER_STEP and j == 0) else 0
        in_prev = jnp.logical_and(kj < Q_BLOCK, kj >= qi + off)
        in_cur = jnp.logical_and(kj >= Q_BLOCK, kj - Q_BLOCK <= qi)
        return jnp.logical_or(in_prev, in_cur)

    def qk(j, q_even, q_odd):
        keys = key_rows(j)
        out = []
        for hp in range(pairs):
            cols = slice(hp * LANES, (hp + 1) * LANES)
            q_st = jnp.concatenate([q_even[:, cols], q_odd[:, cols]], axis=0)
            out.append(_dot_nt(q_st, k_sc[keys, cols]))
        return out

    def finish(j, scores):
        rows = slice(j * Q_BLOCK, (j + 1) * Q_BLOCK)
        keys = key_rows(j)
        mask = mask_of(j)
        ps, ms = [], []
        for hp in range(pairs):
            s = jnp.where(mask, scores[hp], NEG)
            for half in (s[:Q_BLOCK], s[Q_BLOCK:]):
                m = jnp.max(half, axis=-1, keepdims=True)
                ps.append(jnp.exp(half - m).astype(bf16))
                ms.append(m)
        o_parts = []
        m_tile = jnp.zeros((Q_BLOCK, LANES), f32)
        l_tile = jnp.ones((Q_BLOCK, LANES), f32)
        for hp in range(pairs):
            cols = slice(hp * LANES, (hp + 1) * LANES)
            acc_e = _dot(ps[2 * hp], ve_sc[keys, cols])
            acc_o = _dot(ps[2 * hp + 1], vo_sc[keys, cols])
            o_parts.append(jnp.where(low, acc_e, acc_o))
            l_tile = jnp.where(lane == HEAD_DIM + hp, acc_e, jnp.where(lane == hp, acc_o, l_tile))
            m_tile = jnp.where(lane == HEAD_DIM + hp, ms[2 * hp],
                               jnp.where(lane == hp, ms[2 * hp + 1], m_tile))
        o_ref[0, rows, :] = jnp.concatenate(o_parts, axis=1).astype(bf16)
        m_ref[0, rows, :] = m_tile
        l_ref[0, rows, :] = l_tile

    qs = {j: project_q(j) for j in range(min(2, n_blocks))}
    scores = qk(0, *qs[0])
    project_v()
    for j in range(n_blocks):
        nxt = qk(j + 1, *qs[j + 1]) if j + 1 < n_blocks else None
        if j + 2 < n_blocks:
            qs[j + 2] = project_q(j + 2)
        finish(j, scores)
        scores = nxt


def _attn_kernel(x_ref, scale_ref, shift_ref, gpre_ref, wq_ref, wk_ref, wv_ref, wkvt_ref,
                 o_ref, m_ref, l_ref, kvt0_ref, kvt1_ref, kvt2_ref, h_sc, k_sc, ve_sc, vo_sc):
    g = pl.program_id(1)
    c = pl.program_id(2)
    last_step = pl.num_programs(2) - 1

    @pl.when(jnp.logical_and(g == 0, c == 0))
    def _():
        zeros = jnp.zeros((Q_BLOCK, QW), bf16)
        k_sc[0:Q_BLOCK, :] = zeros
        ve_sc[0:Q_BLOCK, :] = zeros
        vo_sc[0:Q_BLOCK, :] = zeros
        for rb in range(SEQ // NORM_ROWS):
            rs = slice(rb * NORM_ROWS, (rb + 1) * NORM_ROWS)
            hb = _norm_mod(x_ref[0, rs, :], gpre_ref[...], scale_ref[0], shift_ref[0])
            for k in range(D_MODEL // LANES):
                h_sc[k, rs, :] = hb[:, k * LANES:(k + 1) * LANES]

    for pattern, kvt_ref in enumerate((kvt0_ref, kvt1_ref, kvt2_ref)):
        @pl.when(g == pattern)
        def _(pattern=pattern, kvt_ref=kvt_ref):
            _attn_step(c, last_step, wq_ref, wk_ref, wv_ref, wkvt_ref, o_ref, m_ref, l_ref, kvt_ref,
                       h_sc, k_sc, ve_sc, vo_sc, dil=DILATIONS[pattern], window=WINDOWS[pattern])


def _attn_prompt(x, mod, g_pre, w_in_b, w_kvt):
    b = x.shape[0]
    n_pat = len(DILATIONS)
    steps = SEQ // ROWS_PER_STEP
    full_window = WINDOWS.index(SEQ)
    per_b = lambda i, g, c: (i, 0, 0)
    step_rows = lambda i, g, c: (g, i, c, 0)

    def kvt_spec(pattern):
        if WINDOWS[pattern] == SEQ:
            return pl.BlockSpec((1, 2 * QW, ROWS_PER_STEP),
                                lambda i, g, c: (i, 0, jnp.where(g == full_window, c, 0)))
        return pl.BlockSpec((1, 2 * QW, WINDOWS[pattern]), per_b)

    return pl.pallas_call(
        _attn_kernel,
        grid=(b, n_pat, steps),
        in_specs=[
            pl.BlockSpec((1, SEQ, D_MODEL), per_b),
            pl.BlockSpec((1, 1, D_MODEL), lambda i, g, c: (i, 0, MOD_SCALE)),
            pl.BlockSpec((1, 1, D_MODEL), lambda i, g, c: (i, 0, MOD_SHIFT)),
            pl.BlockSpec((1, D_MODEL), lambda i, g, c: (0, 0)),
            pl.BlockSpec((D_MODEL, QW), lambda i, g, c: (0, Q_COL0 // QW + g)),
            pl.BlockSpec((D_MODEL, QW), lambda i, g, c: (0, K_COL0 // QW + g)),
            pl.BlockSpec((D_MODEL, QW), lambda i, g, c: (0, V_COL0 // QW + g)),
            pl.BlockSpec((1, 2 * QW, D_MODEL), lambda i, g, c: (g, 0, 0)),
        ],
        out_specs=[
            pl.BlockSpec((None, 1, ROWS_PER_STEP, QW), step_rows),
            pl.BlockSpec((None, 1, ROWS_PER_STEP, LANES), step_rows),
            pl.BlockSpec((None, 1, ROWS_PER_STEP, LANES), step_rows),
        ] + [kvt_spec(p) for p in range(n_pat)],
        out_shape=[
            jax.ShapeDtypeStruct((n_pat, b, SEQ, QW), bf16),
            jax.ShapeDtypeStruct((n_pat, b, SEQ, LANES), f32),
            jax.ShapeDtypeStruct((n_pat, b, SEQ, LANES), f32),
        ] + [jax.ShapeDtypeStruct((b, 2 * QW, win), f32) for win in WINDOWS],
        scratch_shapes=[pltpu.VMEM((D_MODEL // LANES, SEQ, LANES), f32),
                        pltpu.VMEM((SEQ + Q_BLOCK, QW), bf16), pltpu.VMEM((SEQ + Q_BLOCK, QW), bf16),
                        pltpu.VMEM((SEQ + Q_BLOCK, QW), bf16)],
        name="attn_p",
        compiler_params=pltpu.CompilerParams(
            dimension_semantics=("arbitrary", "arbitrary", "arbitrary"), vmem_limit_bytes=VMEM_LIMIT),
    )(x, mod, mod, g_pre, w_in_b, w_in_b, w_in_b, w_kvt)


def _qkv_s_kernel(x_ref, scale_ref, shift_ref, gpre_ref, w_ref, o_ref, *kv_refs):
    h = _norm_mod(x_ref[...], gpre_ref[...], scale_ref[...], shift_ref[...]).astype(bf16)
    tile = _dot(h, w_ref[...])
    o_ref[...] = tile
    for half in range(2):
        @pl.when(pl.program_id(0) == 1 + half)
        def _(half=half):
            for g, kv_ref in enumerate(kv_refs):
                kv_ref[:, half * QW:(half + 1) * QW] = tile[:, g * QW:(g + 1) * QW]


def _qkv_sample(x, scale, shift, g_pre, w_in_b):
    n = x.shape[0]
    n_pat = len(DILATIONS)
    cols = 3 * n_pat * QW
    tile = n_pat * QW
    first = Q_COL0 // tile
    return pl.pallas_call(
        _qkv_s_kernel,
        grid=(cols // tile,),
        in_specs=[
            pl.BlockSpec((n, D_MODEL), lambda j: (0, 0)),
            pl.BlockSpec((n, D_MODEL), lambda j: (0, 0)),
            pl.BlockSpec((n, D_MODEL), lambda j: (0, 0)),
            pl.BlockSpec((1, D_MODEL), lambda j: (0, 0)),
            pl.BlockSpec((D_MODEL, tile), lambda j: (0, first + j)),
        ],
        out_specs=[pl.BlockSpec((n, tile), lambda j: (0, j))]
        + [pl.BlockSpec((n, 2 * QW), lambda j: (0, 0))] * n_pat,
        out_shape=[jax.ShapeDtypeStruct((n, cols), f32)]
        + [jax.ShapeDtypeStruct((n, 2 * QW), f32)] * n_pat,
        name="qkv_s",
        compiler_params=pltpu.CompilerParams(
            dimension_semantics=("arbitrary",), vmem_limit_bytes=VMEM_LIMIT),
    )(x, scale, shift, g_pre, w_in_b)


T_NEW = 8


def _sample_attention(qkv_ref, c0_ref, c1_ref, c2_ref, o_ref, new_sc):
    qkv = qkv_ref[0]
    new_sc[0:T_NEW, :] = qkv[:, 3 * QW:]
    q_all = qkv[:, :3 * QW] * (HEAD_DIM ** -0.5)

    lane8 = lax.broadcasted_iota(jnp.int32, (2 * T_NEW, LANES), 1)
    low = lane8 < HEAD_DIM
    t_idx = lax.broadcasted_iota(jnp.int32, (2 * T_NEW, LANES), 0) % T_NEW
    caches = (c0_ref, c1_ref, c2_ref)
    pairs = range(N_HEADS // 2)
    pats = range(len(DILATIONS))

    def pair_cols(g, hp, base=0):
        return slice(base + g * QW + hp * LANES, base + g * QW + (hp + 1) * LANES)

    s_cache, s_new = {}, {}
    for hp in pairs:
        for g in pats:
            q2 = q_all[:, pair_cols(g, hp)]
            q_st = jnp.concatenate([jnp.where(low[:T_NEW], q2, 0.0), jnp.where(low[:T_NEW], 0.0, q2)],
                                   axis=0).astype(bf16)
            kt = caches[g][0, hp * LANES:(hp + 1) * LANES, :].astype(bf16)
            k_new = new_sc[:, pair_cols(g, hp)].astype(bf16)
            s_cache[hp, g] = _dot(q_st, kt)
            s_new[hp, g] = _dot_nt(q_st, k_new)

    probs = {}
    for hp in pairs:
        m = jnp.full((2 * T_NEW, 1), NEG, f32)
        for g, dil in enumerate(DILATIONS):
            rows = WINDOWS[g]
            rho = lax.broadcasted_iota(jnp.int32, (2 * T_NEW, rows), 1)
            tq = lax.broadcasted_iota(jnp.int32, (2 * T_NEW, rows), 0) % T_NEW
            ok_c = jnp.logical_and(rho >= tq, ((rho - tq) & (dil - 1)) == 0)
            ok_n = jnp.logical_and(lane8 <= t_idx, ((t_idx - lane8) & (dil - 1)) == 0)
            s_cache[hp, g] = jnp.where(ok_c, s_cache[hp, g], NEG)
            s_new[hp, g] = jnp.where(ok_n, s_new[hp, g], NEG)
            m = jnp.maximum(m, jnp.maximum(jnp.max(s_cache[hp, g], axis=-1, keepdims=True),
                                           jnp.max(s_new[hp, g], axis=-1, keepdims=True)))
        l = jnp.zeros((2 * T_NEW, 1), f32)
        for g in pats:
            p_c = jnp.exp(s_cache[hp, g] - m)
            p_n = jnp.exp(s_new[hp, g] - m)
            l = l + jnp.sum(p_c, axis=-1, keepdims=True) + jnp.sum(p_n, axis=-1, keepdims=True)
            probs[hp, g] = (p_c.astype(bf16), p_n.astype(bf16))
        probs[hp] = l

    o_parts = []
    for hp in pairs:
        acc = jnp.zeros((2 * T_NEW, LANES), f32)
        for g in pats:
            vt = caches[g][0, QW + hp * LANES: QW + (hp + 1) * LANES, :].astype(bf16)
            v_new = new_sc[:, pair_cols(g, hp, 3 * QW)].astype(bf16)
            p_c, p_n = probs[hp, g]
            acc = acc + _dot_nt(p_c, vt) + _dot(p_n, v_new)
        out = acc / probs[hp]
        o_parts.append(jnp.where(low[:T_NEW], out[:T_NEW], out[T_NEW:]))
    o_ref[0] = jnp.concatenate(o_parts, axis=1)


TILE = 256


def _expand_heads(w):
    r = lax.broadcasted_iota(jnp.int32, (2 * LANES, QW), 0) % LANES
    head = lax.broadcasted_iota(jnp.int32, (2 * LANES, QW), 1) // HEAD_DIM
    expand = (r == head // 2 + HEAD_DIM * (1 - head % 2)).astype(bf16)
    hi = w.astype(bf16)
    lo = (w - hi.astype(f32)).astype(bf16)
    return _dot(jnp.concatenate([hi, lo], axis=1), expand)


def _rest_kernel(*refs, sample, sample_every=None):
    if sample:
        (x_ref, scale_ref, shift_ref, gate_ref, gpre_ref, gpost_ref, lng_ref, lnb_ref,
         wa_ref, wzb_ref, wg_ref, wsp_ref, bsp_ref, wpa_ref, wpb_ref, wout_ref,
         attn_ref, y_ref, vn_ref) = refs
        x = x_ref[...]
        scale, shift, gate = scale_ref[...], shift_ref[...], gate_ref[...]
    else:
        (x_ref, scale_ref, shift_ref, gate_ref, gpre_ref, gpost_ref, lng_ref, lnb_ref,
         wa_ref, wzb_ref, wg_ref, wsp_ref, bsp_ref, wpa_ref, wpb_ref, wout_ref,
         o1_ref, m1_ref, l1_ref, o4_ref, m4_ref, l4_ref, o16_ref, m16_ref, l16_ref,
         qkv_s_ref, c0_ref, c1_ref, c2_ref, y_ref, attn_s_ref,
         o4_sc, m4_sc, l4_sc, o16_sc, m16_sc, l16_sc, new_sc) = refs
        x = x_ref[0]
        scale, shift, gate = scale_ref[0], shift_ref[0], gate_ref[0]

        @pl.when(jnp.logical_and(pl.program_id(0) == 0, pl.program_id(1) == 0))
        def _():
            new_sc[...] = jnp.zeros_like(new_sc)

        @pl.when(pl.program_id(1) % sample_every == 0)
        def _():
            _sample_attention(qkv_s_ref, c0_ref, c1_ref, c2_ref, attn_s_ref, new_sc)

    rows = x.shape[0]

    if sample:
        attn = attn_ref[...]
    else:
        for osc, msc, lsc, oref, mref, lref, dil in (
                (o4_sc, m4_sc, l4_sc, o4_ref, m4_ref, l4_ref, DILATIONS[1]),
                (o16_sc, m16_sc, l16_sc, o16_ref, m16_ref, l16_ref, DILATIONS[2])):
            n = rows // dil
            for r in range(dil):
                o_r = oref[0, r].astype(f32)
                for k in range(QW // LANES):
                    osc[k, pl.ds(r, n, stride=dil), :] = o_r[:, k * LANES:(k + 1) * LANES]
                msc[pl.ds(r, n, stride=dil), :] = mref[0, r]
                lsc[pl.ds(r, n, stride=dil), :] = lref[0, r]
        ms = (m1_ref[0], m4_sc[...], m16_sc[...])
        ls = (l1_ref[0], l4_sc[...], l16_sc[...])
        unchunk = lambda sc: jnp.concatenate([sc[k] for k in range(QW // LANES)], axis=1)
        outs = (o1_ref[0].astype(f32), unchunk(o4_sc), unchunk(o16_sc))
        m_all = jnp.maximum(jnp.maximum(ms[0], ms[1]), ms[2])
        ws = [jnp.exp(m - m_all) for m in ms]
        den = ws[0] * ls[0] + ws[1] * ls[1] + ws[2] * ls[2]
        attn = sum(_expand_heads(w / den) * o for w, o in zip(ws, outs))

    h = _norm_mod(x, gpre_ref[...], scale, shift).astype(bf16)

    pa = _dot(h, wa_ref[...])
    u_a, v_a, z_a = pa[:, :D_MODEL], pa[:, D_MODEL:2 * D_MODEL], pa[:, 2 * D_MODEL:]
    mu = jnp.mean(v_a, axis=-1, keepdims=True)
    cen = v_a - mu
    var = jnp.mean(cen * cen, axis=-1, keepdims=True)
    v_n = cen * lax.rsqrt(var + EPS) * lng_ref[...] + lnb_ref[...]
    v_nb = v_n.astype(bf16)
    if sample:
        vn_ref[...] = v_n
        pick = (lax.broadcasted_iota(jnp.int32, (rows, CHUNK), 1)
                == lax.broadcasted_iota(jnp.int32, (rows, CHUNK), 0) % T_NEW).astype(bf16)
        same_seq = (lax.broadcasted_iota(jnp.int32, (rows, rows), 0) // T_NEW
                    == lax.broadcasted_iota(jnp.int32, (rows, rows), 1) // T_NEW)
        cols = []
        for g in range(A_GROUPS):
            tiled = _dot_nt(_dot(pick, wsp_ref[g]).astype(bf16), pick)
            w_blk = jnp.where(same_seq, tiled, 0.0).astype(bf16)
            cols.append(_dot(w_blk, v_nb[:, g * LANES:(g + 1) * LANES]) + bsp_ref[:, g:g + 1])
        zs = jnp.concatenate(cols, axis=1)
    else:
        n_ck = rows // CHUNK
        per_group = []
        for g in range(A_GROUPS):
            rhs = jnp.concatenate(
                [v_nb[ck * CHUNK:(ck + 1) * CHUNK, g * LANES:(g + 1) * LANES] for ck in range(n_ck)], axis=1)
            per_group.append(_dot(wsp_ref[g], rhs) + bsp_ref[:, g:g + 1])
        zs = jnp.concatenate(
            [jnp.concatenate([pg[:, ck * LANES:(ck + 1) * LANES] for pg in per_group], axis=1)
             for ck in range(n_ck)], axis=0)
    y_a = u_a * zs * _silu(z_a)

    z_b = _dot(h, wzb_ref[...])
    y_b = (attn * _silu(z_b)).astype(bf16)
    gl = _dot(h, wg_ref[...])
    p_b = _dot(y_b, wpb_ref[...])
    p_a = _dot(y_a.astype(bf16), wpa_ref[...])
    merged = jax.nn.sigmoid(gl[:, :D_MODEL]) * p_a + jax.nn.sigmoid(gl[:, D_MODEL:]) * p_b
    out = _dot(merged.astype(bf16), wout_ref[...])
    normed = out * lax.rsqrt(jnp.mean(out * out, axis=-1, keepdims=True) + EPS) * gpost_ref[...]
    y = x + gate * normed
    if sample:
        y_ref[...] = y
    else:
        y_ref[0] = y


def _full(shape):
    nd = len(shape)
    return pl.BlockSpec(shape, lambda *_: (0,) * nd)


def _weight_specs(weights):
    in_proj = [pl.BlockSpec((D_MODEL, 3 * D_MODEL), lambda *_: (0, 0)),
               pl.BlockSpec((D_MODEL, QW), lambda *_: (0, ZB_COL0 // QW)),
               pl.BlockSpec((D_MODEL, 2 * D_MODEL), lambda *_: (0, GATE_COL0 // (2 * D_MODEL)))]
    return in_proj + [_full(w.shape) for w in weights[3:]]


def _rest_prompt(x, mod, vecs, weights, attn_parts, qkv_s, caches):
    b = x.shape[0]
    n_pat = len(DILATIONS)
    tiles = SEQ // TILE
    n_seq = qkv_s.shape[0]
    sample_every, rem = divmod(b * tiles, n_seq)
    assert rem == 0 and tiles % sample_every == 0, "sample sequences must tile the grid evenly"
    seq_of = lambda i, c: (i * (tiles // sample_every) + c // sample_every, 0, 0)
    tile3 = lambda i, c: (i, c, 0)
    part_args, part_specs, scratch = [], [], []
    for pattern, dil in enumerate(DILATIONS):
        for a in attn_parts:
            lanes = a.shape[-1]
            if dil == 1:
                part_args.append(a)
                part_specs.append(pl.BlockSpec((None, 1, TILE, lanes),
                                               lambda i, c, p=pattern: (p, i, c, 0)))
            else:
                part_args.append(a.reshape(n_pat, b, dil, SEQ // dil, lanes))
                part_specs.append(pl.BlockSpec((None, 1, dil, TILE // dil, lanes),
                                               lambda i, c, p=pattern: (p, i, 0, c, 0)))
        if dil != 1:
            scratch += [pltpu.VMEM((QW // LANES, TILE, LANES), f32),
                        pltpu.VMEM((TILE, LANES), f32), pltpu.VMEM((TILE, LANES), f32)]
    in_specs = (
        [pl.BlockSpec((1, TILE, D_MODEL), tile3)]
        + [pl.BlockSpec((1, 1, D_MODEL), lambda i, c, col=col: (i, 0, col))
           for col in (MOD_SCALE, MOD_SHIFT, MOD_GATE)]
        + [_full(v.shape) for v in vecs] + _weight_specs(weights) + part_specs
        + [pl.BlockSpec((1, T_NEW, 9 * QW), seq_of)]
        + [pl.BlockSpec((1, 2 * QW, win), seq_of) for win in WINDOWS])
    scratch.append(pltpu.VMEM((LANES, 6 * QW), f32))
    return pl.pallas_call(
        functools.partial(_rest_kernel, sample=False, sample_every=sample_every),
        grid=(b, tiles),
        in_specs=in_specs,
        out_specs=[pl.BlockSpec((1, TILE, D_MODEL), tile3), pl.BlockSpec((1, T_NEW, QW), seq_of)],
        out_shape=[jax.ShapeDtypeStruct((b, SEQ, D_MODEL), f32),
                   jax.ShapeDtypeStruct((n_seq, T_NEW, QW), f32)],
        scratch_shapes=scratch,
        name="rest_p",
        compiler_params=pltpu.CompilerParams(
            dimension_semantics=("arbitrary", "arbitrary"), vmem_limit_bytes=VMEM_LIMIT),
    )(x, mod, mod, mod, *vecs, *weights, *part_args, qkv_s, *caches)


def _rest_sample(x, scale, shift, gate, vecs, weights, attn):
    n = x.shape[0]
    args = (x, scale, shift, gate, *vecs, *weights, attn)
    in_specs = ([_full(a.shape) for a in (x, scale, shift, gate, *vecs)] + _weight_specs(weights)
                + [_full(attn.shape)])
    return pl.pallas_call(
        functools.partial(_rest_kernel, sample=True),
        in_specs=in_specs,
        out_specs=[_full((n, D_MODEL)), _full((n, D_MODEL))],
        out_shape=[jax.ShapeDtypeStruct((n, D_MODEL), f32), jax.ShapeDtypeStruct((n, D_MODEL), f32)],
        grid=(1,),
        name="rest_s",
        compiler_params=pltpu.CompilerParams(
            dimension_semantics=("arbitrary",), vmem_limit_bytes=VMEM_LIMIT),
    )(*args)


def kernel(x_prompt, x_sample, cache_kv_w128, cache_kv_w512, cache_kv_w2048, c_prompt, c_sample, w_cond, b_cond, g_pre, w_in, ln_v_g, ln_v_b, w_spatial, b_spatial, w_proj_a, w_proj_b, w_out, g_post):
    assert w_in.shape[0] == 1, "single layer"
    bp, seq, _ = x_prompt.shape
    bs, t_new, _ = x_sample.shape
    assert seq == SEQ and t_new == T_NEW

    w_in_b = w_in[0].astype(bf16)
    w_kvt = jnp.transpose(w_in[0][:, K_COL0:ZB_COL0].reshape(D_MODEL, 2, 3, QW), (2, 1, 3, 0))
    w_kvt = w_kvt.reshape(3, 2 * QW, D_MODEL).astype(bf16)
    causal = jnp.tril(jnp.ones((CHUNK, CHUNK), bool))
    w_sp = jnp.where(causal[None], w_spatial[0], 0.0)
    w_sp_p = w_sp.astype(bf16)
    n_s = bs * T_NEW
    b_sp_p = b_spatial[0].T
    b_sp_s = jnp.tile(b_spatial[0][:, :T_NEW].T, (bs, 1))
    weights_tail = (w_proj_a[0].astype(bf16), w_proj_b[0].astype(bf16), w_out[0].astype(bf16))
    vecs = (g_pre, g_post, ln_v_g, ln_v_b)

    mod = _cond(jnp.concatenate([c_prompt, c_sample], axis=0), w_cond[0], b_cond)
    shift, scale, gate = mod[:, :D_MODEL], mod[:, D_MODEL:2 * D_MODEL], mod[:, 2 * D_MODEL:]
    mod_p = mod.reshape(bp + bs, 1, 3 * D_MODEL)
    ms = lambda a: jnp.repeat(a[bp:], T_NEW, axis=0)

    xs = x_sample.reshape(n_s, D_MODEL)
    qkv_s, *kv_s = _qkv_sample(xs, ms(scale), ms(shift), g_pre, w_in_b)
    caches_t = []
    for cache, win in zip((cache_kv_w128, cache_kv_w512, cache_kv_w2048), WINDOWS):
        caches_t.append(jnp.transpose(cache[0], (0, 2, 3, 4, 1)).reshape(bs, 2 * QW, win))

    o_all, m_all, l_all, *kvt = _attn_prompt(x_prompt, mod_p, g_pre, w_in_b, w_kvt)
    in_proj = (w_in_b, w_in_b, w_in_b)
    y_p, attn_s = _rest_prompt(x_prompt, mod_p, vecs,
                               in_proj + (w_sp_p, b_sp_p) + weights_tail, (o_all, m_all, l_all),
                               qkv_s.reshape(bs, T_NEW, 9 * QW), caches_t)
    kv_p = [jnp.transpose(a.reshape(bp, 2, N_HEADS, HEAD_DIM, win), (0, 4, 1, 2, 3))[None]
            for a, win in zip(kvt, WINDOWS)]

    attn_s = attn_s.reshape(n_s, QW)
    y_s, v_n_s = _rest_sample(xs, ms(scale), ms(shift), ms(gate), vecs,
                              in_proj + (w_sp_p, b_sp_s) + weights_tail, attn_s)
    kv_s = [a.reshape(1, bs, T_NEW, 2, N_HEADS, HEAD_DIM) for a in kv_s]

    return (y_p, y_s.reshape(bs, T_NEW, D_MODEL), kv_p[0], kv_p[1], kv_p[2],
            kv_s[0], kv_s[1], kv_s[2], v_n_s.reshape(1, bs, T_NEW, D_MODEL))
```

```python
import functools

import jax
import jax.numpy as jnp
from jax import lax
from jax.experimental import pallas as pl
from jax.experimental.pallas import tpu as pltpu

D_MODEL = 1024
SEQ = 2048
HEAD_DIM = 64
N_HEADS = 8
QW = N_HEADS * HEAD_DIM
WINDOWS = (128, 512, 2048)
DILATIONS = (1, 4, 16)
CHUNK = 128
A_GROUPS = 8
EPS = 1e-6
NEG = -1e30
LANES = 128
VMEM_LIMIT = 56 * 1024 * 1024
Q_COL0 = 3 * D_MODEL
K_COL0 = Q_COL0 + 3 * QW
V_COL0 = K_COL0 + 3 * QW
ZB_COL0 = V_COL0 + 3 * QW
GATE_COL0 = ZB_COL0 + QW
MOD_SHIFT, MOD_SCALE, MOD_GATE = 0, 1, 2

f32 = jnp.float32
bf16 = jnp.bfloat16


def _silu(x):
    return x * jax.nn.sigmoid(x)


def _norm_mod(x, g_pre, scale, shift):
    y = x * lax.rsqrt(jnp.mean(x * x, axis=-1, keepdims=True) + EPS) * g_pre
    return y * (1.0 + scale) + shift


def _dot(a, b):
    return jnp.dot(a, b, preferred_element_type=f32)


def _dot_nt(a, b):
    return lax.dot_general(a, b, (((1,), (1,)), ((), ())), preferred_element_type=f32)


def _cond_kernel(c_ref, w_ref, b_ref, o_ref):
    o_ref[...] = _dot(_silu(c_ref[...]).astype(bf16), w_ref[...].astype(bf16)) + b_ref[...]


def _cond(c_all, w_cond, b_cond):
    n = c_all.shape[0]
    return pl.pallas_call(
        _cond_kernel,
        out_shape=jax.ShapeDtypeStruct((n, 3 * D_MODEL), f32),
        name="cond",
        compiler_params=pltpu.CompilerParams(vmem_limit_bytes=VMEM_LIMIT),
    )(c_all, w_cond, b_cond)


ROWS_PER_STEP = 512
Q_BLOCK = 128
NORM_ROWS = 256


def _attn_step(c, last_step, wq_ref, wk_ref, wv_ref, wkvt_ref, o_ref, m_ref, l_ref, kvt_ref,
               h_sc, k_sc, ve_sc, vo_sc, *, dil, window):
    seg = SEQ // dil
    n_lc = D_MODEL // LANES

    def rows_of(start, n):
        idx = pl.ds(pl.multiple_of(start, n), n) if dil == 1 else pl.ds(start, n, stride=dil)
        return jnp.concatenate([h_sc[k, idx, :] for k in range(n_lc)], axis=1)

    if seg >= ROWS_PER_STEP:
        per = seg // ROWS_PER_STEP
        h = rows_of((c // per) + (c % per) * ROWS_PER_STEP * dil, ROWS_PER_STEP)
    else:
        per = ROWS_PER_STEP // seg
        h = jnp.concatenate([rows_of(per * c + i, seg) for i in range(per)], axis=0)
    h = h.astype(bf16)
    base = pl.multiple_of(c * ROWS_PER_STEP, ROWS_PER_STEP)

    def tokens_t(start, n):
        return jnp.concatenate([h_sc[k, pl.ds(start, n), :] for k in range(n_lc)], axis=1).astype(bf16)

    if window == SEQ:
        kvt_ref[0] = _dot_nt(wkvt_ref[0], tokens_t(base, ROWS_PER_STEP))
    else:
        @pl.when(c == last_step)
        def _():
            kvt_ref[0] = _dot_nt(wkvt_ref[0], tokens_t(SEQ - window, window))

    low_w = (lax.broadcasted_iota(jnp.int32, (ROWS_PER_STEP, QW), 1) % LANES) < HEAD_DIM
    new_rows = pl.ds(Q_BLOCK + base, ROWS_PER_STEP)
    k_sc[new_rows, :] = _dot(h, wk_ref[...]).astype(bf16)

    def project_v():
        v = _dot(h, wv_ref[...])
        ve_sc[new_rows, :] = jnp.where(low_w, v, 1.0).astype(bf16)
        vo_sc[new_rows, :] = jnp.where(low_w, 1.0, v).astype(bf16)

    low_q = (lax.broadcasted_iota(jnp.int32, (Q_BLOCK, QW), 1) % LANES) < HEAD_DIM

    def project_q(j):
        q = _dot(h[j * Q_BLOCK:(j + 1) * Q_BLOCK], wq_ref[...]) * (HEAD_DIM ** -0.5)
        return jnp.where(low_q, q, 0.0).astype(bf16), jnp.where(low_q, 0.0, q).astype(bf16)

    n_blocks = ROWS_PER_STEP // Q_BLOCK
    pairs = N_HEADS // 2
    qi = lax.broadcasted_iota(jnp.int32, (2 * Q_BLOCK, 2 * Q_BLOCK), 0) % Q_BLOCK
    kj = lax.broadcasted_iota(jnp.int32, (2 * Q_BLOCK, 2 * Q_BLOCK), 1)
    lane = lax.broadcasted_iota(jnp.int32, (Q_BLOCK, LANES), 1)
    low = lane < HEAD_DIM

    def has_prev(j):
        return seg > ROWS_PER_STEP or (j * Q_BLOCK) % seg != 0

    def key_rows(j):
        r0 = pl.multiple_of(base + j * Q_BLOCK, Q_BLOCK)
        return pl.ds(r0, 2 * Q_BLOCK) if has_prev(j) else pl.ds(r0 + Q_BLOCK, Q_BLOCK)

    def mask_of(j):
        if not has_prev(j):
            own = (2 * Q_BLOCK, Q_BLOCK)
            return (lax.broadcasted_iota(jnp.int32, own, 1)
                    <= lax.broadcasted_iota(jnp.int32, own, 0) % Q_BLOCK)
        off = jnp.where((base % seg) != 0, 0, Q_BLOCK) if (seg > ROWS_PER_STEP and j == 0) else 0
        in_prev = jnp.logical_and(kj < Q_BLOCK, kj >= qi + off)
        in_cur = jnp.logical_and(kj >= Q_BLOCK, kj - Q_BLOCK <= qi)
        return jnp.logical_or(in_prev, in_cur)

    def qk(j, q_even, q_odd):
        keys = key_rows(j)
        out = []
        for hp in range(pairs):
            cols = slice(hp * LANES, (hp + 1) * LANES)
            q_st = jnp.concatenate([q_even[:, cols], q_odd[:, cols]], axis=0)
            out.append(_dot_nt(q_st, k_sc[keys, cols]))
        return out

    def finish(j, scores):
        rows = slice(j * Q_BLOCK, (j + 1) * Q_BLOCK)
        keys = key_rows(j)
        mask = mask_of(j)
        ps, ms = [], []
        for hp in range(pairs):
            s = jnp.where(mask, scores[hp], NEG)
            for half in (s[:Q_BLOCK], s[Q_BLOCK:]):
                m = jnp.max(half, axis=-1, keepdims=True)
                ps.append(jnp.exp(half - m).astype(bf16))
                ms.append(m)
        o_parts = []
        m_tile = jnp.zeros((Q_BLOCK, LANES), f32)
        l_tile = jnp.ones((Q_BLOCK, LANES), f32)
        for hp in range(pairs):
            cols = slice(hp * LANES, (hp + 1) * LANES)
            acc_e = _dot(ps[2 * hp], ve_sc[keys, cols])
            acc_o = _dot(ps[2 * hp + 1], vo_sc[keys, cols])
            o_parts.append(jnp.where(low, acc_e, acc_o))
            l_tile = jnp.where(lane == HEAD_DIM + hp, acc_e, jnp.where(lane == hp, acc_o, l_tile))
            m_tile = jnp.where(lane == HEAD_DIM + hp, ms[2 * hp],
                               jnp.where(lane == hp, ms[2 * hp + 1], m_tile))
        o_ref[0, rows, :] = jnp.concatenate(o_parts, axis=1).astype(bf16)
        m_ref[0, rows, :] = m_tile
        l_ref[0, rows, :] = l_tile

    qs = {j: project_q(j) for j in range(min(2, n_blocks))}
    scores = qk(0, *qs[0])
    project_v()
    for j in range(n_blocks):
        nxt = qk(j + 1, *qs[j + 1]) if j + 1 < n_blocks else None
        if j + 2 < n_blocks:
            qs[j + 2] = project_q(j + 2)
        finish(j, scores)
        scores = nxt


def _attn_kernel(x_ref, scale_ref, shift_ref, gpre_ref, wq_ref, wk_ref, wv_ref, wkvt_ref,
                 o_ref, m_ref, l_ref, kvt0_ref, kvt1_ref, kvt2_ref, h_sc, k_sc, ve_sc, vo_sc):
    g = pl.program_id(1)
    c = pl.program_id(2)
    last_step = pl.num_programs(2) - 1

    kvt_refs = (kvt0_ref, kvt1_ref, kvt2_ref)

    def step(pattern, step_index):
        _attn_step(step_index, last_step, wq_ref, wk_ref, wv_ref, wkvt_ref, o_ref, m_ref, l_ref,
                   kvt_refs[pattern], h_sc, k_sc, ve_sc, vo_sc,
                   dil=DILATIONS[pattern], window=WINDOWS[pattern])

    @pl.when(jnp.logical_and(g == 0, c == 0))
    def _():
        zeros = jnp.zeros((Q_BLOCK, QW), bf16)
        k_sc[0:Q_BLOCK, :] = zeros
        ve_sc[0:Q_BLOCK, :] = zeros
        vo_sc[0:Q_BLOCK, :] = zeros
        for rb in range(SEQ // NORM_ROWS):
            rs = slice(rb * NORM_ROWS, (rb + 1) * NORM_ROWS)
            hb = _norm_mod(x_ref[0, rs, :], gpre_ref[...], scale_ref[0], shift_ref[0])
            for k in range(D_MODEL // LANES):
                h_sc[k, rs, :] = hb[:, k * LANES:(k + 1) * LANES]
        step(0, jnp.int32(0))

    for pattern in range(len(DILATIONS)):
        @pl.when(jnp.logical_and(g == pattern, jnp.logical_or(pattern > 0, c > 0)))
        def _(pattern=pattern):
            step(pattern, c)


def _attn_prompt(x, mod, g_pre, w_in_b, w_kvt):
    b = x.shape[0]
    n_pat = len(DILATIONS)
    steps = SEQ // ROWS_PER_STEP
    full_window = WINDOWS.index(SEQ)
    per_b = lambda i, g, c: (i, 0, 0)
    step_rows = lambda i, g, c: (g, i, c, 0)

    def kvt_spec(pattern):
        if WINDOWS[pattern] == SEQ:
            return pl.BlockSpec((1, 2 * QW, ROWS_PER_STEP),
                                lambda i, g, c: (i, 0, jnp.where(g == full_window, c, 0)))
        return pl.BlockSpec((1, 2 * QW, WINDOWS[pattern]), per_b)

    return pl.pallas_call(
        _attn_kernel,
        grid=(b, n_pat, steps),
        in_specs=[
            pl.BlockSpec((1, SEQ, D_MODEL), per_b),
            pl.BlockSpec((1, 1, D_MODEL), lambda i, g, c: (i, 0, MOD_SCALE)),
            pl.BlockSpec((1, 1, D_MODEL), lambda i, g, c: (i, 0, MOD_SHIFT)),
            pl.BlockSpec((1, D_MODEL), lambda i, g, c: (0, 0)),
            pl.BlockSpec((D_MODEL, QW), lambda i, g, c: (0, Q_COL0 // QW + g)),
            pl.BlockSpec((D_MODEL, QW), lambda i, g, c: (0, K_COL0 // QW + g)),
            pl.BlockSpec((D_MODEL, QW), lambda i, g, c: (0, V_COL0 // QW + g)),
            pl.BlockSpec((1, 2 * QW, D_MODEL), lambda i, g, c: (g, 0, 0)),
        ],
        out_specs=[
            pl.BlockSpec((None, 1, ROWS_PER_STEP, QW), step_rows),
            pl.BlockSpec((None, 1, ROWS_PER_STEP, LANES), step_rows),
            pl.BlockSpec((None, 1, ROWS_PER_STEP, LANES), step_rows),
        ] + [kvt_spec(p) for p in range(n_pat)],
        out_shape=[
            jax.ShapeDtypeStruct((n_pat, b, SEQ, QW), bf16),
            jax.ShapeDtypeStruct((n_pat, b, SEQ, LANES), f32),
            jax.ShapeDtypeStruct((n_pat, b, SEQ, LANES), f32),
        ] + [jax.ShapeDtypeStruct((b, 2 * QW, win), f32) for win in WINDOWS],
        scratch_shapes=[pltpu.VMEM((D_MODEL // LANES, SEQ, LANES), f32),
                        pltpu.VMEM((SEQ + Q_BLOCK, QW), bf16), pltpu.VMEM((SEQ + Q_BLOCK, QW), bf16),
                        pltpu.VMEM((SEQ + Q_BLOCK, QW), bf16)],
        name="attn_p",
        compiler_params=pltpu.CompilerParams(
            dimension_semantics=("arbitrary", "arbitrary", "arbitrary"), vmem_limit_bytes=VMEM_LIMIT),
    )(x, mod, mod, g_pre, w_in_b, w_in_b, w_in_b, w_kvt)


def _qkv_s_kernel(x_ref, scale_ref, shift_ref, gpre_ref, w_ref, o_ref, *kv_refs):
    h = _norm_mod(x_ref[...], gpre_ref[...], scale_ref[...], shift_ref[...]).astype(bf16)
    tile = _dot(h, w_ref[...])
    o_ref[...] = tile
    for half in range(2):
        @pl.when(pl.program_id(0) == 1 + half)
        def _(half=half):
            for g, kv_ref in enumerate(kv_refs):
                kv_ref[:, half * QW:(half + 1) * QW] = tile[:, g * QW:(g + 1) * QW]


def _qkv_sample(x, scale, shift, g_pre, w_in_b):
    n = x.shape[0]
    n_pat = len(DILATIONS)
    cols = 3 * n_pat * QW
    tile = n_pat * QW
    first = Q_COL0 // tile
    return pl.pallas_call(
        _qkv_s_kernel,
        grid=(cols // tile,),
        in_specs=[
            pl.BlockSpec((n, D_MODEL), lambda j: (0, 0)),
            pl.BlockSpec((n, D_MODEL), lambda j: (0, 0)),
            pl.BlockSpec((n, D_MODEL), lambda j: (0, 0)),
            pl.BlockSpec((1, D_MODEL), lambda j: (0, 0)),
            pl.BlockSpec((D_MODEL, tile), lambda j: (0, first + j)),
        ],
        out_specs=[pl.BlockSpec((n, tile), lambda j: (0, j))]
        + [pl.BlockSpec((n, 2 * QW), lambda j: (0, 0))] * n_pat,
        out_shape=[jax.ShapeDtypeStruct((n, cols), f32)]
        + [jax.ShapeDtypeStruct((n, 2 * QW), f32)] * n_pat,
        name="qkv_s",
        compiler_params=pltpu.CompilerParams(
            dimension_semantics=("arbitrary",), vmem_limit_bytes=VMEM_LIMIT),
    )(x, scale, shift, g_pre, w_in_b)


T_NEW = 8


def _sample_attention(qkv_ref, c0_ref, c1_ref, c2_ref, o_ref, new_sc):
    qkv = qkv_ref[0]
    new_sc[0:T_NEW, :] = qkv[:, 3 * QW:]
    q_all = qkv[:, :3 * QW] * (HEAD_DIM ** -0.5)

    lane8 = lax.broadcasted_iota(jnp.int32, (2 * T_NEW, LANES), 1)
    low = lane8 < HEAD_DIM
    t_idx = lax.broadcasted_iota(jnp.int32, (2 * T_NEW, LANES), 0) % T_NEW
    caches = (c0_ref, c1_ref, c2_ref)
    pairs = range(N_HEADS // 2)
    pats = range(len(DILATIONS))

    def pair_cols(g, hp, base=0):
        return slice(base + g * QW + hp * LANES, base + g * QW + (hp + 1) * LANES)

    s_cache, s_new = {}, {}
    for hp in pairs:
        for g in pats:
            q2 = q_all[:, pair_cols(g, hp)]
            q_st = jnp.concatenate([jnp.where(low[:T_NEW], q2, 0.0), jnp.where(low[:T_NEW], 0.0, q2)],
                                   axis=0).astype(bf16)
            kt = caches[g][0, hp * LANES:(hp + 1) * LANES, :].astype(bf16)
            k_new = new_sc[:, pair_cols(g, hp)].astype(bf16)
            s_cache[hp, g] = _dot(q_st, kt)
            s_new[hp, g] = _dot_nt(q_st, k_new)

    probs = {}
    for hp in pairs:
        m = jnp.full((2 * T_NEW, 1), NEG, f32)
        for g, dil in enumerate(DILATIONS):
            rows = WINDOWS[g]
            rho = lax.broadcasted_iota(jnp.int32, (2 * T_NEW, rows), 1)
            tq = lax.broadcasted_iota(jnp.int32, (2 * T_NEW, rows), 0) % T_NEW
            ok_c = jnp.logical_and(rho >= tq, ((rho - tq) & (dil - 1)) == 0)
            ok_n = jnp.logical_and(lane8 <= t_idx, ((t_idx - lane8) & (dil - 1)) == 0)
            s_cache[hp, g] = jnp.where(ok_c, s_cache[hp, g], NEG)
            s_new[hp, g] = jnp.where(ok_n, s_new[hp, g], NEG)
            m = jnp.maximum(m, jnp.maximum(jnp.max(s_cache[hp, g], axis=-1, keepdims=True),
                                           jnp.max(s_new[hp, g], axis=-1, keepdims=True)))
        l = jnp.zeros((2 * T_NEW, 1), f32)
        for g in pats:
            p_c = jnp.exp(s_cache[hp, g] - m)
            p_n = jnp.exp(s_new[hp, g] - m)
            l = l + jnp.sum(p_c, axis=-1, keepdims=True) + jnp.sum(p_n, axis=-1, keepdims=True)
            probs[hp, g] = (p_c.astype(bf16), p_n.astype(bf16))
        probs[hp] = l

    o_parts = []
    for hp in pairs:
        acc = jnp.zeros((2 * T_NEW, LANES), f32)
        for g in pats:
            vt = caches[g][0, QW + hp * LANES: QW + (hp + 1) * LANES, :].astype(bf16)
            v_new = new_sc[:, pair_cols(g, hp, 3 * QW)].astype(bf16)
            p_c, p_n = probs[hp, g]
            acc = acc + _dot_nt(p_c, vt) + _dot(p_n, v_new)
        out = acc / probs[hp]
        o_parts.append(jnp.where(low[:T_NEW], out[:T_NEW], out[T_NEW:]))
    o_ref[0] = jnp.concatenate(o_parts, axis=1)


TILE = 256


def _expand_heads(w):
    r = lax.broadcasted_iota(jnp.int32, (2 * LANES, QW), 0) % LANES
    head = lax.broadcasted_iota(jnp.int32, (2 * LANES, QW), 1) // HEAD_DIM
    expand = (r == head // 2 + HEAD_DIM * (1 - head % 2)).astype(bf16)
    hi = w.astype(bf16)
    lo = (w - hi.astype(f32)).astype(bf16)
    return _dot(jnp.concatenate([hi, lo], axis=1), expand)


def _rest_kernel(*refs, sample, sample_every=None):
    if sample:
        (x_ref, scale_ref, shift_ref, gate_ref, gpre_ref, gpost_ref, lng_ref, lnb_ref,
         wa_ref, wzb_ref, wg_ref, wsp_ref, bsp_ref, wpa_ref, wpb_ref, wout_ref,
         attn_ref, y_ref, vn_ref) = refs
        x = x_ref[...]
        scale, shift, gate = scale_ref[...], shift_ref[...], gate_ref[...]
    else:
        (x_ref, scale_ref, shift_ref, gate_ref, gpre_ref, gpost_ref, lng_ref, lnb_ref,
         wa_ref, wzb_ref, wg_ref, wsp_ref, bsp_ref, wpa_ref, wpb_ref, wout_ref,
         o1_ref, m1_ref, l1_ref, o4_ref, m4_ref, l4_ref, o16_ref, m16_ref, l16_ref,
         qkv_s_ref, c0_ref, c1_ref, c2_ref, y_ref, attn_s_ref,
         o4_sc, m4_sc, l4_sc, o16_sc, m16_sc, l16_sc, new_sc) = refs
        x = x_ref[0]
        scale, shift, gate = scale_ref[0], shift_ref[0], gate_ref[0]

        @pl.when(jnp.logical_and(pl.program_id(0) == 0, pl.program_id(1) == 0))
        def _():
            new_sc[...] = jnp.zeros_like(new_sc)

        @pl.when(pl.program_id(1) % sample_every == 0)
        def _():
            _sample_attention(qkv_s_ref, c0_ref, c1_ref, c2_ref, attn_s_ref, new_sc)

    rows = x.shape[0]

    if sample:
        attn = attn_ref[...]
    else:
        for osc, msc, lsc, oref, mref, lref, dil in (
                (o4_sc, m4_sc, l4_sc, o4_ref, m4_ref, l4_ref, DILATIONS[1]),
                (o16_sc, m16_sc, l16_sc, o16_ref, m16_ref, l16_ref, DILATIONS[2])):
            n = rows // dil
            for r in range(dil):
                o_r = oref[0, r].astype(f32)
                for k in range(QW // LANES):
                    osc[k, pl.ds(r, n, stride=dil), :] = o_r[:, k * LANES:(k + 1) * LANES]
                msc[pl.ds(r, n, stride=dil), :] = mref[0, r]
                lsc[pl.ds(r, n, stride=dil), :] = lref[0, r]
        ms = (m1_ref[0], m4_sc[...], m16_sc[...])
        ls = (l1_ref[0], l4_sc[...], l16_sc[...])
        unchunk = lambda sc: jnp.concatenate([sc[k] for k in range(QW // LANES)], axis=1)
        outs = (o1_ref[0].astype(f32), unchunk(o4_sc), unchunk(o16_sc))
        m_all = jnp.maximum(jnp.maximum(ms[0], ms[1]), ms[2])
        ws = [jnp.exp(m - m_all) for m in ms]
        den = ws[0] * ls[0] + ws[1] * ls[1] + ws[2] * ls[2]
        attn = sum(_expand_heads(w / den) * o for w, o in zip(ws, outs))

    h = _norm_mod(x, gpre_ref[...], scale, shift).astype(bf16)

    pa = _dot(h, wa_ref[...])
    u_a, v_a, z_a = pa[:, :D_MODEL], pa[:, D_MODEL:2 * D_MODEL], pa[:, 2 * D_MODEL:]
    mu = jnp.mean(v_a, axis=-1, keepdims=True)
    cen = v_a - mu
    var = jnp.mean(cen * cen, axis=-1, keepdims=True)
    v_n = cen * lax.rsqrt(var + EPS) * lng_ref[...] + lnb_ref[...]
    v_nb = v_n.astype(bf16)
    if sample:
        vn_ref[...] = v_n
        pick = (lax.broadcasted_iota(jnp.int32, (rows, CHUNK), 1)
                == lax.broadcasted_iota(jnp.int32, (rows, CHUNK), 0) % T_NEW).astype(bf16)
        same_seq = (lax.broadcasted_iota(jnp.int32, (rows, rows), 0) // T_NEW
                    == lax.broadcasted_iota(jnp.int32, (rows, rows), 1) // T_NEW)
        cols = []
        for g in range(A_GROUPS):
            tiled = _dot_nt(_dot(pick, wsp_ref[g]).astype(bf16), pick)
            w_blk = jnp.where(same_seq, tiled, 0.0).astype(bf16)
            cols.append(_dot(w_blk, v_nb[:, g * LANES:(g + 1) * LANES]) + bsp_ref[:, g:g + 1])
        zs = jnp.concatenate(cols, axis=1)
    else:
        n_ck = rows // CHUNK
        per_group = []
        for g in range(A_GROUPS):
            rhs = jnp.concatenate(
                [v_nb[ck * CHUNK:(ck + 1) * CHUNK, g * LANES:(g + 1) * LANES] for ck in range(n_ck)], axis=1)
            per_group.append(_dot(wsp_ref[g], rhs) + bsp_ref[:, g:g + 1])
        zs = jnp.concatenate(
            [jnp.concatenate([pg[:, ck * LANES:(ck + 1) * LANES] for pg in per_group], axis=1)
             for ck in range(n_ck)], axis=0)
    y_a = u_a * zs * _silu(z_a)

    z_b = _dot(h, wzb_ref[...])
    y_b = (attn * _silu(z_b)).astype(bf16)
    gl = _dot(h, wg_ref[...])
    p_b = _dot(y_b, wpb_ref[...])
    p_a = _dot(y_a.astype(bf16), wpa_ref[...])
    merged = jax.nn.sigmoid(gl[:, :D_MODEL]) * p_a + jax.nn.sigmoid(gl[:, D_MODEL:]) * p_b
    out = _dot(merged.astype(bf16), wout_ref[...])
    normed = out * lax.rsqrt(jnp.mean(out * out, axis=-1, keepdims=True) + EPS) * gpost_ref[...]
    y = x + gate * normed
    if sample:
        y_ref[...] = y
    else:
        y_ref[0] = y


def _full(shape):
    nd = len(shape)
    return pl.BlockSpec(shape, lambda *_: (0,) * nd)


def _weight_specs(weights):
    in_proj = [pl.BlockSpec((D_MODEL, 3 * D_MODEL), lambda *_: (0, 0)),
               pl.BlockSpec((D_MODEL, QW), lambda *_: (0, ZB_COL0 // QW)),
               pl.BlockSpec((D_MODEL, 2 * D_MODEL), lambda *_: (0, GATE_COL0 // (2 * D_MODEL)))]
    return in_proj + [_full(w.shape) for w in weights[3:]]


def _rest_prompt(x, mod, vecs, weights, attn_parts, qkv_s, caches):
    b = x.shape[0]
    n_pat = len(DILATIONS)
    tiles = SEQ // TILE
    n_seq = qkv_s.shape[0]
    sample_every, rem = divmod(b * tiles, n_seq)
    assert rem == 0 and tiles % sample_every == 0, "sample sequences must tile the grid evenly"
    seq_of = lambda i, c: (i * (tiles // sample_every) + c // sample_every, 0, 0)
    tile3 = lambda i, c: (i, c, 0)
    part_args, part_specs, scratch = [], [], []
    for pattern, dil in enumerate(DILATIONS):
        for a in attn_parts:
            lanes = a.shape[-1]
            if dil == 1:
                part_args.append(a)
                part_specs.append(pl.BlockSpec((None, 1, TILE, lanes),
                                               lambda i, c, p=pattern: (p, i, c, 0)))
            else:
                part_args.append(a.reshape(n_pat, b, dil, SEQ // dil, lanes))
                part_specs.append(pl.BlockSpec((None, 1, dil, TILE // dil, lanes),
                                               lambda i, c, p=pattern: (p, i, 0, c, 0)))
        if dil != 1:
            scratch += [pltpu.VMEM((QW // LANES, TILE, LANES), f32),
                        pltpu.VMEM((TILE, LANES), f32), pltpu.VMEM((TILE, LANES), f32)]
    in_specs = (
        [pl.BlockSpec((1, TILE, D_MODEL), tile3)]
        + [pl.BlockSpec((1, 1, D_MODEL), lambda i, c, col=col: (i, 0, col))
           for col in (MOD_SCALE, MOD_SHIFT, MOD_GATE)]
        + [_full(v.shape) for v in vecs] + _weight_specs(weights) + part_specs
        + [pl.BlockSpec((1, T_NEW, 9 * QW), seq_of)]
        + [pl.BlockSpec((1, 2 * QW, win), seq_of) for win in WINDOWS])
    scratch.append(pltpu.VMEM((LANES, 6 * QW), f32))
    return pl.pallas_call(
        functools.partial(_rest_kernel, sample=False, sample_every=sample_every),
        grid=(b, tiles),
        in_specs=in_specs,
        out_specs=[pl.BlockSpec((1, TILE, D_MODEL), tile3), pl.BlockSpec((1, T_NEW, QW), seq_of)],
        out_shape=[jax.ShapeDtypeStruct((b, SEQ, D_MODEL), f32),
                   jax.ShapeDtypeStruct((n_seq, T_NEW, QW), f32)],
        scratch_shapes=scratch,
        name="rest_p",
        compiler_params=pltpu.CompilerParams(
            dimension_semantics=("arbitrary", "arbitrary"), vmem_limit_bytes=VMEM_LIMIT),
    )(x, mod, mod, mod, *vecs, *weights, *part_args, qkv_s, *caches)


def _rest_sample(x, scale, shift, gate, vecs, weights, attn):
    n = x.shape[0]
    args = (x, scale, shift, gate, *vecs, *weights, attn)
    in_specs = ([_full(a.shape) for a in (x, scale, shift, gate, *vecs)] + _weight_specs(weights)
                + [_full(attn.shape)])
    return pl.pallas_call(
        functools.partial(_rest_kernel, sample=True),
        in_specs=in_specs,
        out_specs=[_full((n, D_MODEL)), _full((n, D_MODEL))],
        out_shape=[jax.ShapeDtypeStruct((n, D_MODEL), f32), jax.ShapeDtypeStruct((n, D_MODEL), f32)],
        grid=(1,),
        name="rest_s",
        compiler_params=pltpu.CompilerParams(
            dimension_semantics=("arbitrary",), vmem_limit_bytes=VMEM_LIMIT),
    )(*args)


def kernel(x_prompt, x_sample, cache_kv_w128, cache_kv_w512, cache_kv_w2048, c_prompt, c_sample, w_cond, b_cond, g_pre, w_in, ln_v_g, ln_v_b, w_spatial, b_spatial, w_proj_a, w_proj_b, w_out, g_post):
    assert w_in.shape[0] == 1, "single layer"
    bp, seq, _ = x_prompt.shape
    bs, t_new, _ = x_sample.shape
    assert seq == SEQ and t_new == T_NEW

    w_in_b = w_in[0].astype(bf16)
    w_kvt = jnp.transpose(w_in[0][:, K_COL0:ZB_COL0].reshape(D_MODEL, 2, 3, QW), (2, 1, 3, 0))
    w_kvt = w_kvt.reshape(3, 2 * QW, D_MODEL).astype(bf16)
    causal = jnp.tril(jnp.ones((CHUNK, CHUNK), bool))
    w_sp = jnp.where(causal[None], w_spatial[0], 0.0)
    w_sp_p = w_sp.astype(bf16)
    n_s = bs * T_NEW
    b_sp_p = b_spatial[0].T
    b_sp_s = jnp.tile(b_spatial[0][:, :T_NEW].T, (bs, 1))
    weights_tail = (w_proj_a[0].astype(bf16), w_proj_b[0].astype(bf16), w_out[0].astype(bf16))
    vecs = (g_pre, g_post, ln_v_g, ln_v_b)

    mod = _cond(jnp.concatenate([c_prompt, c_sample], axis=0), w_cond[0], b_cond)
    shift, scale, gate = mod[:, :D_MODEL], mod[:, D_MODEL:2 * D_MODEL], mod[:, 2 * D_MODEL:]
    mod_p = mod.reshape(bp + bs, 1, 3 * D_MODEL)
    ms = lambda a: jnp.repeat(a[bp:], T_NEW, axis=0)

    xs = x_sample.reshape(n_s, D_MODEL)
    qkv_s, *kv_s = _qkv_sample(xs, ms(scale), ms(shift), g_pre, w_in_b)
    caches_t = []
    for cache, win in zip((cache_kv_w128, cache_kv_w512, cache_kv_w2048), WINDOWS):
        caches_t.append(jnp.transpose(cache[0], (0, 2, 3, 4, 1)).reshape(bs, 2 * QW, win))

    o_all, m_all, l_all, *kvt = _attn_prompt(x_prompt, mod_p, g_pre, w_in_b, w_kvt)
    in_proj = (w_in_b, w_in_b, w_in_b)
    y_p, attn_s = _rest_prompt(x_prompt, mod_p, vecs,
                               in_proj + (w_sp_p, b_sp_p) + weights_tail, (o_all, m_all, l_all),
                               qkv_s.reshape(bs, T_NEW, 9 * QW), caches_t)
    kv_p = [jnp.transpose(a.reshape(bp, 2, N_HEADS, HEAD_DIM, win), (0, 4, 1, 2, 3))[None]
            for a, win in zip(kvt, WINDOWS)]

    attn_s = attn_s.reshape(n_s, QW)
    y_s, v_n_s = _rest_sample(xs, ms(scale), ms(shift), ms(gate), vecs,
                              in_proj + (w_sp_p, b_sp_s) + weights_tail, attn_s)
    kv_s = [a.reshape(1, bs, T_NEW, 2, N_HEADS, HEAD_DIM) for a in kv_s]

    return (y_p, y_s.reshape(bs, T_NEW, D_MODEL), kv_p[0], kv_p[1], kv_p[2],
            kv_s[0], kv_s[1], kv_s[2], v_n_s.reshape(1, bs, T_NEW, D_MODEL))
```

```python
import functools

import jax
import jax.numpy as jnp
from jax import lax
from jax.experimental import pallas as pl
from jax.experimental.pallas import tpu as pltpu

D_MODEL = 1024
SEQ = 2048
HEAD_DIM = 64
N_HEADS = 8
QW = N_HEADS * HEAD_DIM
WINDOWS = (128, 512, 2048)
DILATIONS = (1, 4, 16)
CHUNK = 128
A_GROUPS = 8
EPS = 1e-6
NEG = -1e30
LANES = 128
VMEM_LIMIT = 56 * 1024 * 1024
Q_COL0 = 3 * D_MODEL
K_COL0 = Q_COL0 + 3 * QW
V_COL0 = K_COL0 + 3 * QW
ZB_COL0 = V_COL0 + 3 * QW
GATE_COL0 = ZB_COL0 + QW
MOD_SHIFT, MOD_SCALE, MOD_GATE = 0, 1, 2

f32 = jnp.float32
bf16 = jnp.bfloat16


def _silu(x):
    return x * jax.nn.sigmoid(x)


def _norm_mod(x, g_pre, scale, shift):
    y = x * lax.rsqrt(jnp.mean(x * x, axis=-1, keepdims=True) + EPS) * g_pre
    return y * (1.0 + scale) + shift


def _dot(a, b):
    return jnp.dot(a, b, preferred_element_type=f32)


def _dot_nt(a, b):
    return lax.dot_general(a, b, (((1,), (1,)), ((), ())), preferred_element_type=f32)


def _cond_kernel(c_ref, w_ref, b_ref, o_ref):
    o_ref[...] = _dot(_silu(c_ref[...]).astype(bf16), w_ref[...].astype(bf16)) + b_ref[...]


def _cond(c_all, w_cond, b_cond):
    n = c_all.shape[0]
    return pl.pallas_call(
        _cond_kernel,
        out_shape=jax.ShapeDtypeStruct((n, 3 * D_MODEL), f32),
        name="cond",
        compiler_params=pltpu.CompilerParams(vmem_limit_bytes=VMEM_LIMIT),
    )(c_all, w_cond, b_cond)


ROWS_PER_STEP = 512
Q_BLOCK = 128
NORM_ROWS = 256


def _attn_step(c, is_last, wq_ref, wk_ref, wv_ref, wkvt_ref, o_ref, m_ref, l_ref, kvt_ref,
               h_sc, k_sc, ve_sc, vo_sc, *, dil, window):
    seg = SEQ // dil
    n_lc = D_MODEL // LANES

    def rows_of(start, n):
        idx = pl.ds(pl.multiple_of(start, n), n) if dil == 1 else pl.ds(start, n, stride=dil)
        return jnp.concatenate([h_sc[k, idx, :] for k in range(n_lc)], axis=1)

    if seg >= ROWS_PER_STEP:
        per = seg // ROWS_PER_STEP
        h = rows_of((c // per) + (c % per) * ROWS_PER_STEP * dil, ROWS_PER_STEP)
    else:
        per = ROWS_PER_STEP // seg
        h = jnp.concatenate([rows_of(per * c + i, seg) for i in range(per)], axis=0)
    h = h.astype(bf16)
    base = pl.multiple_of(c * ROWS_PER_STEP, ROWS_PER_STEP)

    def tokens_t(start, n):
        return jnp.concatenate([h_sc[k, pl.ds(start, n), :] for k in range(n_lc)], axis=1).astype(bf16)

    if window == SEQ:
        kvt_ref[0] = _dot_nt(wkvt_ref[0], tokens_t(base, ROWS_PER_STEP))
    elif is_last:
        kvt_ref[0] = _dot_nt(wkvt_ref[0], tokens_t(SEQ - window, window))

    low_w = (lax.broadcasted_iota(jnp.int32, (ROWS_PER_STEP, QW), 1) % LANES) < HEAD_DIM
    new_rows = pl.ds(Q_BLOCK + base, ROWS_PER_STEP)
    k_sc[new_rows, :] = _dot(h, wk_ref[...]).astype(bf16)

    def project_v():
        v = _dot(h, wv_ref[...])
        ve_sc[new_rows, :] = jnp.where(low_w, v, 1.0).astype(bf16)
        vo_sc[new_rows, :] = jnp.where(low_w, 1.0, v).astype(bf16)

    low_q = (lax.broadcasted_iota(jnp.int32, (Q_BLOCK, QW), 1) % LANES) < HEAD_DIM

    def project_q(j):
        q = _dot(h[j * Q_BLOCK:(j + 1) * Q_BLOCK], wq_ref[...]) * (HEAD_DIM ** -0.5)
        return jnp.where(low_q, q, 0.0).astype(bf16), jnp.where(low_q, 0.0, q).astype(bf16)

    n_blocks = ROWS_PER_STEP // Q_BLOCK
    pairs = N_HEADS // 2
    qi = lax.broadcasted_iota(jnp.int32, (2 * Q_BLOCK, 2 * Q_BLOCK), 0) % Q_BLOCK
    kj = lax.broadcasted_iota(jnp.int32, (2 * Q_BLOCK, 2 * Q_BLOCK), 1)
    lane = lax.broadcasted_iota(jnp.int32, (Q_BLOCK, LANES), 1)
    low = lane < HEAD_DIM

    def has_prev(j):
        return seg > ROWS_PER_STEP or (j * Q_BLOCK) % seg != 0

    def key_rows(j):
        r0 = pl.multiple_of(base + j * Q_BLOCK, Q_BLOCK)
        return pl.ds(r0, 2 * Q_BLOCK) if has_prev(j) else pl.ds(r0 + Q_BLOCK, Q_BLOCK)

    def mask_of(j):
        if not has_prev(j):
            own = (2 * Q_BLOCK, Q_BLOCK)
            return (lax.broadcasted_iota(jnp.int32, own, 1)
                    <= lax.broadcasted_iota(jnp.int32, own, 0) % Q_BLOCK)
        off = jnp.where((base % seg) != 0, 0, Q_BLOCK) if (seg > ROWS_PER_STEP and j == 0) else 0
        in_prev = jnp.logical_and(kj < Q_BLOCK, kj >= qi + off)
        in_cur = jnp.logical_and(kj >= Q_BLOCK, kj - Q_BLOCK <= qi)
        return jnp.logical_or(in_prev, in_cur)

    def qk(j, q_even, q_odd):
        keys = key_rows(j)
        out = []
        for hp in range(pairs):
            cols = slice(hp * LANES, (hp + 1) * LANES)
            q_st = jnp.concatenate([q_even[:, cols], q_odd[:, cols]], axis=0)
            out.append(_dot_nt(q_st, k_sc[keys, cols]))
        return out

    def finish(j, scores):
        rows = slice(j * Q_BLOCK, (j + 1) * Q_BLOCK)
        keys = key_rows(j)
        mask = mask_of(j)
        ps, ms = [], []
        for hp in range(pairs):
            s = jnp.where(mask, scores[hp], NEG)
            for half in (s[:Q_BLOCK], s[Q_BLOCK:]):
                m = jnp.max(half, axis=-1, keepdims=True)
                ps.append(jnp.exp(half - m).astype(bf16))
                ms.append(m)
        o_parts = []
        m_tile = jnp.zeros((Q_BLOCK, LANES), f32)
        l_tile = jnp.ones((Q_BLOCK, LANES), f32)
        for hp in range(pairs):
            cols = slice(hp * LANES, (hp + 1) * LANES)
            acc_e = _dot(ps[2 * hp], ve_sc[keys, cols])
            acc_o = _dot(ps[2 * hp + 1], vo_sc[keys, cols])
            o_parts.append(jnp.where(low, acc_e, acc_o))
            l_tile = jnp.where(lane == HEAD_DIM + hp, acc_e, jnp.where(lane == hp, acc_o, l_tile))
            m_tile = jnp.where(lane == HEAD_DIM + hp, ms[2 * hp],
                               jnp.where(lane == hp, ms[2 * hp + 1], m_tile))
        o_ref[0, rows, :] = jnp.concatenate(o_parts, axis=1).astype(bf16)
        m_ref[0, rows, :] = m_tile
        l_ref[0, rows, :] = l_tile

    qs = {j: project_q(j) for j in range(min(2, n_blocks))}
    scores = qk(0, *qs[0])
    project_v()
    for j in range(n_blocks):
        nxt = qk(j + 1, *qs[j + 1]) if j + 1 < n_blocks else None
        if j + 2 < n_blocks:
            qs[j + 2] = project_q(j + 2)
        finish(j, scores)
        scores = nxt


def _attn_kernel(x_ref, scale_ref, shift_ref, gpre_ref, wq_ref, wk_ref, wv_ref, wkvt_ref,
                 o_ref, m_ref, l_ref, kvt0_ref, kvt1_ref, kvt2_ref, h_sc, k_sc, ve_sc, vo_sc):
    g = pl.program_id(1)
    c = pl.program_id(2)
    last_step = pl.num_programs(2) - 1

    kvt_refs = (kvt0_ref, kvt1_ref, kvt2_ref)

    def step(pattern, step_index, is_last):
        _attn_step(step_index, is_last, wq_ref, wk_ref, wv_ref, wkvt_ref, o_ref, m_ref, l_ref,
                   kvt_refs[pattern], h_sc, k_sc, ve_sc, vo_sc,
                   dil=DILATIONS[pattern], window=WINDOWS[pattern])

    @pl.when(jnp.logical_and(g == 0, c == 0))
    def _():
        zeros = jnp.zeros((Q_BLOCK, QW), bf16)
        k_sc[0:Q_BLOCK, :] = zeros
        ve_sc[0:Q_BLOCK, :] = zeros
        vo_sc[0:Q_BLOCK, :] = zeros
        for rb in range(SEQ // NORM_ROWS):
            rs = slice(rb * NORM_ROWS, (rb + 1) * NORM_ROWS)
            hb = _norm_mod(x_ref[0, rs, :], gpre_ref[...], scale_ref[0], shift_ref[0])
            for k in range(D_MODEL // LANES):
                h_sc[k, rs, :] = hb[:, k * LANES:(k + 1) * LANES]
        step(0, jnp.int32(0), False)

    for pattern in range(len(DILATIONS)):
        for is_last in ((False,) if WINDOWS[pattern] == SEQ else (False, True)):
            mine = g == pattern
            if WINDOWS[pattern] != SEQ:
                mine = jnp.logical_and(mine, (c == last_step) == is_last)
            if pattern == 0:
                mine = jnp.logical_and(mine, c > 0)

            @pl.when(mine)
            def _(pattern=pattern, is_last=is_last):
                step(pattern, c, is_last)


def _attn_prompt(x, mod, g_pre, w_in_b, w_kvt):
    b = x.shape[0]
    n_pat = len(DILATIONS)
    steps = SEQ // ROWS_PER_STEP
    full_window = WINDOWS.index(SEQ)
    per_b = lambda i, g, c: (i, 0, 0)
    step_rows = lambda i, g, c: (g, i, c, 0)

    def kvt_spec(pattern):
        if WINDOWS[pattern] == SEQ:
            return pl.BlockSpec((1, 2 * QW, ROWS_PER_STEP),
                                lambda i, g, c: (i, 0, jnp.where(g == full_window, c, 0)))
        return pl.BlockSpec((1, 2 * QW, WINDOWS[pattern]), per_b)

    return pl.pallas_call(
        _attn_kernel,
        grid=(b, n_pat, steps),
        in_specs=[
            pl.BlockSpec((1, SEQ, D_MODEL), per_b),
            pl.BlockSpec((1, 1, D_MODEL), lambda i, g, c: (i, 0, MOD_SCALE)),
            pl.BlockSpec((1, 1, D_MODEL), lambda i, g, c: (i, 0, MOD_SHIFT)),
            pl.BlockSpec((1, D_MODEL), lambda i, g, c: (0, 0)),
            pl.BlockSpec((D_MODEL, QW), lambda i, g, c: (0, Q_COL0 // QW + g)),
            pl.BlockSpec((D_MODEL, QW), lambda i, g, c: (0, K_COL0 // QW + g)),
            pl.BlockSpec((D_MODEL, QW), lambda i, g, c: (0, V_COL0 // QW + g)),
            pl.BlockSpec((1, 2 * QW, D_MODEL), lambda i, g, c: (g, 0, 0)),
        ],
        out_specs=[
            pl.BlockSpec((None, 1, ROWS_PER_STEP, QW), step_rows),
            pl.BlockSpec((None, 1, ROWS_PER_STEP, LANES), step_rows),
            pl.BlockSpec((None, 1, ROWS_PER_STEP, LANES), step_rows),
        ] + [kvt_spec(p) for p in range(n_pat)],
        out_shape=[
            jax.ShapeDtypeStruct((n_pat, b, SEQ, QW), bf16),
            jax.ShapeDtypeStruct((n_pat, b, SEQ, LANES), f32),
            jax.ShapeDtypeStruct((n_pat, b, SEQ, LANES), f32),
        ] + [jax.ShapeDtypeStruct((b, 2 * QW, win), f32) for win in WINDOWS],
        scratch_shapes=[pltpu.VMEM((D_MODEL // LANES, SEQ, LANES), f32),
                        pltpu.VMEM((SEQ + Q_BLOCK, QW), bf16), pltpu.VMEM((SEQ + Q_BLOCK, QW), bf16),
                        pltpu.VMEM((SEQ + Q_BLOCK, QW), bf16)],
        name="attn_p",
        compiler_params=pltpu.CompilerParams(
            dimension_semantics=("arbitrary", "arbitrary", "arbitrary"), vmem_limit_bytes=VMEM_LIMIT),
    )(x, mod, mod, g_pre, w_in_b, w_in_b, w_in_b, w_kvt)


def _qkv_s_kernel(x_ref, scale_ref, shift_ref, gpre_ref, w_ref, o_ref, *kv_refs):
    h = _norm_mod(x_ref[...], gpre_ref[...], scale_ref[...], shift_ref[...]).astype(bf16)
    tile = _dot(h, w_ref[...])
    o_ref[...] = tile
    for half in range(2):
        @pl.when(pl.program_id(0) == 1 + half)
        def _(half=half):
            for g, kv_ref in enumerate(kv_refs):
                kv_ref[:, half * QW:(half + 1) * QW] = tile[:, g * QW:(g + 1) * QW]


def _qkv_sample(x, scale, shift, g_pre, w_in_b):
    n = x.shape[0]
    n_pat = len(DILATIONS)
    cols = 3 * n_pat * QW
    tile = n_pat * QW
    first = Q_COL0 // tile
    return pl.pallas_call(
        _qkv_s_kernel,
        grid=(cols // tile,),
        in_specs=[
            pl.BlockSpec((n, D_MODEL), lambda j: (0, 0)),
            pl.BlockSpec((n, D_MODEL), lambda j: (0, 0)),
            pl.BlockSpec((n, D_MODEL), lambda j: (0, 0)),
            pl.BlockSpec((1, D_MODEL), lambda j: (0, 0)),
            pl.BlockSpec((D_MODEL, tile), lambda j: (0, first + j)),
        ],
        out_specs=[pl.BlockSpec((n, tile), lambda j: (0, j))]
        + [pl.BlockSpec((n, 2 * QW), lambda j: (0, 0))] * n_pat,
        out_shape=[jax.ShapeDtypeStruct((n, cols), f32)]
        + [jax.ShapeDtypeStruct((n, 2 * QW), f32)] * n_pat,
        name="qkv_s",
        compiler_params=pltpu.CompilerParams(
            dimension_semantics=("arbitrary",), vmem_limit_bytes=VMEM_LIMIT),
    )(x, scale, shift, g_pre, w_in_b)


T_NEW = 8


def _sample_attention(qkv_ref, c0_ref, c1_ref, c2_ref, o_ref, new_sc):
    qkv = qkv_ref[0]
    new_sc[0:T_NEW, :] = qkv[:, 3 * QW:]
    q_all = qkv[:, :3 * QW] * (HEAD_DIM ** -0.5)

    lane8 = lax.broadcasted_iota(jnp.int32, (2 * T_NEW, LANES), 1)
    low = lane8 < HEAD_DIM
    t_idx = lax.broadcasted_iota(jnp.int32, (2 * T_NEW, LANES), 0) % T_NEW
    caches = (c0_ref, c1_ref, c2_ref)
    pairs = range(N_HEADS // 2)
    pats = range(len(DILATIONS))

    def pair_cols(g, hp, base=0):
        return slice(base + g * QW + hp * LANES, base + g * QW + (hp + 1) * LANES)

    s_cache, s_new = {}, {}
    for hp in pairs:
        for g in pats:
            q2 = q_all[:, pair_cols(g, hp)]
            q_st = jnp.concatenate([jnp.where(low[:T_NEW], q2, 0.0), jnp.where(low[:T_NEW], 0.0, q2)],
                                   axis=0).astype(bf16)
            kt = caches[g][0, hp * LANES:(hp + 1) * LANES, :].astype(bf16)
            k_new = new_sc[:, pair_cols(g, hp)].astype(bf16)
            s_cache[hp, g] = _dot(q_st, kt)
            s_new[hp, g] = _dot_nt(q_st, k_new)

    probs = {}
    for hp in pairs:
        m = jnp.full((2 * T_NEW, 1), NEG, f32)
        for g, dil in enumerate(DILATIONS):
            rows = WINDOWS[g]
            rho = lax.broadcasted_iota(jnp.int32, (2 * T_NEW, rows), 1)
            tq = lax.broadcasted_iota(jnp.int32, (2 * T_NEW, rows), 0) % T_NEW
            ok_c = jnp.logical_and(rho >= tq, ((rho - tq) & (dil - 1)) == 0)
            ok_n = jnp.logical_and(lane8 <= t_idx, ((t_idx - lane8) & (dil - 1)) == 0)
            s_cache[hp, g] = jnp.where(ok_c, s_cache[hp, g], NEG)
            s_new[hp, g] = jnp.where(ok_n, s_new[hp, g], NEG)
            m = jnp.maximum(m, jnp.maximum(jnp.max(s_cache[hp, g], axis=-1, keepdims=True),
                                           jnp.max(s_new[hp, g], axis=-1, keepdims=True)))
        l = jnp.zeros((2 * T_NEW, 1), f32)
        for g in pats:
            p_c = jnp.exp(s_cache[hp, g] - m)
            p_n = jnp.exp(s_new[hp, g] - m)
            l = l + jnp.sum(p_c, axis=-1, keepdims=True) + jnp.sum(p_n, axis=-1, keepdims=True)
            probs[hp, g] = (p_c.astype(bf16), p_n.astype(bf16))
        probs[hp] = l

    o_parts = []
    for hp in pairs:
        acc = jnp.zeros((2 * T_NEW, LANES), f32)
        for g in pats:
            vt = caches[g][0, QW + hp * LANES: QW + (hp + 1) * LANES, :].astype(bf16)
            v_new = new_sc[:, pair_cols(g, hp, 3 * QW)].astype(bf16)
            p_c, p_n = probs[hp, g]
            acc = acc + _dot_nt(p_c, vt) + _dot(p_n, v_new)
        out = acc / probs[hp]
        o_parts.append(jnp.where(low[:T_NEW], out[:T_NEW], out[T_NEW:]))
    o_ref[0] = jnp.concatenate(o_parts, axis=1)


TILE = 256


def _expand_heads(w):
    r = lax.broadcasted_iota(jnp.int32, (2 * LANES, QW), 0) % LANES
    head = lax.broadcasted_iota(jnp.int32, (2 * LANES, QW), 1) // HEAD_DIM
    expand = (r == head // 2 + HEAD_DIM * (1 - head % 2)).astype(bf16)
    hi = w.astype(bf16)
    lo = (w - hi.astype(f32)).astype(bf16)
    return _dot(jnp.concatenate([hi, lo], axis=1), expand)


def _rest_kernel(*refs, sample, sample_every=None):
    if sample:
        (x_ref, scale_ref, shift_ref, gate_ref, gpre_ref, gpost_ref, lng_ref, lnb_ref,
         wa_ref, wzb_ref, wg_ref, wsp_ref, bsp_ref, wpa_ref, wpb_ref, wout_ref,
         attn_ref, y_ref, vn_ref) = refs
        x = x_ref[...]
        scale, shift, gate = scale_ref[...], shift_ref[...], gate_ref[...]
    else:
        (x_ref, scale_ref, shift_ref, gate_ref, gpre_ref, gpost_ref, lng_ref, lnb_ref,
         wa_ref, wzb_ref, wg_ref, wsp_ref, bsp_ref, wpa_ref, wpb_ref, wout_ref,
         o1_ref, m1_ref, l1_ref, o4_ref, m4_ref, l4_ref, o16_ref, m16_ref, l16_ref,
         qkv_s_ref, c0_ref, c1_ref, c2_ref, y_ref, attn_s_ref,
         o4_sc, m4_sc, l4_sc, o16_sc, m16_sc, l16_sc, new_sc) = refs
        x = x_ref[0]
        scale, shift, gate = scale_ref[0], shift_ref[0], gate_ref[0]

        @pl.when(jnp.logical_and(pl.program_id(0) == 0, pl.program_id(1) == 0))
        def _():
            new_sc[...] = jnp.zeros_like(new_sc)

        @pl.when(pl.program_id(1) % sample_every == 0)
        def _():
            _sample_attention(qkv_s_ref, c0_ref, c1_ref, c2_ref, attn_s_ref, new_sc)

    rows = x.shape[0]

    if sample:
        attn = attn_ref[...]
    else:
        for osc, msc, lsc, oref, mref, lref, dil in (
                (o4_sc, m4_sc, l4_sc, o4_ref, m4_ref, l4_ref, DILATIONS[1]),
                (o16_sc, m16_sc, l16_sc, o16_ref, m16_ref, l16_ref, DILATIONS[2])):
            n = rows // dil
            for r in range(dil):
                o_r = oref[0, r].astype(f32)
                for k in range(QW // LANES):
                    osc[k, pl.ds(r, n, stride=dil), :] = o_r[:, k * LANES:(k + 1) * LANES]
                msc[pl.ds(r, n, stride=dil), :] = mref[0, r]
                lsc[pl.ds(r, n, stride=dil), :] = lref[0, r]
        ms = (m1_ref[0], m4_sc[...], m16_sc[...])
        ls = (l1_ref[0], l4_sc[...], l16_sc[...])
        unchunk = lambda sc: jnp.concatenate([sc[k] for k in range(QW // LANES)], axis=1)
        outs = (o1_ref[0].astype(f32), unchunk(o4_sc), unchunk(o16_sc))
        m_all = jnp.maximum(jnp.maximum(ms[0], ms[1]), ms[2])
        ws = [jnp.exp(m - m_all) for m in ms]
        den = ws[0] * ls[0] + ws[1] * ls[1] + ws[2] * ls[2]
        attn = sum(_expand_heads(w / den) * o for w, o in zip(ws, outs))

    h = _norm_mod(x, gpre_ref[...], scale, shift).astype(bf16)

    pa = _dot(h, wa_ref[...])
    u_a, v_a, z_a = pa[:, :D_MODEL], pa[:, D_MODEL:2 * D_MODEL], pa[:, 2 * D_MODEL:]
    mu = jnp.mean(v_a, axis=-1, keepdims=True)
    cen = v_a - mu
    var = jnp.mean(cen * cen, axis=-1, keepdims=True)
    v_n = cen * lax.rsqrt(var + EPS) * lng_ref[...] + lnb_ref[...]
    v_nb = v_n.astype(bf16)
    if sample:
        vn_ref[...] = v_n
        pick = (lax.broadcasted_iota(jnp.int32, (rows, CHUNK), 1)
                == lax.broadcasted_iota(jnp.int32, (rows, CHUNK), 0) % T_NEW).astype(bf16)
        same_seq = (lax.broadcasted_iota(jnp.int32, (rows, rows), 0) // T_NEW
                    == lax.broadcasted_iota(jnp.int32, (rows, rows), 1) // T_NEW)
        cols = []
        for g in range(A_GROUPS):
            tiled = _dot_nt(_dot(pick, wsp_ref[g]).astype(bf16), pick)
            w_blk = jnp.where(same_seq, tiled, 0.0).astype(bf16)
            cols.append(_dot(w_blk, v_nb[:, g * LANES:(g + 1) * LANES]) + bsp_ref[:, g:g + 1])
        zs = jnp.concatenate(cols, axis=1)
    else:
        n_ck = rows // CHUNK
        per_group = []
        for g in range(A_GROUPS):
            rhs = jnp.concatenate(
                [v_nb[ck * CHUNK:(ck + 1) * CHUNK, g * LANES:(g + 1) * LANES] for ck in range(n_ck)], axis=1)
            per_group.append(_dot(wsp_ref[g], rhs) + bsp_ref[:, g:g + 1])
        zs = jnp.concatenate(
            [jnp.concatenate([pg[:, ck * LANES:(ck + 1) * LANES] for pg in per_group], axis=1)
             for ck in range(n_ck)], axis=0)
    y_a = u_a * zs * _silu(z_a)

    z_b = _dot(h, wzb_ref[...])
    y_b = (attn * _silu(z_b)).astype(bf16)
    gl = _dot(h, wg_ref[...])
    p_b = _dot(y_b, wpb_ref[...])
    p_a = _dot(y_a.astype(bf16), wpa_ref[...])
    merged = jax.nn.sigmoid(gl[:, :D_MODEL]) * p_a + jax.nn.sigmoid(gl[:, D_MODEL:]) * p_b
    out = _dot(merged.astype(bf16), wout_ref[...])
    normed = out * lax.rsqrt(jnp.mean(out * out, axis=-1, keepdims=True) + EPS) * gpost_ref[...]
    y = x + gate * normed
    if sample:
        y_ref[...] = y
    else:
        y_ref[0] = y


def _full(shape):
    nd = len(shape)
    return pl.BlockSpec(shape, lambda *_: (0,) * nd)


def _weight_specs(weights):
    in_proj = [pl.BlockSpec((D_MODEL, 3 * D_MODEL), lambda *_: (0, 0)),
               pl.BlockSpec((D_MODEL, QW), lambda *_: (0, ZB_COL0 // QW)),
               pl.BlockSpec((D_MODEL, 2 * D_MODEL), lambda *_: (0, GATE_COL0 // (2 * D_MODEL)))]
    return in_proj + [_full(w.shape) for w in weights[3:]]


def _rest_prompt(x, mod, vecs, weights, attn_parts, qkv_s, caches):
    b = x.shape[0]
    n_pat = len(DILATIONS)
    tiles = SEQ // TILE
    n_seq = qkv_s.shape[0]
    sample_every, rem = divmod(b * tiles, n_seq)
    assert rem == 0 and tiles % sample_every == 0, "sample sequences must tile the grid evenly"
    seq_of = lambda i, c: (i * (tiles // sample_every) + c // sample_every, 0, 0)
    tile3 = lambda i, c: (i, c, 0)
    part_args, part_specs, scratch = [], [], []
    for pattern, dil in enumerate(DILATIONS):
        for a in attn_parts:
            lanes = a.shape[-1]
            if dil == 1:
                part_args.append(a)
                part_specs.append(pl.BlockSpec((None, 1, TILE, lanes),
                                               lambda i, c, p=pattern: (p, i, c, 0)))
            else:
                part_args.append(a.reshape(n_pat, b, dil, SEQ // dil, lanes))
                part_specs.append(pl.BlockSpec((None, 1, dil, TILE // dil, lanes),
                                               lambda i, c, p=pattern: (p, i, 0, c, 0)))
        if dil != 1:
            scratch += [pltpu.VMEM((QW // LANES, TILE, LANES), f32),
                        pltpu.VMEM((TILE, LANES), f32), pltpu.VMEM((TILE, LANES), f32)]
    in_specs = (
        [pl.BlockSpec((1, TILE, D_MODEL), tile3)]
        + [pl.BlockSpec((1, 1, D_MODEL), lambda i, c, col=col: (i, 0, col))
           for col in (MOD_SCALE, MOD_SHIFT, MOD_GATE)]
        + [_full(v.shape) for v in vecs] + _weight_specs(weights) + part_specs
        + [pl.BlockSpec((1, T_NEW, 9 * QW), seq_of)]
        + [pl.BlockSpec((1, 2 * QW, win), seq_of) for win in WINDOWS])
    scratch.append(pltpu.VMEM((LANES, 6 * QW), f32))
    return pl.pallas_call(
        functools.partial(_rest_kernel, sample=False, sample_every=sample_every),
        grid=(b, tiles),
        in_specs=in_specs,
        out_specs=[pl.BlockSpec((1, TILE, D_MODEL), tile3), pl.BlockSpec((1, T_NEW, QW), seq_of)],
        out_shape=[jax.ShapeDtypeStruct((b, SEQ, D_MODEL), f32),
                   jax.ShapeDtypeStruct((n_seq, T_NEW, QW), f32)],
        scratch_shapes=scratch,
        name="rest_p",
        compiler_params=pltpu.CompilerParams(
            dimension_semantics=("arbitrary", "arbitrary"), vmem_limit_bytes=VMEM_LIMIT),
    )(x, mod, mod, mod, *vecs, *weights, *part_args, qkv_s, *caches)


def _rest_sample(x, scale, shift, gate, vecs, weights, attn):
    n = x.shape[0]
    args = (x, scale, shift, gate, *vecs, *weights, attn)
    in_specs = ([_full(a.shape) for a in (x, scale, shift, gate, *vecs)] + _weight_specs(weights)
                + [_full(attn.shape)])
    return pl.pallas_call(
        functools.partial(_rest_kernel, sample=True),
        in_specs=in_specs,
        out_specs=[_full((n, D_MODEL)), _full((n, D_MODEL))],
        out_shape=[jax.ShapeDtypeStruct((n, D_MODEL), f32), jax.ShapeDtypeStruct((n, D_MODEL), f32)],
        grid=(1,),
        name="rest_s",
        compiler_params=pltpu.CompilerParams(
            dimension_semantics=("arbitrary",), vmem_limit_bytes=VMEM_LIMIT),
    )(*args)


def kernel(x_prompt, x_sample, cache_kv_w128, cache_kv_w512, cache_kv_w2048, c_prompt, c_sample, w_cond, b_cond, g_pre, w_in, ln_v_g, ln_v_b, w_spatial, b_spatial, w_proj_a, w_proj_b, w_out, g_post):
    assert w_in.shape[0] == 1, "single layer"
    bp, seq, _ = x_prompt.shape
    bs, t_new, _ = x_sample.shape
    assert seq == SEQ and t_new == T_NEW

    w_in_b = w_in[0].astype(bf16)
    w_kvt = jnp.transpose(w_in[0][:, K_COL0:ZB_COL0].reshape(D_MODEL, 2, 3, QW), (2, 1, 3, 0))
    w_kvt = w_kvt.reshape(3, 2 * QW, D_MODEL).astype(bf16)
    causal = jnp.tril(jnp.ones((CHUNK, CHUNK), bool))
    w_sp = jnp.where(causal[None], w_spatial[0], 0.0)
    w_sp_p = w_sp.astype(bf16)
    n_s = bs * T_NEW
    b_sp_p = b_spatial[0].T
    b_sp_s = jnp.tile(b_spatial[0][:, :T_NEW].T, (bs, 1))
    weights_tail = (w_proj_a[0].astype(bf16), w_proj_b[0].astype(bf16), w_out[0].astype(bf16))
    vecs = (g_pre, g_post, ln_v_g, ln_v_b)

    mod = _cond(jnp.concatenate([c_prompt, c_sample], axis=0), w_cond[0], b_cond)
    shift, scale, gate = mod[:, :D_MODEL], mod[:, D_MODEL:2 * D_MODEL], mod[:, 2 * D_MODEL:]
    mod_p = mod.reshape(bp + bs, 1, 3 * D_MODEL)
    ms = lambda a: jnp.repeat(a[bp:], T_NEW, axis=0)

    xs = x_sample.reshape(n_s, D_MODEL)
    qkv_s, *kv_s = _qkv_sample(xs, ms(scale), ms(shift), g_pre, w_in_b)
    caches_t = []
    for cache, win in zip((cache_kv_w128, cache_kv_w512, cache_kv_w2048), WINDOWS):
        caches_t.append(jnp.transpose(cache[0], (0, 2, 3, 4, 1)).reshape(bs, 2 * QW, win))

    o_all, m_all, l_all, *kvt = _attn_prompt(x_prompt, mod_p, g_pre, w_in_b, w_kvt)
    in_proj = (w_in_b, w_in_b, w_in_b)
    y_p, attn_s = _rest_prompt(x_prompt, mod_p, vecs,
                               in_proj + (w_sp_p, b_sp_p) + weights_tail, (o_all, m_all, l_all),
                               qkv_s.reshape(bs, T_NEW, 9 * QW), caches_t)
    kv_p = [jnp.transpose(a.reshape(bp, 2, N_HEADS, HEAD_DIM, win), (0, 4, 1, 2, 3))[None]
            for a, win in zip(kvt, WINDOWS)]

    attn_s = attn_s.reshape(n_s, QW)
    y_s, v_n_s = _rest_sample(xs, ms(scale), ms(shift), ms(gate), vecs,
                              in_proj + (w_sp_p, b_sp_s) + weights_tail, attn_s)
    kv_s = [a.reshape(1, bs, T_NEW, 2, N_HEADS, HEAD_DIM) for a in kv_s]

    return (y_p, y_s.reshape(bs, T_NEW, D_MODEL), kv_p[0], kv_p[1], kv_p[2],
            kv_s[0], kv_s[1], kv_s[2], v_n_s.reshape(1, bs, T_NEW, D_MODEL))
```

```python
import functools

import jax
import jax.numpy as jnp
from jax import lax
from jax.experimental import pallas as pl
from jax.experimental.pallas import tpu as pltpu

D_MODEL = 1024
SEQ = 2048
HEAD_DIM = 64
N_HEADS = 8
QW = N_HEADS * HEAD_DIM
WINDOWS = (128, 512, 2048)
DILATIONS = (1, 4, 16)
CHUNK = 128
A_GROUPS = 8
EPS = 1e-6
NEG = -1e30
LANES = 128
VMEM_LIMIT = 56 * 1024 * 1024
Q_COL0 = 3 * D_MODEL
K_COL0 = Q_COL0 + 3 * QW
V_COL0 = K_COL0 + 3 * QW
ZB_COL0 = V_COL0 + 3 * QW
GATE_COL0 = ZB_COL0 + QW
MOD_SHIFT, MOD_SCALE, MOD_GATE = 0, 1, 2

f32 = jnp.float32
bf16 = jnp.bfloat16


def _silu(x):
    return x * jax.nn.sigmoid(x)


def _norm_mod(x, g_pre, scale, shift):
    y = x * lax.rsqrt(jnp.mean(x * x, axis=-1, keepdims=True) + EPS) * g_pre
    return y * (1.0 + scale) + shift


def _dot(a, b):
    return jnp.dot(a, b, preferred_element_type=f32)


def _dot_nt(a, b):
    return lax.dot_general(a, b, (((1,), (1,)), ((), ())), preferred_element_type=f32)


def _cond_kernel(c_ref, w_ref, b_ref, o_ref):
    o_ref[...] = _dot(_silu(c_ref[...]).astype(bf16), w_ref[...].astype(bf16)) + b_ref[...]


def _cond(c_all, w_cond, b_cond):
    n = c_all.shape[0]
    return pl.pallas_call(
        _cond_kernel,
        out_shape=jax.ShapeDtypeStruct((n, 3 * D_MODEL), f32),
        name="cond",
        compiler_params=pltpu.CompilerParams(vmem_limit_bytes=VMEM_LIMIT),
    )(c_all, w_cond, b_cond)


ROWS_PER_STEP = 512
Q_BLOCK = 128
NORM_ROWS = 256


def _attn_step(c, last_step, wq_ref, wk_ref, wv_ref, wkvt_ref, o_ref, ml_ref, kvt_ref,
               h_sc, k_sc, ve_sc, vo_sc, *, dil, window):
    seg = SEQ // dil
    n_lc = D_MODEL // LANES

    def rows_of(start, n):
        idx = pl.ds(pl.multiple_of(start, n), n) if dil == 1 else pl.ds(start, n, stride=dil)
        return jnp.concatenate([h_sc[k, idx, :] for k in range(n_lc)], axis=1)

    if seg >= ROWS_PER_STEP:
        per = seg // ROWS_PER_STEP
        h = rows_of((c // per) + (c % per) * ROWS_PER_STEP * dil, ROWS_PER_STEP)
    else:
        per = ROWS_PER_STEP // seg
        h = jnp.concatenate([rows_of(per * c + i, seg) for i in range(per)], axis=0)
    h = h.astype(bf16)
    base = pl.multiple_of(c * ROWS_PER_STEP, ROWS_PER_STEP)

    def tokens_t(start, n):
        return jnp.concatenate([h_sc[k, pl.ds(start, n), :] for k in range(n_lc)], axis=1).astype(bf16)

    if window == SEQ:
        kvt_ref[0] = _dot_nt(wkvt_ref[0], tokens_t(base, ROWS_PER_STEP))
    else:
        @pl.when(c == last_step)
        def _():
            kvt_ref[0] = _dot_nt(wkvt_ref[0], tokens_t(SEQ - window, window))

    low_w = (lax.broadcasted_iota(jnp.int32, (ROWS_PER_STEP, QW), 1) % LANES) < HEAD_DIM
    new_rows = pl.ds(Q_BLOCK + base, ROWS_PER_STEP)
    k_sc[new_rows, :] = _dot(h, wk_ref[...]).astype(bf16)

    def project_v():
        v = _dot(h, wv_ref[...])
        ve_sc[new_rows, :] = jnp.where(low_w, v, 1.0).astype(bf16)
        vo_sc[new_rows, :] = jnp.where(low_w, 1.0, v).astype(bf16)

    low_q = (lax.broadcasted_iota(jnp.int32, (Q_BLOCK, QW), 1) % LANES) < HEAD_DIM

    def project_q(j):
        q = _dot(h[j * Q_BLOCK:(j + 1) * Q_BLOCK], wq_ref[...]) * (HEAD_DIM ** -0.5)
        return jnp.where(low_q, q, 0.0).astype(bf16), jnp.where(low_q, 0.0, q).astype(bf16)

    n_blocks = ROWS_PER_STEP // Q_BLOCK
    pairs = N_HEADS // 2
    qi = lax.broadcasted_iota(jnp.int32, (2 * Q_BLOCK, 2 * Q_BLOCK), 0) % Q_BLOCK
    kj = lax.broadcasted_iota(jnp.int32, (2 * Q_BLOCK, 2 * Q_BLOCK), 1)
    lane = lax.broadcasted_iota(jnp.int32, (Q_BLOCK, LANES), 1)
    low = lane < HEAD_DIM

    def has_prev(j):
        return seg > ROWS_PER_STEP or (j * Q_BLOCK) % seg != 0

    def key_rows(j):
        r0 = pl.multiple_of(base + j * Q_BLOCK, Q_BLOCK)
        return pl.ds(r0, 2 * Q_BLOCK) if has_prev(j) else pl.ds(r0 + Q_BLOCK, Q_BLOCK)

    def mask_of(j):
        if not has_prev(j):
            own = (2 * Q_BLOCK, Q_BLOCK)
            return (lax.broadcasted_iota(jnp.int32, own, 1)
                    <= lax.broadcasted_iota(jnp.int32, own, 0) % Q_BLOCK)
        off = jnp.where((base % seg) != 0, 0, Q_BLOCK) if (seg > ROWS_PER_STEP and j == 0) else 0
        in_prev = jnp.logical_and(kj < Q_BLOCK, kj >= qi + off)
        in_cur = jnp.logical_and(kj >= Q_BLOCK, kj - Q_BLOCK <= qi)
        return jnp.logical_or(in_prev, in_cur)

    def qk(j, q_even, q_odd):
        keys = key_rows(j)
        out = []
        for hp in range(pairs):
            cols = slice(hp * LANES, (hp + 1) * LANES)
            q_st = jnp.concatenate([q_even[:, cols], q_odd[:, cols]], axis=0)
            out.append(_dot_nt(q_st, k_sc[keys, cols]))
        return out

    def finish(j, scores):
        rows = slice(j * Q_BLOCK, (j + 1) * Q_BLOCK)
        keys = key_rows(j)
        mask = mask_of(j)
        ps, ms = [], []
        for hp in range(pairs):
            s = jnp.where(mask, scores[hp], NEG)
            for half in (s[:Q_BLOCK], s[Q_BLOCK:]):
                m = jnp.max(half, axis=-1, keepdims=True)
                ps.append(jnp.exp(half - m).astype(bf16))
                ms.append(m)
        o_parts = []
        m_tile = jnp.zeros((Q_BLOCK, LANES), f32)
        l_tile = jnp.ones((Q_BLOCK, LANES), f32)
        for hp in range(pairs):
            cols = slice(hp * LANES, (hp + 1) * LANES)
            acc_e = _dot(ps[2 * hp], ve_sc[keys, cols])
            acc_o = _dot(ps[2 * hp + 1], vo_sc[keys, cols])
            o_parts.append(jnp.where(low, acc_e, acc_o))
            l_tile = jnp.where(lane == HEAD_DIM + hp, acc_e, jnp.where(lane == hp, acc_o, l_tile))
            m_tile = jnp.where(lane == HEAD_DIM + hp, ms[2 * hp],
                               jnp.where(lane == hp, ms[2 * hp + 1], m_tile))
        o_ref[0, rows, :] = jnp.concatenate(o_parts, axis=1).astype(bf16)
        ml_ref[0, rows, :] = jnp.concatenate([m_tile, l_tile], axis=1)

    qs = {j: project_q(j) for j in range(min(2, n_blocks))}
    scores = qk(0, *qs[0])
    project_v()
    for j in range(n_blocks):
        nxt = qk(j + 1, *qs[j + 1]) if j + 1 < n_blocks else None
        if j + 2 < n_blocks:
            qs[j + 2] = project_q(j + 2)
        finish(j, scores)
        scores = nxt


def _attn_kernel(x_ref, scale_ref, shift_ref, gpre_ref, wq_ref, wk_ref, wv_ref, wkvt_ref,
                 o_ref, ml_ref, kvt0_ref, kvt1_ref, kvt2_ref, h_sc, k_sc, ve_sc, vo_sc):
    g = pl.program_id(1)
    c = pl.program_id(2)
    last_step = pl.num_programs(2) - 1

    kvt_refs = (kvt0_ref, kvt1_ref, kvt2_ref)

    def step(pattern, step_index):
        _attn_step(step_index, last_step, wq_ref, wk_ref, wv_ref, wkvt_ref, o_ref, ml_ref,
                   kvt_refs[pattern], h_sc, k_sc, ve_sc, vo_sc,
                   dil=DILATIONS[pattern], window=WINDOWS[pattern])

    @pl.when(jnp.logical_and(g == 0, c == 0))
    def _():
        zeros = jnp.zeros((Q_BLOCK, QW), bf16)
        k_sc[0:Q_BLOCK, :] = zeros
        ve_sc[0:Q_BLOCK, :] = zeros
        vo_sc[0:Q_BLOCK, :] = zeros
        for rb in range(SEQ // NORM_ROWS):
            rs = slice(rb * NORM_ROWS, (rb + 1) * NORM_ROWS)
            hb = _norm_mod(x_ref[0, rs, :], gpre_ref[...], scale_ref[0], shift_ref[0])
            for k in range(D_MODEL // LANES):
                h_sc[k, rs, :] = hb[:, k * LANES:(k + 1) * LANES]
        step(0, jnp.int32(0))

    for pattern in range(len(DILATIONS)):
        @pl.when(jnp.logical_and(g == pattern, jnp.logical_or(pattern > 0, c > 0)))
        def _(pattern=pattern):
            step(pattern, c)


def _attn_prompt(x, mod, g_pre, w_in_b, w_kvt):
    b = x.shape[0]
    n_pat = len(DILATIONS)
    steps = SEQ // ROWS_PER_STEP
    full_window = WINDOWS.index(SEQ)
    per_b = lambda i, g, c: (i, 0, 0)
    step_rows = lambda i, g, c: (g, i, c, 0)

    def kvt_spec(pattern):
        if WINDOWS[pattern] == SEQ:
            return pl.BlockSpec((1, 2 * QW, ROWS_PER_STEP),
                                lambda i, g, c: (i, 0, jnp.where(g == full_window, c, 0)))
        return pl.BlockSpec((1, 2 * QW, WINDOWS[pattern]), per_b)

    return pl.pallas_call(
        _attn_kernel,
        grid=(b, n_pat, steps),
        in_specs=[
            pl.BlockSpec((1, SEQ, D_MODEL), per_b),
            pl.BlockSpec((1, 1, D_MODEL), lambda i, g, c: (i, 0, MOD_SCALE)),
            pl.BlockSpec((1, 1, D_MODEL), lambda i, g, c: (i, 0, MOD_SHIFT)),
            pl.BlockSpec((1, D_MODEL), lambda i, g, c: (0, 0)),
            pl.BlockSpec((D_MODEL, QW), lambda i, g, c: (0, Q_COL0 // QW + g)),
            pl.BlockSpec((D_MODEL, QW), lambda i, g, c: (0, K_COL0 // QW + g)),
            pl.BlockSpec((D_MODEL, QW), lambda i, g, c: (0, V_COL0 // QW + g)),
            pl.BlockSpec((1, 2 * QW, D_MODEL), lambda i, g, c: (g, 0, 0)),
        ],
        out_specs=[
            pl.BlockSpec((None, 1, ROWS_PER_STEP, QW), step_rows),
            pl.BlockSpec((None, 1, ROWS_PER_STEP, 2 * LANES), step_rows),
        ] + [kvt_spec(p) for p in range(n_pat)],
        out_shape=[
            jax.ShapeDtypeStruct((n_pat, b, SEQ, QW), bf16),
            jax.ShapeDtypeStruct((n_pat, b, SEQ, 2 * LANES), f32),
        ] + [jax.ShapeDtypeStruct((b, 2 * QW, win), f32) for win in WINDOWS],
        scratch_shapes=[pltpu.VMEM((D_MODEL // LANES, SEQ, LANES), f32),
                        pltpu.VMEM((SEQ + Q_BLOCK, QW), bf16), pltpu.VMEM((SEQ + Q_BLOCK, QW), bf16),
                        pltpu.VMEM((SEQ + Q_BLOCK, QW), bf16)],
        name="attn_p",
        compiler_params=pltpu.CompilerParams(
            dimension_semantics=("arbitrary", "arbitrary", "arbitrary"), vmem_limit_bytes=VMEM_LIMIT),
    )(x, mod, mod, g_pre, w_in_b, w_in_b, w_in_b, w_kvt)


def _qkv_s_kernel(x_ref, scale_ref, shift_ref, gpre_ref, w_ref, o_ref, *kv_refs):
    h = _norm_mod(x_ref[...], gpre_ref[...], scale_ref[...], shift_ref[...]).astype(bf16)
    tile = _dot(h, w_ref[...])
    o_ref[...] = tile
    for half in range(2):
        @pl.when(pl.program_id(0) == 1 + half)
        def _(half=half):
            for g, kv_ref in enumerate(kv_refs):
                kv_ref[:, half * QW:(half + 1) * QW] = tile[:, g * QW:(g + 1) * QW]


def _qkv_sample(x, scale, shift, g_pre, w_in_b):
    n = x.shape[0]
    n_pat = len(DILATIONS)
    cols = 3 * n_pat * QW
    tile = n_pat * QW
    first = Q_COL0 // tile
    return pl.pallas_call(
        _qkv_s_kernel,
        grid=(cols // tile,),
        in_specs=[
            pl.BlockSpec((n, D_MODEL), lambda j: (0, 0)),
            pl.BlockSpec((n, D_MODEL), lambda j: (0, 0)),
            pl.BlockSpec((n, D_MODEL), lambda j: (0, 0)),
            pl.BlockSpec((1, D_MODEL), lambda j: (0, 0)),
            pl.BlockSpec((D_MODEL, tile), lambda j: (0, first + j)),
        ],
        out_specs=[pl.BlockSpec((n, tile), lambda j: (0, j))]
        + [pl.BlockSpec((n, 2 * QW), lambda j: (0, 0))] * n_pat,
        out_shape=[jax.ShapeDtypeStruct((n, cols), f32)]
        + [jax.ShapeDtypeStruct((n, 2 * QW), f32)] * n_pat,
        name="qkv_s",
        compiler_params=pltpu.CompilerParams(
            dimension_semantics=("arbitrary",), vmem_limit_bytes=VMEM_LIMIT),
    )(x, scale, shift, g_pre, w_in_b)


T_NEW = 8


def _sample_attention(qkv_ref, c0_ref, c1_ref, c2_ref, o_ref, new_sc):
    qkv = qkv_ref[0]
    new_sc[0:T_NEW, :] = qkv[:, 3 * QW:]
    q_all = qkv[:, :3 * QW] * (HEAD_DIM ** -0.5)

    lane8 = lax.broadcasted_iota(jnp.int32, (2 * T_NEW, LANES), 1)
    low = lane8 < HEAD_DIM
    t_idx = lax.broadcasted_iota(jnp.int32, (2 * T_NEW, LANES), 0) % T_NEW
    caches = (c0_ref, c1_ref, c2_ref)
    pairs = range(N_HEADS // 2)
    pats = range(len(DILATIONS))

    def pair_cols(g, hp, base=0):
        return slice(base + g * QW + hp * LANES, base + g * QW + (hp + 1) * LANES)

    s_cache, s_new = {}, {}
    for hp in pairs:
        for g in pats:
            q2 = q_all[:, pair_cols(g, hp)]
            q_st = jnp.concatenate([jnp.where(low[:T_NEW], q2, 0.0), jnp.where(low[:T_NEW], 0.0, q2)],
                                   axis=0).astype(bf16)
            kt = caches[g][0, hp * LANES:(hp + 1) * LANES, :].astype(bf16)
            k_new = new_sc[:, pair_cols(g, hp)].astype(bf16)
            s_cache[hp, g] = _dot(q_st, kt)
            s_new[hp, g] = _dot_nt(q_st, k_new)

    probs = {}
    for hp in pairs:
        m = jnp.full((2 * T_NEW, 1), NEG, f32)
        for g, dil in enumerate(DILATIONS):
            rows = WINDOWS[g]
            rho = lax.broadcasted_iota(jnp.int32, (2 * T_NEW, rows), 1)
            tq = lax.broadcasted_iota(jnp.int32, (2 * T_NEW, rows), 0) % T_NEW
            ok_c = jnp.logical_and(rho >= tq, ((rho - tq) & (dil - 1)) == 0)
            ok_n = jnp.logical_and(lane8 <= t_idx, ((t_idx - lane8) & (dil - 1)) == 0)
            s_cache[hp, g] = jnp.where(ok_c, s_cache[hp, g], NEG)
            s_new[hp, g] = jnp.where(ok_n, s_new[hp, g], NEG)
            m = jnp.maximum(m, jnp.maximum(jnp.max(s_cache[hp, g], axis=-1, keepdims=True),
                                           jnp.max(s_new[hp, g], axis=-1, keepdims=True)))
        l = jnp.zeros((2 * T_NEW, 1), f32)
        for g in pats:
            p_c = jnp.exp(s_cache[hp, g] - m)
            p_n = jnp.exp(s_new[hp, g] - m)
            l = l + jnp.sum(p_c, axis=-1, keepdims=True) + jnp.sum(p_n, axis=-1, keepdims=True)
            probs[hp, g] = (p_c.astype(bf16), p_n.astype(bf16))
        probs[hp] = l

    o_parts = []
    for hp in pairs:
        acc = jnp.zeros((2 * T_NEW, LANES), f32)
        for g in pats:
            vt = caches[g][0, QW + hp * LANES: QW + (hp + 1) * LANES, :].astype(bf16)
            v_new = new_sc[:, pair_cols(g, hp, 3 * QW)].astype(bf16)
            p_c, p_n = probs[hp, g]
            acc = acc + _dot_nt(p_c, vt) + _dot(p_n, v_new)
        out = acc / probs[hp]
        o_parts.append(jnp.where(low[:T_NEW], out[:T_NEW], out[T_NEW:]))
    o_ref[0] = jnp.concatenate(o_parts, axis=1)


TILE = 256


def _expand_heads(w):
    r = lax.broadcasted_iota(jnp.int32, (2 * LANES, QW), 0) % LANES
    head = lax.broadcasted_iota(jnp.int32, (2 * LANES, QW), 1) // HEAD_DIM
    expand = (r == head // 2 + HEAD_DIM * (1 - head % 2)).astype(bf16)
    hi = w.astype(bf16)
    lo = (w - hi.astype(f32)).astype(bf16)
    return _dot(jnp.concatenate([hi, lo], axis=1), expand)


def _rest_kernel(*refs, sample, sample_every=None):
    if sample:
        (x_ref, scale_ref, shift_ref, gate_ref, gpre_ref, gpost_ref, lng_ref, lnb_ref,
         wa_ref, wzb_ref, wg_ref, wsp_ref, bsp_ref, wpa_ref, wpb_ref, wout_ref,
         attn_ref, y_ref, vn_ref) = refs
        x = x_ref[...]
        scale, shift, gate = scale_ref[...], shift_ref[...], gate_ref[...]
    else:
        (x_ref, scale_ref, shift_ref, gate_ref, gpre_ref, gpost_ref, lng_ref, lnb_ref,
         wa_ref, wzb_ref, wg_ref, wsp_ref, bsp_ref, wpa_ref, wpb_ref, wout_ref,
         o1_ref, ml1_ref, o4_ref, ml4_ref, o16_ref, ml16_ref,
         qkv_s_ref, c0_ref, c1_ref, c2_ref, y_ref, attn_s_ref,
         o4_sc, m4_sc, l4_sc, o16_sc, m16_sc, l16_sc, new_sc) = refs
        x = x_ref[0]
        scale, shift, gate = scale_ref[0], shift_ref[0], gate_ref[0]

        @pl.when(jnp.logical_and(pl.program_id(0) == 0, pl.program_id(1) == 0))
        def _():
            new_sc[...] = jnp.zeros_like(new_sc)

        @pl.when(pl.program_id(1) % sample_every == 0)
        def _():
            _sample_attention(qkv_s_ref, c0_ref, c1_ref, c2_ref, attn_s_ref, new_sc)

    rows = x.shape[0]

    if sample:
        attn = attn_ref[...]
    else:
        for osc, msc, lsc, oref, mlref, dil in (
                (o4_sc, m4_sc, l4_sc, o4_ref, ml4_ref, DILATIONS[1]),
                (o16_sc, m16_sc, l16_sc, o16_ref, ml16_ref, DILATIONS[2])):
            n = rows // dil
            for r in range(dil):
                o_r = oref[0, r].astype(f32)
                for k in range(QW // LANES):
                    osc[k, pl.ds(r, n, stride=dil), :] = o_r[:, k * LANES:(k + 1) * LANES]
                msc[pl.ds(r, n, stride=dil), :] = mlref[0, r, :, :LANES]
                lsc[pl.ds(r, n, stride=dil), :] = mlref[0, r, :, LANES:]
        ms = (ml1_ref[0, :, :LANES], m4_sc[...], m16_sc[...])
        ls = (ml1_ref[0, :, LANES:], l4_sc[...], l16_sc[...])
        unchunk = lambda sc: jnp.concatenate([sc[k] for k in range(QW // LANES)], axis=1)
        outs = (o1_ref[0].astype(f32), unchunk(o4_sc), unchunk(o16_sc))
        m_all = jnp.maximum(jnp.maximum(ms[0], ms[1]), ms[2])
        ws = [jnp.exp(m - m_all) for m in ms]
        den = ws[0] * ls[0] + ws[1] * ls[1] + ws[2] * ls[2]
        attn = sum(_expand_heads(w / den) * o for w, o in zip(ws, outs))

    h = _norm_mod(x, gpre_ref[...], scale, shift).astype(bf16)

    pa = _dot(h, wa_ref[...])
    u_a, v_a, z_a = pa[:, :D_MODEL], pa[:, D_MODEL:2 * D_MODEL], pa[:, 2 * D_MODEL:]
    mu = jnp.mean(v_a, axis=-1, keepdims=True)
    cen = v_a - mu
    var = jnp.mean(cen * cen, axis=-1, keepdims=True)
    v_n = cen * lax.rsqrt(var + EPS) * lng_ref[...] + lnb_ref[...]
    v_nb = v_n.astype(bf16)
    if sample:
        vn_ref[...] = v_n
        pick = (lax.broadcasted_iota(jnp.int32, (rows, CHUNK), 1)
                == lax.broadcasted_iota(jnp.int32, (rows, CHUNK), 0) % T_NEW).astype(bf16)
        same_seq = (lax.broadcasted_iota(jnp.int32, (rows, rows), 0) // T_NEW
                    == lax.broadcasted_iota(jnp.int32, (rows, rows), 1) // T_NEW)
        cols = []
        for g in range(A_GROUPS):
            tiled = _dot_nt(_dot(pick, wsp_ref[g]).astype(bf16), pick)
            w_blk = jnp.where(same_seq, tiled, 0.0).astype(bf16)
            cols.append(_dot(w_blk, v_nb[:, g * LANES:(g + 1) * LANES]) + bsp_ref[:, g:g + 1])
        zs = jnp.concatenate(cols, axis=1)
    else:
        n_ck = rows // CHUNK
        per_group = []
        for g in range(A_GROUPS):
            rhs = jnp.concatenate(
                [v_nb[ck * CHUNK:(ck + 1) * CHUNK, g * LANES:(g + 1) * LANES] for ck in range(n_ck)], axis=1)
            per_group.append(_dot(wsp_ref[g], rhs) + bsp_ref[:, g:g + 1])
        zs = jnp.concatenate(
            [jnp.concatenate([pg[:, ck * LANES:(ck + 1) * LANES] for pg in per_group], axis=1)
             for ck in range(n_ck)], axis=0)
    y_a = u_a * zs * _silu(z_a)

    z_b = _dot(h, wzb_ref[...])
    y_b = (attn * _silu(z_b)).astype(bf16)
    gl = _dot(h, wg_ref[...])
    p_b = _dot(y_b, wpb_ref[...])
    p_a = _dot(y_a.astype(bf16), wpa_ref[...])
    merged = jax.nn.sigmoid(gl[:, :D_MODEL]) * p_a + jax.nn.sigmoid(gl[:, D_MODEL:]) * p_b
    out = _dot(merged.astype(bf16), wout_ref[...])
    normed = out * lax.rsqrt(jnp.mean(out * out, axis=-1, keepdims=True) + EPS) * gpost_ref[...]
    y = x + gate * normed
    if sample:
        y_ref[...] = y
    else:
        y_ref[0] = y


def _full(shape):
    nd = len(shape)
    return pl.BlockSpec(shape, lambda *_: (0,) * nd)


def _weight_specs(weights):
    in_proj = [pl.BlockSpec((D_MODEL, 3 * D_MODEL), lambda *_: (0, 0)),
               pl.BlockSpec((D_MODEL, QW), lambda *_: (0, ZB_COL0 // QW)),
               pl.BlockSpec((D_MODEL, 2 * D_MODEL), lambda *_: (0, GATE_COL0 // (2 * D_MODEL)))]
    return in_proj + [_full(w.shape) for w in weights[3:]]


def _rest_prompt(x, mod, vecs, weights, attn_parts, qkv_s, caches):
    b = x.shape[0]
    n_pat = len(DILATIONS)
    tiles = SEQ // TILE
    n_seq = qkv_s.shape[0]
    sample_every, rem = divmod(b * tiles, n_seq)
    assert rem == 0 and tiles % sample_every == 0, "sample sequences must tile the grid evenly"
    seq_of = lambda i, c: (i * (tiles // sample_every) + c // sample_every, 0, 0)
    tile3 = lambda i, c: (i, c, 0)
    part_args, part_specs, scratch = [], [], []
    for pattern, dil in enumerate(DILATIONS):
        for a in attn_parts:
            lanes = a.shape[-1]
            if dil == 1:
                part_args.append(a)
                part_specs.append(pl.BlockSpec((None, 1, TILE, lanes),
                                               lambda i, c, p=pattern: (p, i, c, 0)))
            else:
                part_args.append(a.reshape(n_pat, b, dil, SEQ // dil, lanes))
                part_specs.append(pl.BlockSpec((None, 1, dil, TILE // dil, lanes),
                                               lambda i, c, p=pattern: (p, i, 0, c, 0)))
        if dil != 1:
            scratch += [pltpu.VMEM((QW // LANES, TILE, LANES), f32),
                        pltpu.VMEM((TILE, LANES), f32), pltpu.VMEM((TILE, LANES), f32)]
    in_specs = (
        [pl.BlockSpec((1, TILE, D_MODEL), tile3)]
        + [pl.BlockSpec((1, 1, D_MODEL), lambda i, c, col=col: (i, 0, col))
           for col in (MOD_SCALE, MOD_SHIFT, MOD_GATE)]
        + [_full(v.shape) for v in vecs] + _weight_specs(weights) + part_specs
        + [pl.BlockSpec((1, T_NEW, 9 * QW), seq_of)]
        + [pl.BlockSpec((1, 2 * QW, win), seq_of) for win in WINDOWS])
    scratch.append(pltpu.VMEM((LANES, 6 * QW), f32))
    return pl.pallas_call(
        functools.partial(_rest_kernel, sample=False, sample_every=sample_every),
        grid=(b, tiles),
        in_specs=in_specs,
        out_specs=[pl.BlockSpec((1, TILE, D_MODEL), tile3), pl.BlockSpec((1, T_NEW, QW), seq_of)],
        out_shape=[jax.ShapeDtypeStruct((b, SEQ, D_MODEL), f32),
                   jax.ShapeDtypeStruct((n_seq, T_NEW, QW), f32)],
        scratch_shapes=scratch,
        name="rest_p",
        compiler_params=pltpu.CompilerParams(
            dimension_semantics=("arbitrary", "arbitrary"), vmem_limit_bytes=VMEM_LIMIT),
    )(x, mod, mod, mod, *vecs, *weights, *part_args, qkv_s, *caches)


def _rest_sample(x, scale, shift, gate, vecs, weights, attn):
    n = x.shape[0]
    args = (x, scale, shift, gate, *vecs, *weights, attn)
    in_specs = ([_full(a.shape) for a in (x, scale, shift, gate, *vecs)] + _weight_specs(weights)
                + [_full(attn.shape)])
    return pl.pallas_call(
        functools.partial(_rest_kernel, sample=True),
        in_specs=in_specs,
        out_specs=[_full((n, D_MODEL)), _full((n, D_MODEL))],
        out_shape=[jax.ShapeDtypeStruct((n, D_MODEL), f32), jax.ShapeDtypeStruct((n, D_MODEL), f32)],
        grid=(1,),
        name="rest_s",
        compiler_params=pltpu.CompilerParams(
            dimension_semantics=("arbitrary",), vmem_limit_bytes=VMEM_LIMIT),
    )(*args)


def kernel(x_prompt, x_sample, cache_kv_w128, cache_kv_w512, cache_kv_w2048, c_prompt, c_sample, w_cond, b_cond, g_pre, w_in, ln_v_g, ln_v_b, w_spatial, b_spatial, w_proj_a, w_proj_b, w_out, g_post):
    assert w_in.shape[0] == 1, "single layer"
    bp, seq, _ = x_prompt.shape
    bs, t_new, _ = x_sample.shape
    assert seq == SEQ and t_new == T_NEW

    w_in_b = w_in[0].astype(bf16)
    w_kvt = jnp.transpose(w_in[0][:, K_COL0:ZB_COL0].reshape(D_MODEL, 2, 3, QW), (2, 1, 3, 0))
    w_kvt = w_kvt.reshape(3, 2 * QW, D_MODEL).astype(bf16)
    causal = jnp.tril(jnp.ones((CHUNK, CHUNK), bool))
    w_sp = jnp.where(causal[None], w_spatial[0], 0.0)
    w_sp_p = w_sp.astype(bf16)
    n_s = bs * T_NEW
    b_sp_p = b_spatial[0].T
    b_sp_s = jnp.tile(b_spatial[0][:, :T_NEW].T, (bs, 1))
    weights_tail = (w_proj_a[0].astype(bf16), w_proj_b[0].astype(bf16), w_out[0].astype(bf16))
    vecs = (g_pre, g_post, ln_v_g, ln_v_b)

    mod = _cond(jnp.concatenate([c_prompt, c_sample], axis=0), w_cond[0], b_cond)
    shift, scale, gate = mod[:, :D_MODEL], mod[:, D_MODEL:2 * D_MODEL], mod[:, 2 * D_MODEL:]
    mod_p = mod.reshape(bp + bs, 1, 3 * D_MODEL)
    ms = lambda a: jnp.repeat(a[bp:], T_NEW, axis=0)

    xs = x_sample.reshape(n_s, D_MODEL)
    qkv_s, *kv_s = _qkv_sample(xs, ms(scale), ms(shift), g_pre, w_in_b)
    caches_t = []
    for cache, win in zip((cache_kv_w128, cache_kv_w512, cache_kv_w2048), WINDOWS):
        caches_t.append(jnp.transpose(cache[0], (0, 2, 3, 4, 1)).reshape(bs, 2 * QW, win))

    o_all, ml_all, *kvt = _attn_prompt(x_prompt, mod_p, g_pre, w_in_b, w_kvt)
    in_proj = (w_in_b, w_in_b, w_in_b)
    y_p, attn_s = _rest_prompt(x_prompt, mod_p, vecs,
                               in_proj + (w_sp_p, b_sp_p) + weights_tail, (o_all, ml_all),
                               qkv_s.reshape(bs, T_NEW, 9 * QW), caches_t)
    kv_p = [jnp.transpose(a.reshape(bp, 2, N_HEADS, HEAD_DIM, win), (0, 4, 1, 2, 3))[None]
            for a, win in zip(kvt, WINDOWS)]

    attn_s = attn_s.reshape(n_s, QW)
    y_s, v_n_s = _rest_sample(xs, ms(scale), ms(shift), ms(gate), vecs,
                              in_proj + (w_sp_p, b_sp_s) + weights_tail, attn_s)
    kv_s = [a.reshape(1, bs, T_NEW, 2, N_HEADS, HEAD_DIM) for a in kv_s]

    return (y_p, y_s.reshape(bs, T_NEW, D_MODEL), kv_p[0], kv_p[1], kv_p[2],
            kv_s[0], kv_s[1], kv_s[2], v_n_s.reshape(1, bs, T_NEW, D_MODEL))
```

```python
import functools

import jax
import jax.numpy as jnp
from jax import lax
from jax.experimental import pallas as pl
from jax.experimental.pallas import tpu as pltpu

D_MODEL = 1024
SEQ = 2048
HEAD_DIM = 64
N_HEADS = 8
QW = N_HEADS * HEAD_DIM
WINDOWS = (128, 512, 2048)
DILATIONS = (1, 4, 16)
CHUNK = 128
A_GROUPS = 8
EPS = 1e-6
NEG = -1e30
LANES = 128
VMEM_LIMIT = 56 * 1024 * 1024
Q_COL0 = 3 * D_MODEL
K_COL0 = Q_COL0 + 3 * QW
V_COL0 = K_COL0 + 3 * QW
ZB_COL0 = V_COL0 + 3 * QW
GATE_COL0 = ZB_COL0 + QW
MOD_SHIFT, MOD_SCALE, MOD_GATE = 0, 1, 2

f32 = jnp.float32
bf16 = jnp.bfloat16


def _silu(x):
    return x * jax.nn.sigmoid(x)


def _norm_mod(x, g_pre, scale, shift):
    y = x * lax.rsqrt(jnp.mean(x * x, axis=-1, keepdims=True) + EPS) * g_pre
    return y * (1.0 + scale) + shift


def _dot(a, b):
    return jnp.dot(a, b, preferred_element_type=f32)


def _dot_nt(a, b):
    return lax.dot_general(a, b, (((1,), (1,)), ((), ())), preferred_element_type=f32)


def _cond_kernel(c_ref, w_ref, b_ref, o_ref):
    o_ref[...] = _dot(_silu(c_ref[...]).astype(bf16), w_ref[...].astype(bf16)) + b_ref[...]


def _cond(c_all, w_cond, b_cond):
    n = c_all.shape[0]
    return pl.pallas_call(
        _cond_kernel,
        out_shape=jax.ShapeDtypeStruct((n, 3 * D_MODEL), f32),
        name="cond",
        compiler_params=pltpu.CompilerParams(vmem_limit_bytes=VMEM_LIMIT),
    )(c_all, w_cond, b_cond)


ROWS_PER_STEP = 512
Q_BLOCK = 128
NORM_ROWS = 256


def _attn_step(c, last_step, wq_ref, wk_ref, wv_ref, wkvt_ref, o_ref, m_ref, l_ref, kvt_ref,
               h_sc, k_sc, ve_sc, vo_sc, *, dil, window):
    seg = SEQ // dil
    n_lc = D_MODEL // LANES

    def rows_of(start, n):
        idx = pl.ds(pl.multiple_of(start, n), n) if dil == 1 else pl.ds(start, n, stride=dil)
        return jnp.concatenate([h_sc[k, idx, :] for k in range(n_lc)], axis=1)

    if seg >= ROWS_PER_STEP:
        per = seg // ROWS_PER_STEP
        h = rows_of((c // per) + (c % per) * ROWS_PER_STEP * dil, ROWS_PER_STEP)
    else:
        per = ROWS_PER_STEP // seg
        h = jnp.concatenate([rows_of(per * c + i, seg) for i in range(per)], axis=0)
    h = h.astype(bf16)
    base = pl.multiple_of(c * ROWS_PER_STEP, ROWS_PER_STEP)

    def tokens_t(start, n):
        return jnp.concatenate([h_sc[k, pl.ds(start, n), :] for k in range(n_lc)], axis=1).astype(bf16)

    if window == SEQ:
        kvt_ref[0] = _dot_nt(wkvt_ref[0], tokens_t(base, ROWS_PER_STEP))
    else:
        @pl.when(c == last_step)
        def _():
            kvt_ref[0] = _dot_nt(wkvt_ref[0], tokens_t(SEQ - window, window))

    low_w = (lax.broadcasted_iota(jnp.int32, (ROWS_PER_STEP, QW), 1) % LANES) < HEAD_DIM
    new_rows = pl.ds(Q_BLOCK + base, ROWS_PER_STEP)
    k_sc[new_rows, :] = _dot(h, wk_ref[...]).astype(bf16)

    def project_v():
        v = _dot(h, wv_ref[...])
        ve_sc[new_rows, :] = jnp.where(low_w, v, 1.0).astype(bf16)
        vo_sc[new_rows, :] = jnp.where(low_w, 1.0, v).astype(bf16)

    low_q = (lax.broadcasted_iota(jnp.int32, (Q_BLOCK, QW), 1) % LANES) < HEAD_DIM

    def project_q(j):
        q = _dot(h[j * Q_BLOCK:(j + 1) * Q_BLOCK], wq_ref[...]) * (HEAD_DIM ** -0.5)
        return jnp.where(low_q, q, 0.0).astype(bf16), jnp.where(low_q, 0.0, q).astype(bf16)

    n_blocks = ROWS_PER_STEP // Q_BLOCK
    pairs = N_HEADS // 2
    qi = lax.broadcasted_iota(jnp.int32, (2 * Q_BLOCK, 2 * Q_BLOCK), 0) % Q_BLOCK
    kj = lax.broadcasted_iota(jnp.int32, (2 * Q_BLOCK, 2 * Q_BLOCK), 1)
    lane = lax.broadcasted_iota(jnp.int32, (Q_BLOCK, LANES), 1)
    low = lane < HEAD_DIM

    def has_prev(j):
        return seg > ROWS_PER_STEP or (j * Q_BLOCK) % seg != 0

    def key_rows(j):
        r0 = pl.multiple_of(base + j * Q_BLOCK, Q_BLOCK)
        return pl.ds(r0, 2 * Q_BLOCK) if has_prev(j) else pl.ds(r0 + Q_BLOCK, Q_BLOCK)

    def mask_of(j):
        if not has_prev(j):
            own = (2 * Q_BLOCK, Q_BLOCK)
            return (lax.broadcasted_iota(jnp.int32, own, 1)
                    <= lax.broadcasted_iota(jnp.int32, own, 0) % Q_BLOCK)
        off = jnp.where((base % seg) != 0, 0, Q_BLOCK) if (seg > ROWS_PER_STEP and j == 0) else 0
        in_prev = jnp.logical_and(kj < Q_BLOCK, kj >= qi + off)
        in_cur = jnp.logical_and(kj >= Q_BLOCK, kj - Q_BLOCK <= qi)
        return jnp.logical_or(in_prev, in_cur)

    def qk(j, q_even, q_odd):
        keys = key_rows(j)
        out = []
        for hp in range(pairs):
            cols = slice(hp * LANES, (hp + 1) * LANES)
            q_st = jnp.concatenate([q_even[:, cols], q_odd[:, cols]], axis=0)
            out.append(_dot_nt(q_st, k_sc[keys, cols]))
        return out

    def finish(j, scores):
        rows = slice(j * Q_BLOCK, (j + 1) * Q_BLOCK)
        keys = key_rows(j)
        mask = mask_of(j)
        ps, ms = [], []
        for hp in range(pairs):
            s = jnp.where(mask, scores[hp], NEG)
            for half in (s[:Q_BLOCK], s[Q_BLOCK:]):
                m = jnp.max(half, axis=-1, keepdims=True)
                ps.append(jnp.exp(half - m).astype(bf16))
                ms.append(m)
        o_parts = []
        m_tile = jnp.zeros((Q_BLOCK, LANES), f32)
        l_tile = jnp.ones((Q_BLOCK, LANES), f32)
        for hp in range(pairs):
            cols = slice(hp * LANES, (hp + 1) * LANES)
            acc_e = _dot(ps[2 * hp], ve_sc[keys, cols])
            acc_o = _dot(ps[2 * hp + 1], vo_sc[keys, cols])
            o_parts.append(jnp.where(low, acc_e, acc_o))
            l_tile = jnp.where(lane == HEAD_DIM + hp, acc_e, jnp.where(lane == hp, acc_o, l_tile))
            m_tile = jnp.where(lane == HEAD_DIM + hp, ms[2 * hp],
                               jnp.where(lane == hp, ms[2 * hp + 1], m_tile))
        o_ref[0, rows, :] = jnp.concatenate(o_parts, axis=1).astype(bf16)
        m_ref[0, rows, :] = m_tile
        l_ref[0, rows, :] = l_tile

    qs = {j: project_q(j) for j in range(min(2, n_blocks))}
    scores = qk(0, *qs[0])
    project_v()
    for j in range(n_blocks):
        nxt = qk(j + 1, *qs[j + 1]) if j + 1 < n_blocks else None
        if j + 2 < n_blocks:
            qs[j + 2] = project_q(j + 2)
        finish(j, scores)
        scores = nxt


def _attn_kernel(x_ref, scale_ref, shift_ref, gpre_ref, wq_ref, wk_ref, wv_ref, wkvt_ref,
                 o_ref, m_ref, l_ref, kvt0_ref, kvt1_ref, kvt2_ref, h_sc, k_sc, ve_sc, vo_sc):
    g = pl.program_id(1)
    c = pl.program_id(2)
    last_step = pl.num_programs(2) - 1

    kvt_refs = (kvt0_ref, kvt1_ref, kvt2_ref)

    def step(pattern, step_index):
        _attn_step(step_index, last_step, wq_ref, wk_ref, wv_ref, wkvt_ref, o_ref, m_ref, l_ref,
                   kvt_refs[pattern], h_sc, k_sc, ve_sc, vo_sc,
                   dil=DILATIONS[pattern], window=WINDOWS[pattern])

    @pl.when(jnp.logical_and(g == 0, c == 0))
    def _():
        zeros = jnp.zeros((Q_BLOCK, QW), bf16)
        k_sc[0:Q_BLOCK, :] = zeros
        ve_sc[0:Q_BLOCK, :] = zeros
        vo_sc[0:Q_BLOCK, :] = zeros
        for rb in range(SEQ // NORM_ROWS):
            rs = slice(rb * NORM_ROWS, (rb + 1) * NORM_ROWS)
            hb = _norm_mod(x_ref[0, rs, :], gpre_ref[...], scale_ref[0], shift_ref[0])
            for k in range(D_MODEL // LANES):
                h_sc[k, rs, :] = hb[:, k * LANES:(k + 1) * LANES]
        step(0, jnp.int32(0))

    for pattern in range(len(DILATIONS)):
        @pl.when(jnp.logical_and(g == pattern, jnp.logical_or(pattern > 0, c > 0)))
        def _(pattern=pattern):
            step(pattern, c)


def _attn_prompt(x, mod, g_pre, w_in_b, w_kvt):
    b = x.shape[0]
    n_pat = len(DILATIONS)
    steps = SEQ // ROWS_PER_STEP
    full_window = WINDOWS.index(SEQ)
    per_b = lambda i, g, c: (i, 0, 0)
    step_rows = lambda i, g, c: (g, i, c, 0)

    def kvt_spec(pattern):
        if WINDOWS[pattern] == SEQ:
            return pl.BlockSpec((1, 2 * QW, ROWS_PER_STEP),
                                lambda i, g, c: (i, 0, jnp.where(g == full_window, c, 0)))
        return pl.BlockSpec((1, 2 * QW, WINDOWS[pattern]), per_b)

    return pl.pallas_call(
        _attn_kernel,
        grid=(b, n_pat, steps),
        in_specs=[
            pl.BlockSpec((1, SEQ, D_MODEL), per_b),
            pl.BlockSpec((1, 1, D_MODEL), lambda i, g, c: (i, 0, MOD_SCALE)),
            pl.BlockSpec((1, 1, D_MODEL), lambda i, g, c: (i, 0, MOD_SHIFT)),
            pl.BlockSpec((1, D_MODEL), lambda i, g, c: (0, 0)),
            pl.BlockSpec((D_MODEL, QW), lambda i, g, c: (0, Q_COL0 // QW + g)),
            pl.BlockSpec((D_MODEL, QW), lambda i, g, c: (0, K_COL0 // QW + g)),
            pl.BlockSpec((D_MODEL, QW), lambda i, g, c: (0, V_COL0 // QW + g)),
            pl.BlockSpec((1, 2 * QW, D_MODEL), lambda i, g, c: (g, 0, 0)),
        ],
        out_specs=[
            pl.BlockSpec((None, 1, ROWS_PER_STEP, QW), step_rows),
            pl.BlockSpec((None, 1, ROWS_PER_STEP, LANES), step_rows),
            pl.BlockSpec((None, 1, ROWS_PER_STEP, LANES), step_rows),
        ] + [kvt_spec(p) for p in range(n_pat)],
        out_shape=[
            jax.ShapeDtypeStruct((n_pat, b, SEQ, QW), bf16),
            jax.ShapeDtypeStruct((n_pat, b, SEQ, LANES), f32),
            jax.ShapeDtypeStruct((n_pat, b, SEQ, LANES), f32),
        ] + [jax.ShapeDtypeStruct((b, 2 * QW, win), f32) for win in WINDOWS],
        scratch_shapes=[pltpu.VMEM((D_MODEL // LANES, SEQ, LANES), f32),
                        pltpu.VMEM((SEQ + Q_BLOCK, QW), bf16), pltpu.VMEM((SEQ + Q_BLOCK, QW), bf16),
                        pltpu.VMEM((SEQ + Q_BLOCK, QW), bf16)],
        name="attn_p",
        compiler_params=pltpu.CompilerParams(
            dimension_semantics=("arbitrary", "arbitrary", "arbitrary"), vmem_limit_bytes=VMEM_LIMIT),
    )(x, mod, mod, g_pre, w_in_b, w_in_b, w_in_b, w_kvt)


def _qkv_s_kernel(x_ref, scale_ref, shift_ref, gpre_ref, w_ref, o_ref, *kv_refs):
    h = _norm_mod(x_ref[...], gpre_ref[...], scale_ref[...], shift_ref[...]).astype(bf16)
    tile = _dot(h, w_ref[...])
    o_ref[...] = tile
    for half in range(2):
        @pl.when(pl.program_id(0) == 1 + half)
        def _(half=half):
            for g, kv_ref in enumerate(kv_refs):
                kv_ref[:, half * QW:(half + 1) * QW] = tile[:, g * QW:(g + 1) * QW]


def _qkv_sample(x, scale, shift, g_pre, w_in_b):
    n = x.shape[0]
    n_pat = len(DILATIONS)
    cols = 3 * n_pat * QW
    tile = n_pat * QW
    first = Q_COL0 // tile
    return pl.pallas_call(
        _qkv_s_kernel,
        grid=(cols // tile,),
        in_specs=[
            pl.BlockSpec((n, D_MODEL), lambda j: (0, 0)),
            pl.BlockSpec((n, D_MODEL), lambda j: (0, 0)),
            pl.BlockSpec((n, D_MODEL), lambda j: (0, 0)),
            pl.BlockSpec((1, D_MODEL), lambda j: (0, 0)),
            pl.BlockSpec((D_MODEL, tile), lambda j: (0, first + j)),
        ],
        out_specs=[pl.BlockSpec((n, tile), lambda j: (0, j))]
        + [pl.BlockSpec((n, 2 * QW), lambda j: (0, 0))] * n_pat,
        out_shape=[jax.ShapeDtypeStruct((n, cols), f32)]
        + [jax.ShapeDtypeStruct((n, 2 * QW), f32)] * n_pat,
        name="qkv_s",
        compiler_params=pltpu.CompilerParams(
            dimension_semantics=("arbitrary",), vmem_limit_bytes=VMEM_LIMIT),
    )(x, scale, shift, g_pre, w_in_b)


T_NEW = 8


def _sample_attention(qkv_ref, c0_ref, c1_ref, c2_ref, o_ref, new_sc):
    qkv = qkv_ref[0]
    new_sc[0:T_NEW, :] = qkv[:, 3 * QW:]
    q_all = qkv[:, :3 * QW] * (HEAD_DIM ** -0.5)

    lane8 = lax.broadcasted_iota(jnp.int32, (2 * T_NEW, LANES), 1)
    low = lane8 < HEAD_DIM
    t_idx = lax.broadcasted_iota(jnp.int32, (2 * T_NEW, LANES), 0) % T_NEW
    caches = (c0_ref, c1_ref, c2_ref)
    pairs = range(N_HEADS // 2)
    pats = range(len(DILATIONS))

    def pair_cols(g, hp, base=0):
        return slice(base + g * QW + hp * LANES, base + g * QW + (hp + 1) * LANES)

    s_cache, s_new = {}, {}
    for hp in pairs:
        for g in pats:
            q2 = q_all[:, pair_cols(g, hp)]
            q_st = jnp.concatenate([jnp.where(low[:T_NEW], q2, 0.0), jnp.where(low[:T_NEW], 0.0, q2)],
                                   axis=0).astype(bf16)
            kt = caches[g][0, hp * LANES:(hp + 1) * LANES, :].astype(bf16)
            k_new = new_sc[:, pair_cols(g, hp)].astype(bf16)
            s_cache[hp, g] = _dot(q_st, kt)
            s_new[hp, g] = _dot_nt(q_st, k_new)

    probs = {}
    for hp in pairs:
        m = jnp.full((2 * T_NEW, 1), NEG, f32)
        for g, dil in enumerate(DILATIONS):
            rows = WINDOWS[g]
            rho = lax.broadcasted_iota(jnp.int32, (2 * T_NEW, rows), 1)
            tq = lax.broadcasted_iota(jnp.int32, (2 * T_NEW, rows), 0) % T_NEW
            ok_c = jnp.logical_and(rho >= tq, ((rho - tq) & (dil - 1)) == 0)
            ok_n = jnp.logical_and(lane8 <= t_idx, ((t_idx - lane8) & (dil - 1)) == 0)
            s_cache[hp, g] = jnp.where(ok_c, s_cache[hp, g], NEG)
            s_new[hp, g] = jnp.where(ok_n, s_new[hp, g], NEG)
            m = jnp.maximum(m, jnp.maximum(jnp.max(s_cache[hp, g], axis=-1, keepdims=True),
                                           jnp.max(s_new[hp, g], axis=-1, keepdims=True)))
        l = jnp.zeros((2 * T_NEW, 1), f32)
        for g in pats:
            p_c = jnp.exp(s_cache[hp, g] - m)
            p_n = jnp.exp(s_new[hp, g] - m)
            l = l + jnp.sum(p_c, axis=-1, keepdims=True) + jnp.sum(p_n, axis=-1, keepdims=True)
            probs[hp, g] = (p_c.astype(bf16), p_n.astype(bf16))
        probs[hp] = l

    o_parts = []
    for hp in pairs:
        acc = jnp.zeros((2 * T_NEW, LANES), f32)
        for g in pats:
            vt = caches[g][0, QW + hp * LANES: QW + (hp + 1) * LANES, :].astype(bf16)
            v_new = new_sc[:, pair_cols(g, hp, 3 * QW)].astype(bf16)
            p_c, p_n = probs[hp, g]
            acc = acc + _dot_nt(p_c, vt) + _dot(p_n, v_new)
        out = acc / probs[hp]
        o_parts.append(jnp.where(low[:T_NEW], out[:T_NEW], out[T_NEW:]))
    o_ref[0] = jnp.concatenate(o_parts, axis=1)


TILE = 512


def _expand_heads(w):
    r = lax.broadcasted_iota(jnp.int32, (2 * LANES, QW), 0) % LANES
    head = lax.broadcasted_iota(jnp.int32, (2 * LANES, QW), 1) // HEAD_DIM
    expand = (r == head // 2 + HEAD_DIM * (1 - head % 2)).astype(bf16)
    hi = w.astype(bf16)
    lo = (w - hi.astype(f32)).astype(bf16)
    return _dot(jnp.concatenate([hi, lo], axis=1), expand)


def _rest_kernel(*refs, sample, sample_every=None, n_seq=None):
    if sample:
        (x_ref, scale_ref, shift_ref, gate_ref, gpre_ref, gpost_ref, lng_ref, lnb_ref,
         wa_ref, wzb_ref, wg_ref, wsp_ref, bsp_ref, wpa_ref, wpb_ref, wout_ref,
         attn_ref, y_ref, vn_ref) = refs
        x = x_ref[...]
        scale, shift, gate = scale_ref[...], shift_ref[...], gate_ref[...]
    else:
        (x_ref, scale_ref, shift_ref, gate_ref, gpre_ref, gpost_ref, lng_ref, lnb_ref,
         wa_ref, wzb_ref, wg_ref, wsp_ref, bsp_ref, wpa_ref, wpb_ref, wout_ref,
         o1_ref, m1_ref, l1_ref, o4_ref, m4_ref, l4_ref, o16_ref, m16_ref, l16_ref,
         qkv_s_ref, c0_hbm, c1_hbm, c2_hbm, y_ref, attn_s_ref,
         o4_sc, m4_sc, l4_sc, o16_sc, m16_sc, l16_sc, new_sc, c0_buf, c1_buf, c2_buf, cache_sem) = refs
        x = x_ref[0]
        scale, shift, gate = scale_ref[0], shift_ref[0], gate_ref[0]

        step = pl.program_id(0) * pl.num_programs(1) + pl.program_id(1)
        seq = step // sample_every

        def cache_copies(s):
            return [pltpu.make_async_copy(hbm.at[pl.ds(s, 1)], buf, cache_sem.at[k])
                    for k, (hbm, buf) in enumerate(((c0_hbm, c0_buf), (c1_hbm, c1_buf), (c2_hbm, c2_buf)))]

        @pl.when(step == 0)
        def _():
            new_sc[...] = jnp.zeros_like(new_sc)
            for cp in cache_copies(0):
                cp.start()

        @pl.when(step % sample_every == 0)
        def _():
            for cp in cache_copies(seq):
                cp.wait()
            _sample_attention(qkv_s_ref, c0_buf, c1_buf, c2_buf, attn_s_ref, new_sc)

            @pl.when(seq + 1 < n_seq)
            def _():
                for cp in cache_copies(seq + 1):
                    cp.start()

    rows = x.shape[0]

    if sample:
        attn = attn_ref[...]
    else:
        for osc, msc, lsc, oref, mref, lref, dil in (
                (o4_sc, m4_sc, l4_sc, o4_ref, m4_ref, l4_ref, DILATIONS[1]),
                (o16_sc, m16_sc, l16_sc, o16_ref, m16_ref, l16_ref, DILATIONS[2])):
            n = rows // dil
            for r in range(dil):
                o_r = oref[0, r].astype(f32)
                for k in range(QW // LANES):
                    osc[k, pl.ds(r, n, stride=dil), :] = o_r[:, k * LANES:(k + 1) * LANES]
                msc[pl.ds(r, n, stride=dil), :] = mref[0, r]
                lsc[pl.ds(r, n, stride=dil), :] = lref[0, r]
        ms = (m1_ref[0], m4_sc[...], m16_sc[...])
        ls = (l1_ref[0], l4_sc[...], l16_sc[...])
        unchunk = lambda sc: jnp.concatenate([sc[k] for k in range(QW // LANES)], axis=1)
        outs = (o1_ref[0].astype(f32), unchunk(o4_sc), unchunk(o16_sc))
        m_all = jnp.maximum(jnp.maximum(ms[0], ms[1]), ms[2])
        ws = [jnp.exp(m - m_all) for m in ms]
        den = ws[0] * ls[0] + ws[1] * ls[1] + ws[2] * ls[2]
        attn = sum(_expand_heads(w / den) * o for w, o in zip(ws, outs))

    h = _norm_mod(x, gpre_ref[...], scale, shift).astype(bf16)

    pa = _dot(h, wa_ref[...])
    u_a, v_a, z_a = pa[:, :D_MODEL], pa[:, D_MODEL:2 * D_MODEL], pa[:, 2 * D_MODEL:]
    mu = jnp.mean(v_a, axis=-1, keepdims=True)
    cen = v_a - mu
    var = jnp.mean(cen * cen, axis=-1, keepdims=True)
    v_n = cen * lax.rsqrt(var + EPS) * lng_ref[...] + lnb_ref[...]
    v_nb = v_n.astype(bf16)
    if sample:
        vn_ref[...] = v_n
        pick = (lax.broadcasted_iota(jnp.int32, (rows, CHUNK), 1)
                == lax.broadcasted_iota(jnp.int32, (rows, CHUNK), 0) % T_NEW).astype(bf16)
        same_seq = (lax.broadcasted_iota(jnp.int32, (rows, rows), 0) // T_NEW
                    == lax.broadcasted_iota(jnp.int32, (rows, rows), 1) // T_NEW)
        cols = []
        for g in range(A_GROUPS):
            tiled = _dot_nt(_dot(pick, wsp_ref[g]).astype(bf16), pick)
            w_blk = jnp.where(same_seq, tiled, 0.0).astype(bf16)
            cols.append(_dot(w_blk, v_nb[:, g * LANES:(g + 1) * LANES]) + bsp_ref[:, g:g + 1])
        zs = jnp.concatenate(cols, axis=1)
    else:
        n_ck = rows // CHUNK
        per_group = []
        for g in range(A_GROUPS):
            rhs = jnp.concatenate(
                [v_nb[ck * CHUNK:(ck + 1) * CHUNK, g * LANES:(g + 1) * LANES] for ck in range(n_ck)], axis=1)
            per_group.append(_dot(wsp_ref[g], rhs) + bsp_ref[:, g:g + 1])
        zs = jnp.concatenate(
            [jnp.concatenate([pg[:, ck * LANES:(ck + 1) * LANES] for pg in per_group], axis=1)
             for ck in range(n_ck)], axis=0)
    y_a = u_a * zs * _silu(z_a)

    z_b = _dot(h, wzb_ref[...])
    y_b = (attn * _silu(z_b)).astype(bf16)
    gl = _dot(h, wg_ref[...])
    p_b = _dot(y_b, wpb_ref[...])
    p_a = _dot(y_a.astype(bf16), wpa_ref[...])
    merged = jax.nn.sigmoid(gl[:, :D_MODEL]) * p_a + jax.nn.sigmoid(gl[:, D_MODEL:]) * p_b
    out = _dot(merged.astype(bf16), wout_ref[...])
    normed = out * lax.rsqrt(jnp.mean(out * out, axis=-1, keepdims=True) + EPS) * gpost_ref[...]
    y = x + gate * normed
    if sample:
        y_ref[...] = y
    else:
        y_ref[0] = y


def _full(shape):
    nd = len(shape)
    return pl.BlockSpec(shape, lambda *_: (0,) * nd)


def _weight_specs(weights):
    in_proj = [pl.BlockSpec((D_MODEL, 3 * D_MODEL), lambda *_: (0, 0)),
               pl.BlockSpec((D_MODEL, QW), lambda *_: (0, ZB_COL0 // QW)),
               pl.BlockSpec((D_MODEL, 2 * D_MODEL), lambda *_: (0, GATE_COL0 // (2 * D_MODEL)))]
    return in_proj + [_full(w.shape) for w in weights[3:]]


def _rest_prompt(x, mod, vecs, weights, attn_parts, qkv_s, caches):
    b = x.shape[0]
    n_pat = len(DILATIONS)
    tiles = SEQ // TILE
    n_seq = qkv_s.shape[0]
    sample_every, rem = divmod(b * tiles, n_seq)
    assert rem == 0 and tiles % sample_every == 0, "sample sequences must tile the grid evenly"
    seq_of = lambda i, c: (i * (tiles // sample_every) + c // sample_every, 0, 0)
    tile3 = lambda i, c: (i, c, 0)
    part_args, part_specs, scratch = [], [], []
    for pattern, dil in enumerate(DILATIONS):
        for a in attn_parts:
            lanes = a.shape[-1]
            if dil == 1:
                part_args.append(a)
                part_specs.append(pl.BlockSpec((None, 1, TILE, lanes),
                                               lambda i, c, p=pattern: (p, i, c, 0)))
            else:
                part_args.append(a.reshape(n_pat, b, dil, SEQ // dil, lanes))
                part_specs.append(pl.BlockSpec((None, 1, dil, TILE // dil, lanes),
                                               lambda i, c, p=pattern: (p, i, 0, c, 0)))
        if dil != 1:
            scratch += [pltpu.VMEM((QW // LANES, TILE, LANES), f32),
                        pltpu.VMEM((TILE, LANES), f32), pltpu.VMEM((TILE, LANES), f32)]
    in_specs = (
        [pl.BlockSpec((1, TILE, D_MODEL), tile3)]
        + [pl.BlockSpec((1, 1, D_MODEL), lambda i, c, col=col: (i, 0, col))
           for col in (MOD_SCALE, MOD_SHIFT, MOD_GATE)]
        + [_full(v.shape) for v in vecs] + _weight_specs(weights) + part_specs
        + [pl.BlockSpec((1, T_NEW, 9 * QW), seq_of)]
        + [pl.BlockSpec(memory_space=pl.ANY) for _ in WINDOWS])
    scratch.append(pltpu.VMEM((LANES, 6 * QW), f32))
    scratch += [pltpu.VMEM((1, 2 * QW, win), f32) for win in WINDOWS]
    scratch.append(pltpu.SemaphoreType.DMA((len(WINDOWS),)))
    return pl.pallas_call(
        functools.partial(_rest_kernel, sample=False, sample_every=sample_every, n_seq=n_seq),
        grid=(b, tiles),
        in_specs=in_specs,
        out_specs=[pl.BlockSpec((1, TILE, D_MODEL), tile3), pl.BlockSpec((1, T_NEW, QW), seq_of)],
        out_shape=[jax.ShapeDtypeStruct((b, SEQ, D_MODEL), f32),
                   jax.ShapeDtypeStruct((n_seq, T_NEW, QW), f32)],
        scratch_shapes=scratch,
        name="rest_p",
        compiler_params=pltpu.CompilerParams(
            dimension_semantics=("arbitrary", "arbitrary"), vmem_limit_bytes=VMEM_LIMIT),
    )(x, mod, mod, mod, *vecs, *weights, *part_args, qkv_s, *caches)


def _rest_sample(x, scale, shift, gate, vecs, weights, attn):
    n = x.shape[0]
    args = (x, scale, shift, gate, *vecs, *weights, attn)
    in_specs = ([_full(a.shape) for a in (x, scale, shift, gate, *vecs)] + _weight_specs(weights)
                + [_full(attn.shape)])
    return pl.pallas_call(
        functools.partial(_rest_kernel, sample=True),
        in_specs=in_specs,
        out_specs=[_full((n, D_MODEL)), _full((n, D_MODEL))],
        out_shape=[jax.ShapeDtypeStruct((n, D_MODEL), f32), jax.ShapeDtypeStruct((n, D_MODEL), f32)],
        grid=(1,),
        name="rest_s",
        compiler_params=pltpu.CompilerParams(
            dimension_semantics=("arbitrary",), vmem_limit_bytes=VMEM_LIMIT),
    )(*args)


def kernel(x_prompt, x_sample, cache_kv_w128, cache_kv_w512, cache_kv_w2048, c_prompt, c_sample, w_cond, b_cond, g_pre, w_in, ln_v_g, ln_v_b, w_spatial, b_spatial, w_proj_a, w_proj_b, w_out, g_post):
    assert w_in.shape[0] == 1, "single layer"
    bp, seq, _ = x_prompt.shape
    bs, t_new, _ = x_sample.shape
    assert seq == SEQ and t_new == T_NEW

    w_in_b = w_in[0].astype(bf16)
    w_kvt = jnp.transpose(w_in[0][:, K_COL0:ZB_COL0].reshape(D_MODEL, 2, 3, QW), (2, 1, 3, 0))
    w_kvt = w_kvt.reshape(3, 2 * QW, D_MODEL).astype(bf16)
    causal = jnp.tril(jnp.ones((CHUNK, CHUNK), bool))
    w_sp = jnp.where(causal[None], w_spatial[0], 0.0)
    w_sp_p = w_sp.astype(bf16)
    n_s = bs * T_NEW
    b_sp_p = b_spatial[0].T
    b_sp_s = jnp.tile(b_spatial[0][:, :T_NEW].T, (bs, 1))
    weights_tail = (w_proj_a[0].astype(bf16), w_proj_b[0].astype(bf16), w_out[0].astype(bf16))
    vecs = (g_pre, g_post, ln_v_g, ln_v_b)

    mod = _cond(jnp.concatenate([c_prompt, c_sample], axis=0), w_cond[0], b_cond)
    shift, scale, gate = mod[:, :D_MODEL], mod[:, D_MODEL:2 * D_MODEL], mod[:, 2 * D_MODEL:]
    mod_p = mod.reshape(bp + bs, 1, 3 * D_MODEL)
    ms = lambda a: jnp.repeat(a[bp:], T_NEW, axis=0)

    xs = x_sample.reshape(n_s, D_MODEL)
    qkv_s, *kv_s = _qkv_sample(xs, ms(scale), ms(shift), g_pre, w_in_b)
    caches_t = []
    for cache, win in zip((cache_kv_w128, cache_kv_w512, cache_kv_w2048), WINDOWS):
        caches_t.append(jnp.transpose(cache[0], (0, 2, 3, 4, 1)).reshape(bs, 2 * QW, win))

    o_all, m_all, l_all, *kvt = _attn_prompt(x_prompt, mod_p, g_pre, w_in_b, w_kvt)
    in_proj = (w_in_b, w_in_b, w_in_b)
    y_p, attn_s = _rest_prompt(x_prompt, mod_p, vecs,
                               in_proj + (w_sp_p, b_sp_p) + weights_tail, (o_all, m_all, l_all),
                               qkv_s.reshape(bs, T_NEW, 9 * QW), caches_t)
    kv_p = [jnp.transpose(a.reshape(bp, 2, N_HEADS, HEAD_DIM, win), (0, 4, 1, 2, 3))[None]
            for a, win in zip(kvt, WINDOWS)]

    attn_s = attn_s.reshape(n_s, QW)
    y_s, v_n_s = _rest_sample(xs, ms(scale), ms(shift), ms(gate), vecs,
                              in_proj + (w_sp_p, b_sp_s) + weights_tail, attn_s)
    kv_s = [a.reshape(1, bs, T_NEW, 2, N_HEADS, HEAD_DIM) for a in kv_s]

    return (y_p, y_s.reshape(bs, T_NEW, D_MODEL), kv_p[0], kv_p[1], kv_p[2],
            kv_s[0], kv_s[1], kv_s[2], v_n_s.reshape(1, bs, T_NEW, D_MODEL))
```

```python
import functools

import jax
import jax.numpy as jnp
from jax import lax
from jax.experimental import pallas as pl
from jax.experimental.pallas import tpu as pltpu

D_MODEL = 1024
SEQ = 2048
HEAD_DIM = 64
N_HEADS = 8
QW = N_HEADS * HEAD_DIM
WINDOWS = (128, 512, 2048)
DILATIONS = (1, 4, 16)
CHUNK = 128
A_GROUPS = 8
EPS = 1e-6
NEG = -1e30
LANES = 128
VMEM_LIMIT = 56 * 1024 * 1024
Q_COL0 = 3 * D_MODEL
K_COL0 = Q_COL0 + 3 * QW
V_COL0 = K_COL0 + 3 * QW
ZB_COL0 = V_COL0 + 3 * QW
GATE_COL0 = ZB_COL0 + QW
MOD_SHIFT, MOD_SCALE, MOD_GATE = 0, 1, 2

f32 = jnp.float32
bf16 = jnp.bfloat16


def _silu(x):
    return x * jax.nn.sigmoid(x)


def _norm_mod(x, g_pre, scale, shift):
    y = x * lax.rsqrt(jnp.mean(x * x, axis=-1, keepdims=True) + EPS) * g_pre
    return y * (1.0 + scale) + shift


def _dot(a, b):
    return jnp.dot(a, b, preferred_element_type=f32)


def _dot_nt(a, b):
    return lax.dot_general(a, b, (((1,), (1,)), ((), ())), preferred_element_type=f32)


def _cond_kernel(c_ref, w_ref, b_ref, o_ref):
    o_ref[...] = _dot(_silu(c_ref[...]).astype(bf16), w_ref[...].astype(bf16)) + b_ref[...]


def _cond(c_all, w_cond, b_cond):
    n = c_all.shape[0]
    return pl.pallas_call(
        _cond_kernel,
        out_shape=jax.ShapeDtypeStruct((n, 3 * D_MODEL), f32),
        name="cond",
        compiler_params=pltpu.CompilerParams(vmem_limit_bytes=VMEM_LIMIT),
    )(c_all, w_cond, b_cond)


ROWS_PER_STEP = 512
Q_BLOCK = 128
NORM_ROWS = 256


def _attn_step(c, last_step, wq_ref, wk_ref, wv_ref, wkvt_ref, o_ref, m_ref, l_ref, kvt_ref,
               h_sc, k_sc, ve_sc, vo_sc, *, dil, window):
    seg = SEQ // dil
    n_lc = D_MODEL // LANES

    def rows_of(start, n):
        idx = pl.ds(pl.multiple_of(start, n), n) if dil == 1 else pl.ds(start, n, stride=dil)
        return jnp.concatenate([h_sc[k, idx, :] for k in range(n_lc)], axis=1)

    if seg >= ROWS_PER_STEP:
        per = seg // ROWS_PER_STEP
        h = rows_of((c // per) + (c % per) * ROWS_PER_STEP * dil, ROWS_PER_STEP)
    else:
        per = ROWS_PER_STEP // seg
        h = jnp.concatenate([rows_of(per * c + i, seg) for i in range(per)], axis=0)
    h = h.astype(bf16)
    base = pl.multiple_of(c * ROWS_PER_STEP, ROWS_PER_STEP)

    def tokens_t(start, n):
        return jnp.concatenate([h_sc[k, pl.ds(start, n), :] for k in range(n_lc)], axis=1).astype(bf16)

    if window == SEQ:
        kvt_ref[0] = _dot_nt(wkvt_ref[0], tokens_t(base, ROWS_PER_STEP))
    else:
        @pl.when(c == last_step)
        def _():
            kvt_ref[0] = _dot_nt(wkvt_ref[0], tokens_t(SEQ - window, window))

    low_w = (lax.broadcasted_iota(jnp.int32, (ROWS_PER_STEP, QW), 1) % LANES) < HEAD_DIM
    new_rows = pl.ds(Q_BLOCK + base, ROWS_PER_STEP)
    k_sc[new_rows, :] = _dot(h, wk_ref[...]).astype(bf16)

    def project_v():
        v = _dot(h, wv_ref[...])
        ve_sc[new_rows, :] = jnp.where(low_w, v, 1.0).astype(bf16)
        vo_sc[new_rows, :] = jnp.where(low_w, 1.0, v).astype(bf16)

    low_q = (lax.broadcasted_iota(jnp.int32, (Q_BLOCK, QW), 1) % LANES) < HEAD_DIM

    def project_q(j):
        q = _dot(h[j * Q_BLOCK:(j + 1) * Q_BLOCK], wq_ref[...]) * (HEAD_DIM ** -0.5)
        return jnp.where(low_q, q, 0.0).astype(bf16), jnp.where(low_q, 0.0, q).astype(bf16)

    n_blocks = ROWS_PER_STEP // Q_BLOCK
    pairs = N_HEADS // 2
    qi = lax.broadcasted_iota(jnp.int32, (2 * Q_BLOCK, 2 * Q_BLOCK), 0) % Q_BLOCK
    kj = lax.broadcasted_iota(jnp.int32, (2 * Q_BLOCK, 2 * Q_BLOCK), 1)
    lane = lax.broadcasted_iota(jnp.int32, (Q_BLOCK, LANES), 1)
    low = lane < HEAD_DIM

    def has_prev(j):
        return seg > ROWS_PER_STEP or (j * Q_BLOCK) % seg != 0

    def key_rows(j):
        r0 = pl.multiple_of(base + j * Q_BLOCK, Q_BLOCK)
        return pl.ds(r0, 2 * Q_BLOCK) if has_prev(j) else pl.ds(r0 + Q_BLOCK, Q_BLOCK)

    def mask_of(j):
        if not has_prev(j):
            own = (2 * Q_BLOCK, Q_BLOCK)
            return (lax.broadcasted_iota(jnp.int32, own, 1)
                    <= lax.broadcasted_iota(jnp.int32, own, 0) % Q_BLOCK)
        off = jnp.where((base % seg) != 0, 0, Q_BLOCK) if (seg > ROWS_PER_STEP and j == 0) else 0
        in_prev = jnp.logical_and(kj < Q_BLOCK, kj >= qi + off)
        in_cur = jnp.logical_and(kj >= Q_BLOCK, kj - Q_BLOCK <= qi)
        return jnp.logical_or(in_prev, in_cur)

    def qk(j, q_even, q_odd):
        keys = key_rows(j)
        out = []
        for hp in range(pairs):
            cols = slice(hp * LANES, (hp + 1) * LANES)
            q_st = jnp.concatenate([q_even[:, cols], q_odd[:, cols]], axis=0)
            out.append(_dot_nt(q_st, k_sc[keys, cols]))
        return out

    def finish(j, scores):
        rows = slice(j * Q_BLOCK, (j + 1) * Q_BLOCK)
        keys = key_rows(j)
        mask = mask_of(j)
        ps, ms = [], []
        for hp in range(pairs):
            s = jnp.where(mask, scores[hp], NEG)
            for half in (s[:Q_BLOCK], s[Q_BLOCK:]):
                m = jnp.max(half, axis=-1, keepdims=True)
                ps.append(jnp.exp(half - m).astype(bf16))
                ms.append(m)
        o_parts = []
        m_tile = jnp.zeros((Q_BLOCK, LANES), f32)
        l_tile = jnp.ones((Q_BLOCK, LANES), f32)
        for hp in range(pairs):
            cols = slice(hp * LANES, (hp + 1) * LANES)
            acc_e = _dot(ps[2 * hp], ve_sc[keys, cols])
            acc_o = _dot(ps[2 * hp + 1], vo_sc[keys, cols])
            o_parts.append(jnp.where(low, acc_e, acc_o))
            l_tile = jnp.where(lane == HEAD_DIM + hp, acc_e, jnp.where(lane == hp, acc_o, l_tile))
            m_tile = jnp.where(lane == HEAD_DIM + hp, ms[2 * hp],
                               jnp.where(lane == hp, ms[2 * hp + 1], m_tile))
        o_ref[0, rows, :] = jnp.concatenate(o_parts, axis=1).astype(bf16)
        m_ref[0, rows, :] = m_tile
        l_ref[0, rows, :] = l_tile

    qs = {j: project_q(j) for j in range(min(2, n_blocks))}
    scores = qk(0, *qs[0])
    project_v()
    for j in range(n_blocks):
        nxt = qk(j + 1, *qs[j + 1]) if j + 1 < n_blocks else None
        if j + 2 < n_blocks:
            qs[j + 2] = project_q(j + 2)
        finish(j, scores)
        scores = nxt


def _attn_kernel(x_ref, scale_ref, shift_ref, gpre_ref, wq_ref, wk_ref, wv_ref, wkvt_ref,
                 o_ref, m_ref, l_ref, kvt0_ref, kvt1_ref, kvt2_ref, h_sc, k_sc, ve_sc, vo_sc):
    g = pl.program_id(1)
    c = pl.program_id(2)
    last_step = pl.num_programs(2) - 1

    kvt_refs = (kvt0_ref, kvt1_ref, kvt2_ref)

    def step(pattern, step_index):
        _attn_step(step_index, last_step, wq_ref, wk_ref, wv_ref, wkvt_ref, o_ref, m_ref, l_ref,
                   kvt_refs[pattern], h_sc, k_sc, ve_sc, vo_sc,
                   dil=DILATIONS[pattern], window=WINDOWS[pattern])

    @pl.when(jnp.logical_and(g == 0, c == 0))
    def _():
        zeros = jnp.zeros((Q_BLOCK, QW), bf16)
        k_sc[0:Q_BLOCK, :] = zeros
        ve_sc[0:Q_BLOCK, :] = zeros
        vo_sc[0:Q_BLOCK, :] = zeros
        for rb in range(SEQ // NORM_ROWS):
            rs = slice(rb * NORM_ROWS, (rb + 1) * NORM_ROWS)
            hb = _norm_mod(x_ref[0, rs, :], gpre_ref[...], scale_ref[0], shift_ref[0])
            for k in range(D_MODEL // LANES):
                h_sc[k, rs, :] = hb[:, k * LANES:(k + 1) * LANES]
        step(0, jnp.int32(0))

    for pattern in range(len(DILATIONS)):
        @pl.when(jnp.logical_and(g == pattern, jnp.logical_or(pattern > 0, c > 0)))
        def _(pattern=pattern):
            step(pattern, c)


def _attn_prompt(x, mod, g_pre, w_in_b, w_kvt):
    b = x.shape[0]
    n_pat = len(DILATIONS)
    steps = SEQ // ROWS_PER_STEP
    full_window = WINDOWS.index(SEQ)
    per_b = lambda i, g, c: (i, 0, 0)
    step_rows = lambda i, g, c: (g, i, c, 0)

    def kvt_spec(pattern):
        if WINDOWS[pattern] == SEQ:
            return pl.BlockSpec((1, 2 * QW, ROWS_PER_STEP),
                                lambda i, g, c: (i, 0, jnp.where(g == full_window, c, 0)))
        return pl.BlockSpec((1, 2 * QW, WINDOWS[pattern]), per_b)

    return pl.pallas_call(
        _attn_kernel,
        grid=(b, n_pat, steps),
        in_specs=[
            pl.BlockSpec((1, SEQ, D_MODEL), per_b),
            pl.BlockSpec((1, 1, D_MODEL), lambda i, g, c: (i, 0, MOD_SCALE)),
            pl.BlockSpec((1, 1, D_MODEL), lambda i, g, c: (i, 0, MOD_SHIFT)),
            pl.BlockSpec((1, D_MODEL), lambda i, g, c: (0, 0)),
            pl.BlockSpec((D_MODEL, QW), lambda i, g, c: (0, Q_COL0 // QW + g)),
            pl.BlockSpec((D_MODEL, QW), lambda i, g, c: (0, K_COL0 // QW + g)),
            pl.BlockSpec((D_MODEL, QW), lambda i, g, c: (0, V_COL0 // QW + g)),
            pl.BlockSpec((1, 2 * QW, D_MODEL), lambda i, g, c: (g, 0, 0)),
        ],
        out_specs=[
            pl.BlockSpec((None, 1, ROWS_PER_STEP, QW), step_rows),
            pl.BlockSpec((None, 1, ROWS_PER_STEP, LANES), step_rows),
            pl.BlockSpec((None, 1, ROWS_PER_STEP, LANES), step_rows),
        ] + [kvt_spec(p) for p in range(n_pat)],
        out_shape=[
            jax.ShapeDtypeStruct((n_pat, b, SEQ, QW), bf16),
            jax.ShapeDtypeStruct((n_pat, b, SEQ, LANES), f32),
            jax.ShapeDtypeStruct((n_pat, b, SEQ, LANES), f32),
        ] + [jax.ShapeDtypeStruct((b, 2 * QW, win), f32) for win in WINDOWS],
        scratch_shapes=[pltpu.VMEM((D_MODEL // LANES, SEQ, LANES), f32),
                        pltpu.VMEM((SEQ + Q_BLOCK, QW), bf16), pltpu.VMEM((SEQ + Q_BLOCK, QW), bf16),
                        pltpu.VMEM((SEQ + Q_BLOCK, QW), bf16)],
        name="attn_p",
        compiler_params=pltpu.CompilerParams(
            dimension_semantics=("arbitrary", "arbitrary", "arbitrary"), vmem_limit_bytes=VMEM_LIMIT),
    )(x, mod, mod, g_pre, w_in_b, w_in_b, w_in_b, w_kvt)


def _qkv_s_kernel(x_ref, scale_ref, shift_ref, gpre_ref, w_ref, o_ref, *kv_refs):
    h = _norm_mod(x_ref[...], gpre_ref[...], scale_ref[...], shift_ref[...]).astype(bf16)
    tile = _dot(h, w_ref[...])
    o_ref[...] = tile
    for half in range(2):
        @pl.when(pl.program_id(0) == 1 + half)
        def _(half=half):
            for g, kv_ref in enumerate(kv_refs):
                kv_ref[:, half * QW:(half + 1) * QW] = tile[:, g * QW:(g + 1) * QW]


def _qkv_sample(x, scale, shift, g_pre, w_in_b):
    n = x.shape[0]
    n_pat = len(DILATIONS)
    cols = 3 * n_pat * QW
    tile = n_pat * QW
    first = Q_COL0 // tile
    return pl.pallas_call(
        _qkv_s_kernel,
        grid=(cols // tile,),
        in_specs=[
            pl.BlockSpec((n, D_MODEL), lambda j: (0, 0)),
            pl.BlockSpec((n, D_MODEL), lambda j: (0, 0)),
            pl.BlockSpec((n, D_MODEL), lambda j: (0, 0)),
            pl.BlockSpec((1, D_MODEL), lambda j: (0, 0)),
            pl.BlockSpec((D_MODEL, tile), lambda j: (0, first + j)),
        ],
        out_specs=[pl.BlockSpec((n, tile), lambda j: (0, j))]
        + [pl.BlockSpec((n, 2 * QW), lambda j: (0, 0))] * n_pat,
        out_shape=[jax.ShapeDtypeStruct((n, cols), f32)]
        + [jax.ShapeDtypeStruct((n, 2 * QW), f32)] * n_pat,
        name="qkv_s",
        compiler_params=pltpu.CompilerParams(
            dimension_semantics=("arbitrary",), vmem_limit_bytes=VMEM_LIMIT),
    )(x, scale, shift, g_pre, w_in_b)


T_NEW = 8


def _sample_attention(qkv_ref, c0_ref, c1_ref, c2_ref, o_ref, new_sc):
    qkv = qkv_ref[0]
    new_sc[0:T_NEW, :] = qkv[:, 3 * QW:]
    q_all = qkv[:, :3 * QW] * (HEAD_DIM ** -0.5)

    lane8 = lax.broadcasted_iota(jnp.int32, (2 * T_NEW, LANES), 1)
    low = lane8 < HEAD_DIM
    t_idx = lax.broadcasted_iota(jnp.int32, (2 * T_NEW, LANES), 0) % T_NEW
    caches = (c0_ref, c1_ref, c2_ref)
    pairs = range(N_HEADS // 2)
    pats = range(len(DILATIONS))

    def pair_cols(g, hp, base=0):
        return slice(base + g * QW + hp * LANES, base + g * QW + (hp + 1) * LANES)

    s_cache, s_new = {}, {}
    for hp in pairs:
        for g in pats:
            q2 = q_all[:, pair_cols(g, hp)]
            q_st = jnp.concatenate([jnp.where(low[:T_NEW], q2, 0.0), jnp.where(low[:T_NEW], 0.0, q2)],
                                   axis=0).astype(bf16)
            kt = caches[g][0, hp * LANES:(hp + 1) * LANES, :].astype(bf16)
            k_new = new_sc[:, pair_cols(g, hp)].astype(bf16)
            s_cache[hp, g] = _dot(q_st, kt)
            s_new[hp, g] = _dot_nt(q_st, k_new)

    probs = {}
    for hp in pairs:
        m = jnp.full((2 * T_NEW, 1), NEG, f32)
        for g, dil in enumerate(DILATIONS):
            rows = WINDOWS[g]
            rho = lax.broadcasted_iota(jnp.int32, (2 * T_NEW, rows), 1)
            tq = lax.broadcasted_iota(jnp.int32, (2 * T_NEW, rows), 0) % T_NEW
            ok_c = jnp.logical_and(rho >= tq, ((rho - tq) & (dil - 1)) == 0)
            ok_n = jnp.logical_and(lane8 <= t_idx, ((t_idx - lane8) & (dil - 1)) == 0)
            s_cache[hp, g] = jnp.where(ok_c, s_cache[hp, g], NEG)
            s_new[hp, g] = jnp.where(ok_n, s_new[hp, g], NEG)
            m = jnp.maximum(m, jnp.maximum(jnp.max(s_cache[hp, g], axis=-1, keepdims=True),
                                           jnp.max(s_new[hp, g], axis=-1, keepdims=True)))
        l = jnp.zeros((2 * T_NEW, 1), f32)
        for g in pats:
            p_c = jnp.exp(s_cache[hp, g] - m)
            p_n = jnp.exp(s_new[hp, g] - m)
            l = l + jnp.sum(p_c, axis=-1, keepdims=True) + jnp.sum(p_n, axis=-1, keepdims=True)
            probs[hp, g] = (p_c.astype(bf16), p_n.astype(bf16))
        probs[hp] = l

    o_parts = []
    for hp in pairs:
        acc = jnp.zeros((2 * T_NEW, LANES), f32)
        for g in pats:
            vt = caches[g][0, QW + hp * LANES: QW + (hp + 1) * LANES, :].astype(bf16)
            v_new = new_sc[:, pair_cols(g, hp, 3 * QW)].astype(bf16)
            p_c, p_n = probs[hp, g]
            acc = acc + _dot_nt(p_c, vt) + _dot(p_n, v_new)
        out = acc / probs[hp]
        o_parts.append(jnp.where(low[:T_NEW], out[:T_NEW], out[T_NEW:]))
    o_ref[0] = jnp.concatenate(o_parts, axis=1)


TILE = 512


def _expand_heads(w):
    r = lax.broadcasted_iota(jnp.int32, (2 * LANES, QW), 0) % LANES
    head = lax.broadcasted_iota(jnp.int32, (2 * LANES, QW), 1) // HEAD_DIM
    expand = (r == head // 2 + HEAD_DIM * (1 - head % 2)).astype(bf16)
    hi = w.astype(bf16)
    lo = (w - hi.astype(f32)).astype(bf16)
    return _dot(jnp.concatenate([hi, lo], axis=1), expand)


def _rest_kernel(*refs, sample, sample_every=None, n_seq=None):
    if sample:
        (x_ref, scale_ref, shift_ref, gate_ref, gpre_ref, gpost_ref, lng_ref, lnb_ref,
         wa_ref, wzb_ref, wg_ref, wsp_ref, bsp_ref, wpa_ref, wpb_ref, wout_ref,
         attn_ref, y_ref, vn_ref) = refs
        x = x_ref[...]
        scale, shift, gate = scale_ref[...], shift_ref[...], gate_ref[...]
    else:
        (x_ref, scale_ref, shift_ref, gate_ref, gpre_ref, gpost_ref, lng_ref, lnb_ref,
         wa_ref, wzb_ref, wg_ref, wsp_ref, bsp_ref, wpa_ref, wpb_ref, wout_ref,
         o1_ref, m1_ref, l1_ref, o4_ref, m4_ref, l4_ref, o16_ref, m16_ref, l16_ref,
         qkv_s_ref, c0_hbm, c1_hbm, c2_hbm, y_ref, attn_s_ref,
         o4_sc, m4_sc, l4_sc, o16_sc, m16_sc, l16_sc, new_sc, c0_buf, c1_buf, c2_buf, cache_sem) = refs
        x = x_ref[0]
        scale, shift, gate = scale_ref[0], shift_ref[0], gate_ref[0]

        step = pl.program_id(0) * pl.num_programs(1) + pl.program_id(1)
        seq = step // sample_every

        def cache_copies(s):
            return [pltpu.make_async_copy(hbm.at[pl.ds(s, 1)], buf, cache_sem.at[k])
                    for k, (hbm, buf) in enumerate(((c0_hbm, c0_buf), (c1_hbm, c1_buf), (c2_hbm, c2_buf)))]

        @pl.when(step == 0)
        def _():
            new_sc[...] = jnp.zeros_like(new_sc)
            for cp in cache_copies(0):
                cp.start(priority=1)

        @pl.when(step % sample_every == 0)
        def _():
            for cp in cache_copies(seq):
                cp.wait()
            _sample_attention(qkv_s_ref, c0_buf, c1_buf, c2_buf, attn_s_ref, new_sc)

            @pl.when(seq + 1 < n_seq)
            def _():
                for cp in cache_copies(seq + 1):
                    cp.start(priority=1)

    rows = x.shape[0]

    if sample:
        attn = attn_ref[...]
    else:
        for osc, msc, lsc, oref, mref, lref, dil in (
                (o4_sc, m4_sc, l4_sc, o4_ref, m4_ref, l4_ref, DILATIONS[1]),
                (o16_sc, m16_sc, l16_sc, o16_ref, m16_ref, l16_ref, DILATIONS[2])):
            n = rows // dil
            for r in range(dil):
                o_r = oref[0, r].astype(f32)
                for k in range(QW // LANES):
                    osc[k, pl.ds(r, n, stride=dil), :] = o_r[:, k * LANES:(k + 1) * LANES]
                msc[pl.ds(r, n, stride=dil), :] = mref[0, r]
                lsc[pl.ds(r, n, stride=dil), :] = lref[0, r]
        ms = (m1_ref[0], m4_sc[...], m16_sc[...])
        ls = (l1_ref[0], l4_sc[...], l16_sc[...])
        unchunk = lambda sc: jnp.concatenate([sc[k] for k in range(QW // LANES)], axis=1)
        outs = (o1_ref[0].astype(f32), unchunk(o4_sc), unchunk(o16_sc))
        m_all = jnp.maximum(jnp.maximum(ms[0], ms[1]), ms[2])
        ws = [jnp.exp(m - m_all) for m in ms]
        den = ws[0] * ls[0] + ws[1] * ls[1] + ws[2] * ls[2]
        attn = sum(_expand_heads(w / den) * o for w, o in zip(ws, outs))

    h = _norm_mod(x, gpre_ref[...], scale, shift).astype(bf16)

    pa = _dot(h, wa_ref[...])
    u_a, v_a, z_a = pa[:, :D_MODEL], pa[:, D_MODEL:2 * D_MODEL], pa[:, 2 * D_MODEL:]
    mu = jnp.mean(v_a, axis=-1, keepdims=True)
    cen = v_a - mu
    var = jnp.mean(cen * cen, axis=-1, keepdims=True)
    v_n = cen * lax.rsqrt(var + EPS) * lng_ref[...] + lnb_ref[...]
    v_nb = v_n.astype(bf16)
    if sample:
        vn_ref[...] = v_n
        pick = (lax.broadcasted_iota(jnp.int32, (rows, CHUNK), 1)
                == lax.broadcasted_iota(jnp.int32, (rows, CHUNK), 0) % T_NEW).astype(bf16)
        same_seq = (lax.broadcasted_iota(jnp.int32, (rows, rows), 0) // T_NEW
                    == lax.broadcasted_iota(jnp.int32, (rows, rows), 1) // T_NEW)
        cols = []
        for g in range(A_GROUPS):
            tiled = _dot_nt(_dot(pick, wsp_ref[g]).astype(bf16), pick)
            w_blk = jnp.where(same_seq, tiled, 0.0).astype(bf16)
            cols.append(_dot(w_blk, v_nb[:, g * LANES:(g + 1) * LANES]) + bsp_ref[:, g:g + 1])
        zs = jnp.concatenate(cols, axis=1)
    else:
        n_ck = rows // CHUNK
        per_group = []
        for g in range(A_GROUPS):
            rhs = jnp.concatenate(
                [v_nb[ck * CHUNK:(ck + 1) * CHUNK, g * LANES:(g + 1) * LANES] for ck in range(n_ck)], axis=1)
            per_group.append(_dot(wsp_ref[g], rhs) + bsp_ref[:, g:g + 1])
        zs = jnp.concatenate(
            [jnp.concatenate([pg[:, ck * LANES:(ck + 1) * LANES] for pg in per_group], axis=1)
             for ck in range(n_ck)], axis=0)
    y_a = u_a * zs * _silu(z_a)

    z_b = _dot(h, wzb_ref[...])
    y_b = (attn * _silu(z_b)).astype(bf16)
    gl = _dot(h, wg_ref[...])
    p_b = _dot(y_b, wpb_ref[...])
    p_a = _dot(y_a.astype(bf16), wpa_ref[...])
    merged = jax.nn.sigmoid(gl[:, :D_MODEL]) * p_a + jax.nn.sigmoid(gl[:, D_MODEL:]) * p_b
    out = _dot(merged.astype(bf16), wout_ref[...])
    normed = out * lax.rsqrt(jnp.mean(out * out, axis=-1, keepdims=True) + EPS) * gpost_ref[...]
    y = x + gate * normed
    if sample:
        y_ref[...] = y
    else:
        y_ref[0] = y


def _full(shape):
    nd = len(shape)
    return pl.BlockSpec(shape, lambda *_: (0,) * nd)


def _weight_specs(weights):
    in_proj = [pl.BlockSpec((D_MODEL, 3 * D_MODEL), lambda *_: (0, 0)),
               pl.BlockSpec((D_MODEL, QW), lambda *_: (0, ZB_COL0 // QW)),
               pl.BlockSpec((D_MODEL, 2 * D_MODEL), lambda *_: (0, GATE_COL0 // (2 * D_MODEL)))]
    return in_proj + [_full(w.shape) for w in weights[3:]]


def _rest_prompt(x, mod, vecs, weights, attn_parts, qkv_s, caches):
    b = x.shape[0]
    n_pat = len(DILATIONS)
    tiles = SEQ // TILE
    n_seq = qkv_s.shape[0]
    sample_every, rem = divmod(b * tiles, n_seq)
    assert rem == 0 and tiles % sample_every == 0, "sample sequences must tile the grid evenly"
    seq_of = lambda i, c: (i * (tiles // sample_every) + c // sample_every, 0, 0)
    tile3 = lambda i, c: (i, c, 0)
    part_args, part_specs, scratch = [], [], []
    for pattern, dil in enumerate(DILATIONS):
        for a in attn_parts:
            lanes = a.shape[-1]
            if dil == 1:
                part_args.append(a)
                part_specs.append(pl.BlockSpec((None, 1, TILE, lanes),
                                               lambda i, c, p=pattern: (p, i, c, 0)))
            else:
                part_args.append(a.reshape(n_pat, b, dil, SEQ // dil, lanes))
                part_specs.append(pl.BlockSpec((None, 1, dil, TILE // dil, lanes),
                                               lambda i, c, p=pattern: (p, i, 0, c, 0)))
        if dil != 1:
            scratch += [pltpu.VMEM((QW // LANES, TILE, LANES), f32),
                        pltpu.VMEM((TILE, LANES), f32), pltpu.VMEM((TILE, LANES), f32)]
    in_specs = (
        [pl.BlockSpec((1, TILE, D_MODEL), tile3)]
        + [pl.BlockSpec((1, 1, D_MODEL), lambda i, c, col=col: (i, 0, col))
           for col in (MOD_SCALE, MOD_SHIFT, MOD_GATE)]
        + [_full(v.shape) for v in vecs] + _weight_specs(weights) + part_specs
        + [pl.BlockSpec((1, T_NEW, 9 * QW), seq_of)]
        + [pl.BlockSpec(memory_space=pl.ANY) for _ in WINDOWS])
    scratch.append(pltpu.VMEM((LANES, 6 * QW), f32))
    scratch += [pltpu.VMEM((1, 2 * QW, win), f32) for win in WINDOWS]
    scratch.append(pltpu.SemaphoreType.DMA((len(WINDOWS),)))
    return pl.pallas_call(
        functools.partial(_rest_kernel, sample=False, sample_every=sample_every, n_seq=n_seq),
        grid=(b, tiles),
        in_specs=in_specs,
        out_specs=[pl.BlockSpec((1, TILE, D_MODEL), tile3), pl.BlockSpec((1, T_NEW, QW), seq_of)],
        out_shape=[jax.ShapeDtypeStruct((b, SEQ, D_MODEL), f32),
                   jax.ShapeDtypeStruct((n_seq, T_NEW, QW), f32)],
        scratch_shapes=scratch,
        name="rest_p",
        compiler_params=pltpu.CompilerParams(
            dimension_semantics=("arbitrary", "arbitrary"), vmem_limit_bytes=VMEM_LIMIT),
    )(x, mod, mod, mod, *vecs, *weights, *part_args, qkv_s, *caches)


def _rest_sample(x, scale, shift, gate, vecs, weights, attn):
    n = x.shape[0]
    args = (x, scale, shift, gate, *vecs, *weights, attn)
    in_specs = ([_full(a.shape) for a in (x, scale, shift, gate, *vecs)] + _weight_specs(weights)
                + [_full(attn.shape)])
    return pl.pallas_call(
        functools.partial(_rest_kernel, sample=True),
        in_specs=in_specs,
        out_specs=[_full((n, D_MODEL)), _full((n, D_MODEL))],
        out_shape=[jax.ShapeDtypeStruct((n, D_MODEL), f32), jax.ShapeDtypeStruct((n, D_MODEL), f32)],
        grid=(1,),
        name="rest_s",
        compiler_params=pltpu.CompilerParams(
            dimension_semantics=("arbitrary",), vmem_limit_bytes=VMEM_LIMIT),
    )(*args)


def kernel(x_prompt, x_sample, cache_kv_w128, cache_kv_w512, cache_kv_w2048, c_prompt, c_sample, w_cond, b_cond, g_pre, w_in, ln_v_g, ln_v_b, w_spatial, b_spatial, w_proj_a, w_proj_b, w_out, g_post):
    assert w_in.shape[0] == 1, "single layer"
    bp, seq, _ = x_prompt.shape
    bs, t_new, _ = x_sample.shape
    assert seq == SEQ and t_new == T_NEW

    w_in_b = w_in[0].astype(bf16)
    w_kvt = jnp.transpose(w_in[0][:, K_COL0:ZB_COL0].reshape(D_MODEL, 2, 3, QW), (2, 1, 3, 0))
    w_kvt = w_kvt.reshape(3, 2 * QW, D_MODEL).astype(bf16)
    causal = jnp.tril(jnp.ones((CHUNK, CHUNK), bool))
    w_sp = jnp.where(causal[None], w_spatial[0], 0.0)
    w_sp_p = w_sp.astype(bf16)
    n_s = bs * T_NEW
    b_sp_p = b_spatial[0].T
    b_sp_s = jnp.tile(b_spatial[0][:, :T_NEW].T, (bs, 1))
    weights_tail = (w_proj_a[0].astype(bf16), w_proj_b[0].astype(bf16), w_out[0].astype(bf16))
    vecs = (g_pre, g_post, ln_v_g, ln_v_b)

    mod = _cond(jnp.concatenate([c_prompt, c_sample], axis=0), w_cond[0], b_cond)
    shift, scale, gate = mod[:, :D_MODEL], mod[:, D_MODEL:2 * D_MODEL], mod[:, 2 * D_MODEL:]
    mod_p = mod.reshape(bp + bs, 1, 3 * D_MODEL)
    ms = lambda a: jnp.repeat(a[bp:], T_NEW, axis=0)

    xs = x_sample.reshape(n_s, D_MODEL)
    qkv_s, *kv_s = _qkv_sample(xs, ms(scale), ms(shift), g_pre, w_in_b)
    caches_t = []
    for cache, win in zip((cache_kv_w128, cache_kv_w512, cache_kv_w2048), WINDOWS):
        caches_t.append(jnp.transpose(cache[0], (0, 2, 3, 4, 1)).reshape(bs, 2 * QW, win))

    o_all, m_all, l_all, *kvt = _attn_prompt(x_prompt, mod_p, g_pre, w_in_b, w_kvt)
    in_proj = (w_in_b, w_in_b, w_in_b)
    y_p, attn_s = _rest_prompt(x_prompt, mod_p, vecs,
                               in_proj + (w_sp_p, b_sp_p) + weights_tail, (o_all, m_all, l_all),
                               qkv_s.reshape(bs, T_NEW, 9 * QW), caches_t)
    kv_p = [jnp.transpose(a.reshape(bp, 2, N_HEADS, HEAD_DIM, win), (0, 4, 1, 2, 3))[None]
            for a, win in zip(kvt, WINDOWS)]

    attn_s = attn_s.reshape(n_s, QW)
    y_s, v_n_s = _rest_sample(xs, ms(scale), ms(shift), ms(gate), vecs,
                              in_proj + (w_sp_p, b_sp_s) + weights_tail, attn_s)
    kv_s = [a.reshape(1, bs, T_NEW, 2, N_HEADS, HEAD_DIM) for a in kv_s]

    return (y_p, y_s.reshape(bs, T_NEW, D_MODEL), kv_p[0], kv_p[1], kv_p[2],
            kv_s[0], kv_s[1], kv_s[2], v_n_s.reshape(1, bs, T_NEW, D_MODEL))
```
